```python
import math
import jax, jax.numpy as jnp
from jax import lax
import numpy as np

D_MODEL = 2048
BATCH = 4
SEQ = 4096
DEPTH = 2

EPS = 1e-6
F32 = jnp.float32
N_GROUPS = 4
GROUP_WIDTH = D_MODEL // N_GROUPS
CHUNK = 64
GLA_HEADS = 4
GLA_DV = GROUP_WIDTH // GLA_HEADS
GLA_DK = GLA_DV // 2
GLA_GATE_RANK = 16
GLA_GATE_NORMALIZER = 16.0
RET_HEADS = 4
RET_DK = GROUP_WIDTH // RET_HEADS
RET_DV = GROUP_WIDTH // RET_HEADS
RET_ROT_BASE = 10000.0
SWA_Q_HEADS = 8
SWA_KV_HEADS = 2
SWA_HD = GROUP_WIDTH // SWA_Q_HEADS
WINDOW = 128
ROPE_THETA = 500000.0
ROPE_DIM = SWA_HD // 4
HG_HEADS = 4
HG_EXPAND = GROUP_WIDTH // HG_HEADS
HG_DV = GROUP_WIDTH // HG_HEADS
D_FF = ((8 * D_MODEL + 3 * 256 - 1) // (3 * 256)) * 256
IN_SPLITS = (
    GLA_HEADS * GLA_DK, GLA_HEADS * GLA_DK, GROUP_WIDTH, GROUP_WIDTH, GLA_GATE_RANK,
    RET_HEADS * RET_DK, RET_HEADS * RET_DK, GROUP_WIDTH, GROUP_WIDTH,
    SWA_Q_HEADS * SWA_HD, SWA_KV_HEADS * SWA_HD, SWA_KV_HEADS * SWA_HD,
    HG_HEADS * HG_EXPAND, HG_HEADS * HG_EXPAND, GROUP_WIDTH, GROUP_WIDTH,
)
IN_WIDTH = sum(IN_SPLITS)

kernel_name = "hymba_style_gla_ret_swa_hgrn2_adaln"


def rms_norm(x, w):
    xf = x.astype(F32)
    y = xf * lax.rsqrt(jnp.mean(xf * xf, axis=-1, keepdims=True) + EPS)
    return (y * w.astype(F32)).astype(x.dtype)


def head_rms_norm(o, w=None):
    of = o.astype(F32)
    y = of * lax.rsqrt(jnp.mean(of * of, axis=-1, keepdims=True) + EPS)
    if w is not None:
        y = y * w.astype(F32)
    return y.astype(o.dtype)


def rotate(x, positions, inv_freq):
    half = inv_freq.shape[0]
    ang = positions.astype(F32)[:, :, None] * inv_freq[None, None, :]
    cos = jnp.cos(ang)[:, :, None, :]
    sin = jnp.sin(ang)[:, :, None, :]
    xf = x.astype(F32)
    x1 = xf[..., :half]
    x2 = xf[..., half:2 * half]
    out = jnp.concatenate([x1 * cos - x2 * sin, x2 * cos + x1 * sin, xf[..., 2 * half:]], axis=-1)
    return out.astype(x.dtype)


def gated_chunk_scan(q, k, v, log_g):
    B, S, H, dk = q.shape
    dv = v.shape[-1]
    n = S // CHUNK

    def to_chunks(t):
        return t.astype(F32).reshape(B, n, CHUNK, H, t.shape[-1]).transpose(1, 0, 3, 2, 4)

    qc, kc, vc, gc = to_chunks(q), to_chunks(k), to_chunks(v), to_chunks(log_g)
    causal = jnp.tril(jnp.ones((CHUNK, CHUNK), dtype=bool))[:, :, None]

    def step(state, inp):
        qi, ki, vi, gi = inp
        b = jnp.cumsum(gi, axis=-2)
        diff = b[:, :, :, None, :] - b[:, :, None, :, :]
        decay = jnp.exp(jnp.where(causal, diff, -jnp.inf))
        attn = jnp.einsum('bhik,bhjk,bhijk->bhij', qi, ki, decay)
        o = (jnp.einsum('bhij,bhjv->bhiv', attn, vi)
             + jnp.einsum('bhik,bhkv->bhiv', qi * jnp.exp(b), state))
        b_last = b[:, :, -1, :]
        k_dec = ki * jnp.exp(b_last[:, :, None, :] - b)
        state = state * jnp.exp(b_last)[..., None] + jnp.einsum('bhjk,bhjv->bhkv', k_dec, vi)
        return state, o

    _, o = lax.scan(step, jnp.zeros((B, H, dk, dv), F32), (qc, kc, vc, gc))
    return o.transpose(1, 0, 3, 2, 4).reshape(B, S, H, dv).astype(v.dtype)


def retention_chunkwise(q, k, v, log_gamma):
    B, S, H, dk = q.shape
    dv = v.shape[-1]
    n = S // CHUNK

    def to_chunks(t):
        return t.astype(F32).reshape(B, n, CHUNK, H, t.shape[-1]).transpose(0, 3, 1, 2, 4)

    qc, kc, vc = to_chunks(q), to_chunks(k), to_chunks(v)
    pos = jnp.arange(CHUNK, dtype=F32)
    rel = pos[:, None] - pos[None, :]
    dmask = jnp.exp(jnp.where(rel >= 0, log_gamma[:, None, None] * rel, -jnp.inf))
    scores = jnp.einsum('bhnik,bhnjk->bhnij', qc, kc) * dmask[None, :, None]
    inner = jnp.einsum('bhnij,bhnjv->bhniv', scores, vc)
    zeta = jnp.exp(log_gamma[:, None] * (CHUNK - 1 - pos))
    chunk_kv = jnp.einsum('bhnjk,hj,bhnjv->nbhkv', kc, zeta, vc)
    chunk_decay = jnp.exp(log_gamma * CHUNK)[None, :, None, None]

    def step(r, kv):
        return r * chunk_decay + kv, r

    _, r_prev = lax.scan(step, jnp.zeros((B, H, dk, dv), F32), chunk_kv)
    xi = jnp.exp(log_gamma[:, None] * (pos + 1.0))
    cross = jnp.einsum('bhnik,hi,nbhkv->bhniv', qc, xi, r_prev)
    o = (inner + cross).transpose(0, 2, 3, 1, 4).reshape(B, S, H, dv)
    return o.astype(v.dtype)


def sliding_window_sink_attention(q, k, v, sinks):
    B, S, Hq, hd = q.shape
    Hkv = k.shape[2]
    G = Hq // Hkv
    nb = S // WINDOW
    qb = q.reshape(B, nb, WINDOW, Hkv, G, hd)
    kb = k.reshape(B, nb, WINDOW, Hkv, hd)
    vb = v.reshape(B, nb, WINDOW, Hkv, hd)

    def with_prev(t):
        prev = jnp.concatenate([jnp.zeros_like(t[:, :1]), t[:, :-1]], axis=1)
        return jnp.concatenate([prev, t], axis=2)

    kk, vv = with_prev(kb), with_prev(vb)
    s = jnp.einsum('bnqhgd,bnkhd->bnhgqk', qb, kk).astype(F32) * (hd ** -0.5)
    qpos = jnp.arange(WINDOW) + WINDOW
    kpos = jnp.arange(2 * WINDOW)
    rel = qpos[:, None] - kpos[None, :]
    band = (rel >= 0) & (rel < WINDOW)
    has_prev = (jnp.arange(nb)[:, None] > 0) | (kpos[None, :] >= WINDOW)
    mask = band[None] & has_prev[:, None, :]
    s = jnp.where(mask[None, :, None, None], s, -jnp.inf)
    sink = jnp.broadcast_to(sinks.astype(F32).reshape(1, 1, Hkv, G, 1, 1), s.shape[:-1] + (1,))
    p = jax.nn.softmax(jnp.concatenate([s, sink], axis=-1), axis=-1)[..., :-1]
    o = jnp.einsum('bnhgqk,bnkhd->bnqhgd', p.astype(v.dtype), vv)
    return o.reshape(B, S, Hq * hd)


def mixer_groups(h, positions, w_in, gla_w2, gla_b2, gla_nw, sinks, lb, hg_nw):
    B, S, _ = h.shape
    proj = h @ w_in
    offs = np.cumsum(IN_SPLITS)[:-1].tolist()
    (a_q, a_k, a_v, a_g, a_r,
     b_q, b_k, b_v, b_g,
     c_q, c_k, c_v,
     d_q, d_f, d_i, d_g) = jnp.split(proj, offs, axis=-1)

    def heads(t, n):
        return t.reshape(B, S, n, -1)

    a_log = jax.nn.log_sigmoid((a_r @ gla_w2 + gla_b2).astype(F32)) / GLA_GATE_NORMALIZER
    a_o = gated_chunk_scan(heads(a_q, GLA_HEADS) * (GLA_DK ** -0.5), heads(a_k, GLA_HEADS),
                           heads(a_v, GLA_HEADS), heads(a_log, GLA_HEADS))
    a_out = (head_rms_norm(a_o, gla_nw) * jax.nn.silu(heads(a_g, GLA_HEADS))).reshape(B, S, GROUP_WIDTH)

    ret_inv = 1.0 / jnp.power(RET_ROT_BASE, jnp.linspace(0.0, 1.0, RET_DK // 2, dtype=F32))
    log_gamma = jnp.log1p(-jnp.power(2.0, -5.0 - jnp.arange(RET_HEADS, dtype=F32)))
    rq = rotate(heads(b_q, RET_HEADS), positions, ret_inv)
    rk = rotate(heads(b_k, RET_HEADS), positions, ret_inv) * (RET_DK ** -0.5)
    b_o = retention_chunkwise(rq, rk, heads(b_v, RET_HEADS), log_gamma)
    b_out = (head_rms_norm(b_o) * jax.nn.silu(heads(b_g, RET_HEADS))).reshape(B, S, GROUP_WIDTH)

    rope_inv = 1.0 / jnp.power(ROPE_THETA, jnp.arange(ROPE_DIM // 2, dtype=F32) / (ROPE_DIM // 2))
    sq = rotate(heads(c_q, SWA_Q_HEADS), positions, rope_inv)
    sk = rotate(heads(c_k, SWA_KV_HEADS), positions, rope_inv)
    c_out = sliding_window_sink_attention(sq, sk, heads(c_v, SWA_KV_HEADS), sinks)

    lb_h = lb.astype(F32).reshape(HG_HEADS, HG_EXPAND)
    f = lb_h + (1.0 - lb_h) * jax.nn.sigmoid(heads(d_f, HG_HEADS).astype(F32))
    hq = jax.nn.silu(heads(d_q, HG_HEADS)) * (HG_EXPAND ** -0.5)
    d_o = gated_chunk_scan(hq, 1.0 - f, heads(d_i, HG_HEADS), jnp.log(f))
    d_out = (head_rms_norm(d_o, hg_nw) * jax.nn.silu(heads(d_g, HG_HEADS))).reshape(B, S, GROUP_WIDTH)

    return jnp.concatenate([a_out, b_out, c_out, d_out], axis=-1)


def setup_inputs(seed: int = 0) -> dict:
    key = jax.random.key(seed)
    ks = jax.random.split(key, 20)

    def nrm(k, shape, scale):
        return jax.random.normal(k, shape, F32) * scale

    return {
        "x": nrm(ks[0], (BATCH, SEQ, D_MODEL), 1.0),
        "c": nrm(ks[1], (BATCH, D_MODEL), 1.0),
        "positions": jnp.broadcast_to(jnp.arange(SEQ, dtype=jnp.int32), (BATCH, SEQ)),
        "w_ada": nrm(ks[2], (DEPTH, D_MODEL, 6 * D_MODEL), 0.5 * D_MODEL ** -0.5),
        "b_ada": nrm(ks[3], (DEPTH, 6 * D_MODEL), 0.02),
        "norm1_w": 1.0 + nrm(ks[4], (DEPTH, D_MODEL), 0.02),
        "w_in": nrm(ks[5], (DEPTH, D_MODEL, IN_WIDTH), D_MODEL ** -0.5),
        "gla_gate_w2": nrm(ks[6], (DEPTH, GLA_GATE_RANK, GLA_HEADS * GLA_DK), GLA_GATE_RANK ** -0.5),
        "gla_gate_b2": nrm(ks[7], (DEPTH, GLA_HEADS * GLA_DK), 0.1),
        "gla_norm_w": 1.0 + nrm(ks[8], (DEPTH, GLA_DV), 0.02),
        "swa_sinks": nrm(ks[9], (DEPTH, SWA_Q_HEADS), 0.5),
        "hgrn_lb": nrm(ks[10], (DEPTH, HG_HEADS * HG_EXPAND), 1.0),
        "hgrn_norm_w": 1.0 + nrm(ks[11], (DEPTH, HG_DV), 0.02),
        "w_out": nrm(ks[12], (DEPTH, D_MODEL, D_MODEL), D_MODEL ** -0.5),
        "norm2_w": 1.0 + nrm(ks[13], (DEPTH, D_MODEL), 0.02),
        "w_ffn_in": nrm(ks[14], (DEPTH, D_MODEL, 2 * D_FF), D_MODEL ** -0.5),
        "w_ffn_down": nrm(ks[15], (DEPTH, D_FF, D_MODEL), D_FF ** -0.5),
        "final_norm_w": 1.0 + nrm(ks[16], (D_MODEL,), 0.02),
    }


def reference(x, c, positions, w_ada, b_ada, norm1_w, w_in, gla_gate_w2, gla_gate_b2,
              gla_norm_w, swa_sinks, hgrn_lb, hgrn_norm_w, w_out, norm2_w,
              w_ffn_in, w_ffn_down, final_norm_w):
    lb_soft = jax.nn.softmax(hgrn_lb.astype(F32), axis=0)
    lower_bounds = jnp.cumsum(lb_soft, axis=0) - lb_soft[0]
    cond = jax.nn.silu(c)
    for l in range(DEPTH):
        mod = cond @ w_ada[l] + b_ada[l]
        shift1, scale1, gate1, shift2, scale2, gate2 = [m[:, None, :] for m in jnp.split(mod, 6, axis=-1)]
        h = rms_norm(x, norm1_w[l]) * (1.0 + scale1) + shift1
        mixed = mixer_groups(h, positions, w_in[l], gla_gate_w2[l], gla_gate_b2[l], gla_norm_w[l],
                             swa_sinks[l], lower_bounds[l], hgrn_norm_w[l])
        x = x + gate1 * (mixed @ w_out[l])
        h = rms_norm(x, norm2_w[l]) * (1.0 + scale2) + shift2
        gate_up = h @ w_ffn_in[l]
        g_ff, u_ff = jnp.split(gate_up, 2, axis=-1)
        x = x + gate2 * ((jax.nn.silu(g_ff) * u_ff) @ w_ffn_down[l])
    return rms_norm(x, final_norm_w)
```

```python
import functools
import math

import jax
import jax.numpy as jnp
from jax import lax
from jax.experimental import pallas as pl
from jax.experimental.pallas import tpu as pltpu

F32 = jnp.float32
BF16 = jnp.bfloat16

D_MODEL = 2048
DEPTH = 2
EPS = 1e-6
GROUP_WIDTH = D_MODEL // 4
CHUNK = 64
SUB = 16
N_SUB = CHUNK // SUB
GLA_HEADS = 4
GLA_DV = GROUP_WIDTH // GLA_HEADS
GLA_DK = GLA_DV // 2
GLA_GATE_RANK = 16
GLA_GATE_NORMALIZER = 16.0
RET_HEADS = 4
RET_DK = GROUP_WIDTH // RET_HEADS
RET_ROT_BASE = 10000.0
SWA_Q_HEADS = 8
SWA_KV_HEADS = 2
SWA_HD = GROUP_WIDTH // SWA_Q_HEADS
WINDOW = 128
ROPE_THETA = 500000.0
ROPE_DIM = SWA_HD // 4
HG_HEADS = 4
HG_DK = GROUP_WIDTH // HG_HEADS
D_FF = ((8 * D_MODEL + 3 * 256 - 1) // (3 * 256)) * 256

LANES = 128
VMEM_LIMIT = 48 * 1024 * 1024

_SRC = dict(a_q=0, a_k=256, a_v=512, a_g=1024, a_r=1536,
            b_q=1552, b_k=2064, b_v=2576, b_g=3088,
            c_q=3600, c_k=4112, c_v=4240,
            d_q=4368, d_f=4880, d_i=5392, d_g=5904)
_DST_ORDER = (("a_q", 256), ("a_k", 256), ("a_v", 512), ("a_g", 512), ("c_q", 512),
              ("b_q", 512), ("b_k", 512), ("b_v", 512), ("b_g", 512),
              ("d_q", 512), ("d_f", 512), ("d_i", 512), ("d_g", 512),
              ("c_k", 128), ("c_v", 128), ("a_r", GLA_GATE_RANK))
PROJ_WIDTH = 6528
PROJ_TN = PROJ_WIDTH // 3


def _dst_offsets():
    offs, o = {}, 0
    for name, w in _DST_ORDER:
        offs[name] = o
        o += w
    return offs


_DST = _dst_offsets()


def _dot(a, b):
    return jnp.dot(a, b, preferred_element_type=F32)


def _dot_nt(a, b):
    return lax.dot_general(a, b, (((1,), (1,)), ((), ())), preferred_element_type=F32)


def _dot_tn(a, b):
    return lax.dot_general(a, b, (((0,), (0,)), ((), ())), preferred_element_type=F32)


def _split_bf16(x):
    hi = x.astype(BF16)
    lo = (x - hi.astype(F32)).astype(BF16)
    return hi, lo


def _dot3(a, b):
    a_hi, a_lo = _split_bf16(a)
    b_hi, b_lo = _split_bf16(b)
    return _dot(a_hi, b_hi) + (_dot(a_hi, b_lo) + _dot(a_lo, b_hi))


def _sigmoid(x):
    return 1.0 / (1.0 + jnp.exp(-x))


def _silu(x):
    return x / (1.0 + jnp.exp(-x))


def _log_sigmoid(x):
    return jnp.minimum(x, 0.0) - jnp.log1p(jnp.exp(-jnp.abs(x)))


def _rms(x):
    return x * lax.rsqrt(jnp.mean(x * x, axis=-1, keepdims=True) + EPS)


def _ada_kernel(c_ref, w_ref, b_ref, o_ref):
    cond = _silu(c_ref[...])
    o_ref[0] = _dot3(cond, w_ref[0]) + b_ref[0]


def _ada_modulation(c, w_ada, b_ada):
    depth, d, n = w_ada.shape
    rows = 8
    c_pad = jnp.zeros((rows, d), F32).at[:c.shape[0]].set(c)
    tn = 1024
    out = pl.pallas_call(
        _ada_kernel,
        out_shape=jax.ShapeDtypeStruct((depth, rows, n), F32),
        grid=(depth, n // tn),
        in_specs=[pl.BlockSpec((rows, d), lambda l, j: (0, 0)),
                  pl.BlockSpec((1, d, tn), lambda l, j: (l, 0, j)),
                  pl.BlockSpec((1, 1, tn), lambda l, j: (l, 0, j))],
        out_specs=pl.BlockSpec((1, rows, tn), lambda l, j: (l, 0, j)),
        compiler_params=pltpu.CompilerParams(
            dimension_semantics=("arbitrary", "arbitrary"), vmem_limit_bytes=VMEM_LIMIT),
        name="ada_modulation",
    )(c_pad, w_ada, b_ada.reshape(depth, 1, n))
    return out[:, :c.shape[0]]


def _modulated_norm(x, nw, shift, scale):
    return (_rms(x) * nw) * (1.0 + scale) + shift


def _in_proj_kernel(x_ref, mod_ref, nw_ref, w_ref, o_ref, h_ref):
    @pl.when(pl.program_id(1) == 0)
    def _():
        m = mod_ref[0]
        h = _modulated_norm(x_ref[...], nw_ref[...], m[0:1], m[1:2])
        h_ref[...] = h.astype(BF16)

    o_ref[...] = _dot(h_ref[...], w_ref[...])


def _in_proj(x2, mod_l, nw, w_perm, seq, tm=512):
    m_rows, d = x2.shape
    n = w_perm.shape[1]
    per_b = seq // tm
    return pl.pallas_call(
        _in_proj_kernel,
        out_shape=jax.ShapeDtypeStruct((m_rows, n), F32),
        grid=(m_rows // tm, n // PROJ_TN),
        in_specs=[pl.BlockSpec((tm, d), lambda i, j: (i, 0)),
                  pl.BlockSpec((1, 6, d), lambda i, j: (i // per_b, 0, 0)),
                  pl.BlockSpec((1, d), lambda i, j: (0, 0)),
                  pl.BlockSpec((d, PROJ_TN), lambda i, j: (0, j))],
        out_specs=pl.BlockSpec((tm, PROJ_TN), lambda i, j: (i, j)),
        scratch_shapes=[pltpu.VMEM((tm, d), BF16)],
        compiler_params=pltpu.CompilerParams(
            dimension_semantics=("arbitrary", "arbitrary"), vmem_limit_bytes=VMEM_LIMIT),
        name="in_proj",
    )(x2, mod_l, nw.reshape(1, d), w_perm)


def _chunk_cumsum(x, row):
    for s in (1, 2, 4, 8, 16, 32):
        x = x + jnp.where(row >= s, pltpu.roll(x, s, axis=0), 0.0)
    return x


def _gated_chunk(q, k, v, lg, st_ref):
    row = lax.broadcasted_iota(jnp.int32, (CHUNK, 1), 0)
    rowm = lax.broadcasted_iota(jnp.int32, (CHUNK, CHUNK), 0)
    colm = lax.broadcasted_iota(jnp.int32, (CHUNK, CHUNK), 1)
    sub_row = row & (SUB - 1)
    blk_row = rowm >> 4
    blk_col = colm >> 4
    rblk = row >> 4

    b = _chunk_cumsum(lg, row)

    a = jnp.zeros((CHUNK, CHUNK), F32)
    for d in range(SUB):
        if d == 0:
            s = jnp.sum(q * k, axis=-1, keepdims=True)
        else:
            kr = pltpu.roll(k, d, axis=0)
            br = pltpu.roll(b, d, axis=0)
            e = jnp.exp(jnp.minimum(b - br, 0.0))
            s = jnp.sum(q * kr * e, axis=-1, keepdims=True)
        a = jnp.where((colm == rowm - d) & (sub_row >= d), s, a)

    ends = [b[SUB * j + SUB - 1:SUB * j + SUB, :] for j in range(N_SUB)]
    end_own = ends[N_SUB - 1]
    for j in range(N_SUB - 2, -1, -1):
        end_own = jnp.where(rblk <= j, ends[j], end_own)
    ke = (k * jnp.exp(end_own - b)).astype(BF16)
    for j in range(N_SUB - 1):
        ql = (q * jnp.exp(jnp.minimum(b - ends[j], 0.0))).astype(BF16)
        a = jnp.where((blk_col == j) & (blk_row > j), _dot_nt(ql, ke), a)

    last = ends[N_SUB - 1]
    st = st_ref[...]
    vb = v.astype(BF16)
    o = _dot(a.astype(BF16), vb) + _dot_nt((q * jnp.exp(b)).astype(BF16), st.astype(BF16))
    kd = (k * jnp.exp(last - b)).astype(BF16)
    st_ref[...] = st * jnp.exp(last) + _dot_tn(vb, kd)
    return o


def _gla_kernel(q_ref, k_ref, v_ref, g_ref, r_ref, w2_ref, b2_ref, nw_ref, o_ref, st_ref, *, n_chunks):
    @pl.when(pl.program_id(1) == 0)
    def _():
        st_ref[...] = jnp.zeros_like(st_ref)

    w2 = w2_ref[...]
    b2 = b2_ref[...]
    nw = nw_ref[...]

    def body(c, carry):
        rows = pl.ds(pl.multiple_of(c * CHUNK, CHUNK), CHUNK)
        pre = _dot3(r_ref[rows, :], w2) + b2
        lg = _log_sigmoid(pre) * (1.0 / GLA_GATE_NORMALIZER)
        q = q_ref[rows, :] * (GLA_DK ** -0.5)
        k = k_ref[rows, :]
        v = v_ref[rows, :]
        g = g_ref[rows, :]
        for h in range(GLA_HEADS):
            ks = slice(h * GLA_DK, (h + 1) * GLA_DK)
            vs = slice(h * GLA_DV, (h + 1) * GLA_DV)
            o = _gated_chunk(q[:, ks], k[:, ks], v[:, vs], lg[:, ks], st_ref.at[h])
            o_ref[rows, vs] = ((_rms(o) * nw) * _silu(g[:, vs])).astype(BF16)
        return carry

    lax.fori_loop(0, n_chunks, body, 0)


def _hgrn_kernel(q_ref, f_ref, i_ref, g_ref, lb_ref, nw_ref, o_ref, st_ref, *, n_chunks, layer):
    @pl.when(pl.program_id(1) == 0)
    def _():
        st_ref[...] = jnp.zeros_like(st_ref)

    lbv = lb_ref[...]
    e = jnp.exp(lbv - jnp.max(lbv, axis=0, keepdims=True))
    soft = e / jnp.sum(e, axis=0, keepdims=True)
    cum = soft[0:1]
    for i in range(1, layer + 1):
        cum = cum + soft[i:i + 1]
    lb = cum - soft[0:1]
    nw = nw_ref[...]

    def body(c, carry):
        rows = pl.ds(pl.multiple_of(c * CHUNK, CHUNK), CHUNK)
        f = lb + (1.0 - lb) * _sigmoid(f_ref[rows, :])
        q = _silu(q_ref[rows, :]) * (HG_DK ** -0.5)
        k = 1.0 - f
        lg = jnp.log(f)
        v = i_ref[rows, :]
        g = g_ref[rows, :]
        for h in range(HG_HEADS):
            hs = slice(h * HG_DK, (h + 1) * HG_DK)
            o = _gated_chunk(q[:, hs], k[:, hs], v[:, hs], lg[:, hs], st_ref.at[h])
            o_ref[rows, hs] = ((_rms(o) * nw) * _silu(g[:, hs])).astype(BF16)
        return carry

    lax.fori_loop(0, n_chunks, body, 0)


def _col_spec(t, width, name, nt):
    blk = _DST[name] // width
    return pl.BlockSpec((t, width), lambda b, i: (b * nt + i, blk))


def _mixer_out_spec(t, nt):
    return pl.BlockSpec((t, GROUP_WIDTH), lambda b, i: (b * nt + i, 0))


def _mixer_params():
    return pltpu.CompilerParams(dimension_semantics=("arbitrary", "arbitrary"),
                                vmem_limit_bytes=VMEM_LIMIT)


def _gla(proj, w2_pad, b2, nw, batch, seq, t=512):
    nt = seq // t
    full = lambda shape: pl.BlockSpec(shape, lambda b, i: (0, 0))
    return pl.pallas_call(
        functools.partial(_gla_kernel, n_chunks=t // CHUNK),
        out_shape=jax.ShapeDtypeStruct((batch * seq, GROUP_WIDTH), BF16),
        grid=(batch, nt),
        in_specs=[_col_spec(t, 256, "a_q", nt), _col_spec(t, 256, "a_k", nt),
                  _col_spec(t, 512, "a_v", nt), _col_spec(t, 512, "a_g", nt),
                  _col_spec(t, LANES, "a_r", nt),
                  full((LANES, GLA_HEADS * GLA_DK)), full((1, GLA_HEADS * GLA_DK)), full((1, GLA_DV))],
        out_specs=_mixer_out_spec(t, nt),
        scratch_shapes=[pltpu.VMEM((GLA_HEADS, GLA_DV, GLA_DK), F32)],
        compiler_params=_mixer_params(),
        name="gla_mixer",
    )(proj, proj, proj, proj, proj, w2_pad, b2.reshape(1, -1), nw.reshape(1, -1))


def _hgrn(proj, hgrn_lb, nw, layer, batch, seq, t=512):
    nt = seq // t
    full = lambda shape: pl.BlockSpec(shape, lambda b, i: (0, 0))
    return pl.pallas_call(
        functools.partial(_hgrn_kernel, n_chunks=t // CHUNK, layer=layer),
        out_shape=jax.ShapeDtypeStruct((batch * seq, GROUP_WIDTH), BF16),
        grid=(batch, nt),
        in_specs=[_col_spec(t, 512, "d_q", nt), _col_spec(t, 512, "d_f", nt),
                  _col_spec(t, 512, "d_i", nt), _col_spec(t, 512, "d_g", nt),
                  full(hgrn_lb.shape), full((1, HG_DK))],
        out_specs=_mixer_out_spec(t, nt),
        scratch_shapes=[pltpu.VMEM((HG_HEADS, HG_DK, HG_DK), F32)],
        compiler_params=_mixer_params(),
        name="hgrn_mixer",
    )(proj, proj, proj, proj, hgrn_lb, nw.reshape(1, -1))


def _ret_kernel(q_ref, k_ref, v_ref, g_ref, cos_ref, sin_ref, o_ref, st_ref, *, t):
    @pl.when(pl.program_id(1) == 0)
    def _():
        st_ref[...] = jnp.zeros_like(st_ref)

    cos = cos_ref[0]
    sin = sin_ref[0]
    q = q_ref[...]
    k = k_ref[...]
    v = v_ref[...]
    g = g_ref[...]
    rowm = lax.broadcasted_iota(jnp.int32, (t, t), 0)
    colm = lax.broadcasted_iota(jnp.int32, (t, t), 1)
    rel = (rowm - colm).astype(F32)
    pos = lax.broadcasted_iota(jnp.int32, (t, RET_DK), 0).astype(F32)
    half = RET_DK // 2
    for h in range(RET_HEADS):
        lg = math.log1p(-(2.0 ** (-5.0 - h)))
        hs = slice(h * RET_DK, (h + 1) * RET_DK)
        qh = q[:, hs]
        kh = k[:, hs]
        qr = qh * cos + pltpu.roll(qh, half, axis=1) * sin
        kr = (kh * cos + pltpu.roll(kh, half, axis=1) * sin) * (RET_DK ** -0.5)
        vb = v[:, hs].astype(BF16)
        dmask = jnp.where(rel >= 0, jnp.exp(jnp.minimum(lg * rel, 0.0)), 0.0)
        scores = _dot_nt(qr.astype(BF16), kr.astype(BF16)) * dmask
        st = st_ref[h]
        xi = jnp.exp(lg * (pos + 1.0))
        o = _dot(scores.astype(BF16), vb) + _dot((qr * xi).astype(BF16), st.astype(BF16))
        zeta = jnp.exp(lg * (t - 1.0 - pos))
        st_ref[h] = st * math.exp(lg * t) + _dot_tn((kr * zeta).astype(BF16), vb)
        o_ref[:, hs] = (_rms(o) * _silu(g[:, hs])).astype(BF16)


def _ret(proj, cos, sin, batch, seq, t=256):
    nt = seq // t
    tab = pl.BlockSpec((1, t, RET_DK), lambda b, i: (b, i, 0))
    return pl.pallas_call(
        functools.partial(_ret_kernel, t=t),
        out_shape=jax.ShapeDtypeStruct((batch * seq, GROUP_WIDTH), BF16),
        grid=(batch, nt),
        in_specs=[_col_spec(t, 512, "b_q", nt), _col_spec(t, 512, "b_k", nt),
                  _col_spec(t, 512, "b_v", nt), _col_spec(t, 512, "b_g", nt), tab, tab],
        out_specs=_mixer_out_spec(t, nt),
        scratch_shapes=[pltpu.VMEM((RET_HEADS, RET_DK, RET_DK), F32)],
        compiler_params=_mixer_params(),
        name="ret_mixer",
    )(proj, proj, proj, proj, cos, sin)


def _rope_partial(x, c, s_lo, s_hi):
    n = x.shape[-1]
    half = ROPE_DIM // 2
    return x * c + pltpu.roll(x, n - half, axis=1) * s_lo + pltpu.roll(x, half, axis=1) * s_hi


def _swa_kernel(q_ref, kc_ref, vc_ref, kp_ref, vp_ref, c_ref, slo_ref, shi_ref,
                cp_ref, slop_ref, ship_ref, sink_ref, o_ref):
    w = WINDOW
    has_prev = pl.program_id(1) > 0
    c = c_ref[0]
    s_lo = slo_ref[0]
    s_hi = shi_ref[0]
    tile4 = lambda a: jnp.concatenate([a] * (GROUP_WIDTH // LANES), axis=1)
    q = _rope_partial(q_ref[...], tile4(c), tile4(s_lo), tile4(s_hi)) * (SWA_HD ** -0.5)
    kc = _rope_partial(kc_ref[...], c, s_lo, s_hi)
    kp = _rope_partial(kp_ref[...], cp_ref[0], slop_ref[0], ship_ref[0])
    vc = vc_ref[...].astype(BF16)
    vp = vp_ref[...].astype(BF16)
    rowm = lax.broadcasted_iota(jnp.int32, (w, w), 0)
    colm = lax.broadcasted_iota(jnp.int32, (w, w), 1)
    cur_ok = colm <= rowm
    prev_ok = (colm > rowm) & has_prev
    sinks = sink_ref[...]
    group = SWA_Q_HEADS // SWA_KV_HEADS
    for h in range(SWA_Q_HEADS):
        kv = h // group
        qs = slice(h * SWA_HD, (h + 1) * SWA_HD)
        ks = slice(kv * SWA_HD, (kv + 1) * SWA_HD)
        qh = q[:, qs].astype(BF16)
        s_c = jnp.where(cur_ok, _dot_nt(qh, kc[:, ks].astype(BF16)), -jnp.inf)
        s_p = jnp.where(prev_ok, _dot_nt(qh, kp[:, ks].astype(BF16)), -jnp.inf)
        sink = sinks[:, h:h + 1]
        m = jnp.maximum(jnp.maximum(jnp.max(s_c, axis=-1, keepdims=True),
                                    jnp.max(s_p, axis=-1, keepdims=True)), sink)
        e_c = jnp.exp(s_c - m)
        e_p = jnp.exp(s_p - m)
        denom = (jnp.sum(e_c, axis=-1, keepdims=True) + jnp.sum(e_p, axis=-1, keepdims=True)
                 + jnp.exp(sink - m))
        o = _dot(e_c.astype(BF16), vc[:, ks]) + _dot(e_p.astype(BF16), vp[:, ks])
        o_ref[:, qs] = (o / denom).astype(BF16)


def _swa(proj, tabs, sinks, batch, seq):
    w = WINDOW
    nt = seq // w
    kv_w = SWA_KV_HEADS * SWA_HD
    cur = lambda name: pl.BlockSpec((w, kv_w), lambda b, i: (b * nt + i, _DST[name] // kv_w))
    prev = lambda name: pl.BlockSpec(
        (w, kv_w), lambda b, i: (b * nt + jnp.maximum(i - 1, 0), _DST[name] // kv_w))
    tab_c = pl.BlockSpec((1, w, LANES), lambda b, i: (b, i, 0))
    tab_p = pl.BlockSpec((1, w, LANES), lambda b, i: (b, jnp.maximum(i - 1, 0), 0))
    c, s_lo, s_hi = tabs
    return pl.pallas_call(
        _swa_kernel,
        out_shape=jax.ShapeDtypeStruct((batch * seq, GROUP_WIDTH), BF16),
        grid=(batch, nt),
        in_specs=[_col_spec(w, 512, "c_q", nt), cur("c_k"), cur("c_v"), prev("c_k"), prev("c_v"),
                  tab_c, tab_c, tab_c, tab_p, tab_p, tab_p,
                  pl.BlockSpec((1, SWA_Q_HEADS), lambda b, i: (0, 0))],
        out_specs=_mixer_out_spec(w, nt),
        compiler_params=_mixer_params(),
        name="swa_mixer",
    )(proj, proj, proj, proj, proj, c, s_lo, s_hi, c, s_lo, s_hi, sinks.reshape(1, -1))


def _out_proj_kernel(x_ref, mod_ref, a_ref, b_ref, c_ref, d_ref, w_ref, o_ref):
    gw = GROUP_WIDTH
    acc = _dot(a_ref[...], w_ref[0:gw, :])
    acc += _dot(b_ref[...], w_ref[gw:2 * gw, :])
    acc += _dot(c_ref[...], w_ref[2 * gw:3 * gw, :])
    acc += _dot(d_ref[...], w_ref[3 * gw:4 * gw, :])
    gate = mod_ref[0][2:3]
    o_ref[...] = x_ref[...] + gate * acc


def _out_proj(x2, mod_l, mixed, w_out, seq, tm=512):
    m_rows, d = x2.shape
    per_b = seq // tm
    grp = pl.BlockSpec((tm, GROUP_WIDTH), lambda i: (i, 0))
    return pl.pallas_call(
        _out_proj_kernel,
        out_shape=jax.ShapeDtypeStruct((m_rows, d), F32),
        grid=(m_rows // tm,),
        in_specs=[pl.BlockSpec((tm, d), lambda i: (i, 0)),
                  pl.BlockSpec((1, 6, d), lambda i: (i // per_b, 0, 0)),
                  grp, grp, grp, grp,
                  pl.BlockSpec((d, d), lambda i: (0, 0))],
        out_specs=pl.BlockSpec((tm, d), lambda i: (i, 0)),
        compiler_params=pltpu.CompilerParams(
            dimension_semantics=("arbitrary",), vmem_limit_bytes=VMEM_LIMIT),
        name="out_proj",
    )(x2, mod_l, *mixed, w_out)


def _ffn_kernel(x_ref, mod_ref, nw_ref, wg_ref, wu_ref, wd_ref, o_ref, h_ref, acc_ref):
    f = pl.program_id(1)

    @pl.when(f == 0)
    def _():
        m = mod_ref[0]
        h = _modulated_norm(x_ref[...], nw_ref[...], m[3:4], m[4:5])
        h_ref[...] = h.astype(BF16)
        acc_ref[...] = jnp.zeros_like(acc_ref)

    h = h_ref[...]
    g = _dot(h, wg_ref[...])
    u = _dot(h, wu_ref[...])
    act = (_silu(g) * u).astype(BF16)
    acc_ref[...] += _dot(act, wd_ref[...])

    @pl.when(f == pl.num_programs(1) - 1)
    def _():
        o_ref[...] = x_ref[...] + mod_ref[0][5:6] * acc_ref[...]


def _ffn(x2, mod_l, nw, w_in, w_down, seq, tm=512, tf=512):
    m_rows, d = x2.shape
    nf = D_FF // tf
    per_b = seq // tm
    return pl.pallas_call(
        _ffn_kernel,
        out_shape=jax.ShapeDtypeStruct((m_rows, d), F32),
        grid=(m_rows // tm, nf),
        in_specs=[pl.BlockSpec((tm, d), lambda i, f: (i, 0)),
                  pl.BlockSpec((1, 6, d), lambda i, f: (i // per_b, 0, 0)),
                  pl.BlockSpec((1, d), lambda i, f: (0, 0)),
                  pl.BlockSpec((d, tf), lambda i, f: (0, f)),
                  pl.BlockSpec((d, tf), lambda i, f: (0, nf + f)),
                  pl.BlockSpec((tf, d), lambda i, f: (f, 0))],
        out_specs=pl.BlockSpec((tm, d), lambda i, f: (i, 0)),
        scratch_shapes=[pltpu.VMEM((tm, d), BF16), pltpu.VMEM((tm, d), F32)],
        compiler_params=pltpu.CompilerParams(
            dimension_semantics=("arbitrary", "arbitrary"), vmem_limit_bytes=VMEM_LIMIT),
        name="ffn",
    )(x2, mod_l, nw.reshape(1, d), w_in, w_in, w_down)


def _final_norm_kernel(x_ref, w_ref, o_ref):
    o_ref[...] = _rms(x_ref[...]) * w_ref[...]


def _final_norm(x2, w, tm=512):
    m_rows, d = x2.shape
    return pl.pallas_call(
        _final_norm_kernel,
        out_shape=jax.ShapeDtypeStruct((m_rows, d), F32),
        grid=(m_rows // tm,),
        in_specs=[pl.BlockSpec((tm, d), lambda i: (i, 0)), pl.BlockSpec((1, d), lambda i: (0, 0))],
        out_specs=pl.BlockSpec((tm, d), lambda i: (i, 0)),
        compiler_params=pltpu.CompilerParams(
            dimension_semantics=("arbitrary",), vmem_limit_bytes=VMEM_LIMIT),
        name="final_norm",
    )(x2, w.reshape(1, d))


def _permute_w_in(w):
    parts = [w[:, _SRC[name]:_SRC[name] + width] for name, width in _DST_ORDER]
    pad = PROJ_WIDTH - sum(width for _, width in _DST_ORDER)
    parts.append(jnp.zeros((w.shape[0], pad), w.dtype))
    return jnp.concatenate(parts, axis=1).astype(BF16)


def _ret_tables(positions):
    inv = 1.0 / jnp.power(RET_ROT_BASE, jnp.linspace(0.0, 1.0, RET_DK // 2, dtype=F32))
    ang = positions.astype(F32)[:, :, None] * inv[None, None, :]
    cos, sin = jnp.cos(ang), jnp.sin(ang)
    return jnp.concatenate([cos, cos], axis=-1), jnp.concatenate([-sin, sin], axis=-1)


def _swa_tables(positions):
    half = ROPE_DIM // 2
    inv = 1.0 / jnp.power(ROPE_THETA, jnp.arange(half, dtype=F32) / half)
    ang = positions.astype(F32)[:, :, None] * inv[None, None, :]
    cos, sin = jnp.cos(ang), jnp.sin(ang)
    rest = SWA_HD - 2 * half
    shape = ang.shape[:-1]
    c = jnp.concatenate([cos, cos, jnp.ones(shape + (rest,), F32)], axis=-1)
    s_lo = jnp.concatenate([-sin, jnp.zeros(shape + (SWA_HD - half,), F32)], axis=-1)
    s_hi = jnp.concatenate([jnp.zeros(shape + (half,), F32), sin, jnp.zeros(shape + (rest,), F32)], axis=-1)
    tile = lambda a: jnp.concatenate([a] * (LANES // SWA_HD), axis=-1)
    return tile(c), tile(s_lo), tile(s_hi)


def kernel(x, c, positions, w_ada, b_ada, norm1_w, w_in, gla_gate_w2, gla_gate_b2, gla_norm_w, swa_sinks, hgrn_lb, hgrn_norm_w, w_out, norm2_w, w_ffn_in, w_ffn_down, final_norm_w):
    batch, seq, d = x.shape
    depth = w_ada.shape[0]
    mod = _ada_modulation(c, w_ada, b_ada).reshape(depth, batch, 6, d)
    ret_cos, ret_sin = _ret_tables(positions)
    swa_tabs = _swa_tables(positions)
    x2 = x.reshape(batch * seq, d)
    for l in range(depth):
        w2_pad = jnp.zeros((LANES, GLA_HEADS * GLA_DK), F32).at[:GLA_GATE_RANK].set(gla_gate_w2[l])
        proj = _in_proj(x2, mod[l], norm1_w[l], _permute_w_in(w_in[l]), seq)
        mixed = (
            _gla(proj, w2_pad, gla_gate_b2[l], gla_norm_w[l], batch, seq),
            _ret(proj, ret_cos, ret_sin, batch, seq),
            _swa(proj, swa_tabs, swa_sinks[l], batch, seq),
            _hgrn(proj, hgrn_lb, hgrn_norm_w[l], l, batch, seq),
        )
        x2 = _out_proj(x2, mod[l], mixed, w_out[l].astype(BF16), seq)
        x2 = _ffn(x2, mod[l], norm2_w[l], w_ffn_in[l].astype(BF16), w_ffn_down[l].astype(BF16), seq)
    return _final_norm(x2, final_norm_w).reshape(batch, seq, d)
```

```python
import functools
import math

import jax
import jax.numpy as jnp
from jax import lax
from jax.experimental import pallas as pl
from jax.experimental.pallas import tpu as pltpu

F32 = jnp.float32
BF16 = jnp.bfloat16

D_MODEL = 2048
DEPTH = 2
EPS = 1e-6
GROUP_WIDTH = D_MODEL // 4
CHUNK = 64
SUB = 16
N_SUB = CHUNK // SUB
GLA_HEADS = 4
GLA_DV = GROUP_WIDTH // GLA_HEADS
GLA_DK = GLA_DV // 2
GLA_GATE_RANK = 16
GLA_GATE_NORMALIZER = 16.0
RET_HEADS = 4
RET_DK = GROUP_WIDTH // RET_HEADS
RET_ROT_BASE = 10000.0
SWA_Q_HEADS = 8
SWA_KV_HEADS = 2
SWA_HD = GROUP_WIDTH // SWA_Q_HEADS
WINDOW = 128
ROPE_THETA = 500000.0
ROPE_DIM = SWA_HD // 4
HG_HEADS = 4
HG_DK = GROUP_WIDTH // HG_HEADS
D_FF = ((8 * D_MODEL + 3 * 256 - 1) // (3 * 256)) * 256

LANES = 128
VMEM_LIMIT = 48 * 1024 * 1024

_SRC = dict(a_q=0, a_k=256, a_v=512, a_g=1024, a_r=1536,
            b_q=1552, b_k=2064, b_v=2576, b_g=3088,
            c_q=3600, c_k=4112, c_v=4240,
            d_q=4368, d_f=4880, d_i=5392, d_g=5904)
_DST_ORDER = (("a_q", 256), ("a_k", 256), ("a_v", 512), ("a_g", 512), ("c_q", 512),
              ("b_q", 512), ("b_k", 512), ("b_v", 512), ("b_g", 512),
              ("d_q", 512), ("d_f", 512), ("d_i", 512), ("d_g", 512),
              ("c_k", 128), ("c_v", 128), ("a_r", GLA_GATE_RANK))
PROJ_WIDTH = 6528
PROJ_TN = PROJ_WIDTH // 3


def _dst_offsets():
    offs, o = {}, 0
    for name, w in _DST_ORDER:
        offs[name] = o
        o += w
    return offs


_DST = _dst_offsets()


def _dot(a, b):
    return jnp.dot(a, b, preferred_element_type=F32)


def _dot_nt(a, b):
    return lax.dot_general(a, b, (((1,), (1,)), ((), ())), preferred_element_type=F32)


def _dot_tn(a, b):
    return lax.dot_general(a, b, (((0,), (0,)), ((), ())), preferred_element_type=F32)


def _split_bf16(x):
    hi = x.astype(BF16)
    lo = (x - hi.astype(F32)).astype(BF16)
    return hi, lo


def _dot3(a, b):
    a_hi, a_lo = _split_bf16(a)
    b_hi, b_lo = _split_bf16(b)
    return _dot(a_hi, b_hi) + (_dot(a_hi, b_lo) + _dot(a_lo, b_hi))


def _sigmoid(x):
    return 1.0 / (1.0 + jnp.exp(-x))


def _silu(x):
    return (0.5 * x) * (1.0 + jnp.tanh(0.5 * x))


def _log_sigmoid(x):
    return jnp.minimum(x, 0.0) - jnp.log1p(jnp.exp(-jnp.abs(x)))


def _rms(x):
    return x * lax.rsqrt(jnp.mean(x * x, axis=-1, keepdims=True) + EPS)


def _ada_kernel(c_ref, w_ref, b_ref, o_ref):
    cond = _silu(c_ref[...])
    o_ref[0] = _dot3(cond, w_ref[0]) + b_ref[0]


def _ada_modulation(c, w_ada, b_ada):
    depth, d, n = w_ada.shape
    rows = 8
    c_pad = jnp.zeros((rows, d), F32).at[:c.shape[0]].set(c)
    tn = 1024
    out = pl.pallas_call(
        _ada_kernel,
        out_shape=jax.ShapeDtypeStruct((depth, rows, n), F32),
        grid=(depth, n // tn),
        in_specs=[pl.BlockSpec((rows, d), lambda l, j: (0, 0)),
                  pl.BlockSpec((1, d, tn), lambda l, j: (l, 0, j)),
                  pl.BlockSpec((1, 1, tn), lambda l, j: (l, 0, j))],
        out_specs=pl.BlockSpec((1, rows, tn), lambda l, j: (l, 0, j)),
        compiler_params=pltpu.CompilerParams(
            dimension_semantics=("arbitrary", "arbitrary"), vmem_limit_bytes=VMEM_LIMIT),
        name="ada_modulation",
    )(c_pad, w_ada, b_ada.reshape(depth, 1, n))
    return out[:, :c.shape[0]]


def _modulated_norm(x, nw, shift, scale):
    return (_rms(x) * nw) * (1.0 + scale) + shift


def _in_proj_kernel(x_ref, mod_ref, nw_ref, w_ref, o_ref, h_ref):
    @pl.when(pl.program_id(1) == 0)
    def _():
        m = mod_ref[0]
        h = _modulated_norm(x_ref[...], nw_ref[...], m[0:1], m[1:2])
        h_ref[...] = h.astype(BF16)

    o_ref[...] = _dot(h_ref[...], w_ref[...])


def _in_proj(x2, mod_l, nw, w_perm, seq, tm=512):
    m_rows, d = x2.shape
    n = w_perm.shape[1]
    per_b = seq // tm
    return pl.pallas_call(
        _in_proj_kernel,
        out_shape=jax.ShapeDtypeStruct((m_rows, n), F32),
        grid=(m_rows // tm, n // PROJ_TN),
        in_specs=[pl.BlockSpec((tm, d), lambda i, j: (i, 0)),
                  pl.BlockSpec((1, 6, d), lambda i, j: (i // per_b, 0, 0)),
                  pl.BlockSpec((1, d), lambda i, j: (0, 0)),
                  pl.BlockSpec((d, PROJ_TN), lambda i, j: (0, j))],
        out_specs=pl.BlockSpec((tm, PROJ_TN), lambda i, j: (i, j)),
        scratch_shapes=[pltpu.VMEM((tm, d), BF16)],
        compiler_params=pltpu.CompilerParams(
            dimension_semantics=("arbitrary", "arbitrary"), vmem_limit_bytes=VMEM_LIMIT),
        name="in_proj",
    )(x2, mod_l, nw.reshape(1, d), w_perm)


TILE = 2 * CHUNK
CHUNK_SHIFT = CHUNK.bit_length() - 1
SUB_SHIFT = SUB.bit_length() - 1
SUBLANES = 8
SLAB = 64
DIAG_RUN = 4
assert DIAG_RUN == 4


def _ones_where(mask):
    return jnp.where(mask, 1.0, 0.0).astype(BF16)


def _gate_constants():
    i = lax.broadcasted_iota(jnp.int32, (TILE, TILE), 0)
    m = lax.broadcasted_iota(jnp.int32, (TILE, TILE), 1)
    same_chunk = (i >> CHUNK_SHIFT) == (m >> CHUNK_SHIFT)
    sub_end = i | (SUB - 1)
    chunk_start = i & ~(CHUNK - 1)
    after = same_chunk & (m > i)
    mats = [same_chunk & (m <= i),
            after & (m <= sub_end),
            after]
    for j in range(N_SUB - 1):
        mats.append(same_chunk & (m > chunk_start + (SUB * j + SUB - 1)) & (m <= i))
    cum = jnp.concatenate([_ones_where(x) for x in mats], axis=0)
    place = _ones_where(m == ((TILE - (i >> 3)) & (TILE - 1)))
    diag_ok = ((i >> SUB_SHIFT) == (m >> SUB_SHIFT)) & (m <= i)
    off_ok = same_chunk & ((m >> SUB_SHIFT) < (i >> SUB_SHIFT))
    sub_of_row = (lax.broadcasted_iota(jnp.int32, (TILE, 1), 0) >> SUB_SHIFT) & (N_SUB - 1)
    chunk_tril = same_chunk & (m <= i)
    return cum, place, diag_ok, off_ok, sub_of_row, chunk_tril


def _transpose_tiles(x):
    return jnp.concatenate([x[:, t:t + LANES].T for t in range(0, x.shape[1], LANES)], axis=0)


def _diag_partial_sums(q_t, k_t, g_t):
    dk = q_t.shape[0]
    slab = min(dk, SLAB)
    accs = [None] * SUB
    for s0 in range(0, dk, slab):
        qs = q_t[s0:s0 + slab]
        ks = k_t[s0:s0 + slab]
        g1 = g_t[s0:s0 + slab]
        g2 = g1 * pltpu.roll(g1, 1, axis=1)
        g4 = g2 * pltpu.roll(g2, 2, axis=1)
        g4_back = [g4] + [pltpu.roll(g4, DIAG_RUN * n, axis=1) for n in range(1, SUB // DIAG_RUN - 1)]
        for d0 in range(0, SUB, DIAG_RUN):
            w = ks
            if d0 > 0:
                w = pltpu.roll(ks, d0, axis=1)
                for n in range(d0 // DIAG_RUN):
                    w = w * g4_back[n]
            for d in range(d0, d0 + DIAG_RUN):
                if d > d0:
                    w = pltpu.roll(w, 1, axis=1) * g1
                p = jnp.sum((qs * w).reshape(slab // SUBLANES, SUBLANES, TILE), axis=0)
                accs[d] = p if accs[d] is None else accs[d] + p
    return jnp.concatenate(accs, axis=0)


def _split3_bf16(x):
    hi = x.astype(BF16)
    rest = x - hi.astype(F32)
    mid = rest.astype(BF16)
    lo = (rest - mid.astype(F32)).astype(BF16)
    return hi, mid, lo


def _cum_dot(mat, pieces):
    hi, mid, lo = pieces
    return _dot(mat, hi) + (_dot(mat, mid) + _dot(mat, lo))


def _head_lanes(h, dk):
    tile = (h * dk) // LANES
    if dk >= LANES:
        return tile, None
    lane = lax.broadcasted_iota(jnp.int32, (1, LANES), 1)
    start = (h * dk) % LANES
    return tile, (lane >= start) & (lane < start + dk)


def _gated_tile_robust(q, k, lg, lg_pieces, v, st_ref, consts, n_heads, dk, dv):
    cum, place, diag_ok, off_ok, sub_of_row, _ = consts
    sums = _cum_dot(cum, lg_pieces)
    cb = sums[0:TILE]
    q_state = (q * jnp.exp(cb)).astype(BF16)
    k_end = k * jnp.exp(sums[TILE:2 * TILE])
    k_last = (k * jnp.exp(sums[2 * TILE:3 * TILE])).astype(BF16)
    q_off = [(q * jnp.exp(sums[(3 + j) * TILE:(4 + j) * TILE])).astype(BF16) for j in range(N_SUB - 1)]
    k_off = [jnp.where(sub_of_row == j, k_end, 0.0).astype(BF16) for j in range(N_SUB - 1)]
    q_t = _transpose_tiles(q)
    k_t = _transpose_tiles(k)
    g_t = _transpose_tiles(jnp.exp(lg))
    outs = []
    for h in range(n_heads):
        ks = slice(h * dk, (h + 1) * dk)
        tile = (h * dk) // LANES
        ls = slice((h * dk) % LANES, (h * dk) % LANES + dk)
        vb = v[:, h * dv:(h + 1) * dv].astype(BF16)
        a_off = _dot_nt(q_off[0][:, ks], k_off[0][:, ks])
        for j in range(1, N_SUB - 1):
            a_off += _dot_nt(q_off[j][:, ks], k_off[j][:, ks])
        part = _diag_partial_sums(q_t[ks], k_t[ks], g_t[ks])
        p_hi, p_lo = _split_bf16(part)
        skew = _dot_tn(p_hi, place) + _dot_tn(p_lo, place)
        a_diag = pltpu.roll(skew, 0, axis=1, stride=1, stride_axis=0)
        a = jnp.where(diag_ok, a_diag, jnp.where(off_ok, a_off, 0.0)).astype(BF16)
        o = _dot(a, vb)
        st = st_ref[tile][:, ls]
        inter = []
        for c in range(TILE // CHUNK):
            rs = slice(c * CHUNK, (c + 1) * CHUNK)
            inter.append(_dot_nt(q_state[rs, ks], st.astype(BF16)))
            last = cb[c * CHUNK + CHUNK - 1:(c + 1) * CHUNK, ks]
            st = st * jnp.exp(last) + _dot_tn(vb[rs], k_last[rs, ks])
        st_ref[tile, :, ls] = st
        outs.append(o + jnp.concatenate(inter, axis=0))
    return outs


def _gated_tile_bounded(q, k, cb, v, st_ref, consts, n_heads, dk, dv):
    chunk_tril = consts[-1]
    row = lax.broadcasted_iota(jnp.int32, (TILE, 1), 0)
    ends = [cb[c * CHUNK + CHUNK - 1:(c + 1) * CHUNK, :] for c in range(TILE // CHUNK)]
    end_of_row = ends[-1]
    for c in range(TILE // CHUNK - 2, -1, -1):
        end_of_row = jnp.where(row < (c + 1) * CHUNK, ends[c], end_of_row)
    q_c = q * jnp.exp(cb)
    k_c = (k * jnp.exp(-cb)).astype(BF16)
    k_last = k * jnp.exp(end_of_row - cb)
    outs = [None] * n_heads
    heads_per_tile = max(LANES // dk, 1)
    for tile in range(n_heads * dk // LANES):
        ls = slice(tile * LANES, (tile + 1) * LANES)
        heads = range(tile * heads_per_tile, (tile + 1) * heads_per_tile)
        qh, kl, vb, intra = {}, {}, {}, {}
        for h in heads:
            _, mask = _head_lanes(h, dk)
            qt, kt = q_c[:, ls], k_last[:, ls]
            if mask is not None:
                qt, kt = jnp.where(mask, qt, 0.0), jnp.where(mask, kt, 0.0)
            qh[h], kl[h] = qt.astype(BF16), kt.astype(BF16)
            vb[h] = v[:, h * dv:(h + 1) * dv].astype(BF16)
            a = jnp.where(chunk_tril, _dot_nt(qh[h], k_c[:, ls]), 0.0).astype(BF16)
            intra[h] = _dot(a, vb[h])
        st = st_ref[tile]
        inter = {h: [] for h in heads}
        for c in range(TILE // CHUNK):
            rs = slice(c * CHUNK, (c + 1) * CHUNK)
            stb = st.astype(BF16)
            upd = None
            for h in heads:
                inter[h].append(_dot_nt(qh[h][rs], stb))
                u = _dot_tn(vb[h][rs], kl[h][rs])
                upd = u if upd is None else upd + u
            st = st * jnp.exp(ends[c][:, ls]) + upd
        st_ref[tile] = st
        for h in heads:
            outs[h] = intra[h] + jnp.concatenate(inter[h], axis=0)
    return outs


SAFE_DECAY = 64.0


def _gated_tile(q, k, lg, v, st_ref, consts, n_heads, dk, dv, finish):
    pieces = _split3_bf16(lg)
    cb = _cum_dot(consts[0][0:TILE], pieces)
    bounded = jnp.min(cb) >= -SAFE_DECAY

    @pl.when(bounded)
    def _():
        finish(_gated_tile_bounded(q, k, cb, v, st_ref, consts, n_heads, dk, dv))

    @pl.when(jnp.logical_not(bounded))
    def _():
        finish(_gated_tile_robust(q, k, lg, pieces, v, st_ref, consts, n_heads, dk, dv))


def _gla_kernel(q_ref, k_ref, v_ref, g_ref, r_ref, w2_ref, b2_ref, nw_ref, o_ref, st_ref, *, n_tiles):
    @pl.when(pl.program_id(1) == 0)
    def _():
        st_ref[...] = jnp.zeros_like(st_ref)

    consts = _gate_constants()
    w2 = w2_ref[...]
    b2 = b2_ref[...]
    nw = nw_ref[...]

    def body(c, carry):
        rows = pl.ds(pl.multiple_of(c * TILE, TILE), TILE)
        pre = _dot3(r_ref[rows, :], w2) + b2
        lg = _log_sigmoid(pre) * (1.0 / GLA_GATE_NORMALIZER)
        q = q_ref[rows, :] * (GLA_DK ** -0.5)

        def finish(outs):
            g = g_ref[rows, :]
            for h, o in enumerate(outs):
                vs = slice(h * GLA_DV, (h + 1) * GLA_DV)
                o_ref[rows, vs] = ((_rms(o) * nw) * _silu(g[:, vs])).astype(BF16)

        _gated_tile(q, k_ref[rows, :], lg, v_ref[rows, :], st_ref, consts,
                    GLA_HEADS, GLA_DK, GLA_DV, finish)
        return carry

    lax.fori_loop(0, n_tiles, body, 0)


def _hgrn_kernel(q_ref, f_ref, i_ref, g_ref, lb_ref, nw_ref, o_ref, st_ref, *, n_tiles, layer):
    @pl.when(pl.program_id(1) == 0)
    def _():
        st_ref[...] = jnp.zeros_like(st_ref)

    consts = _gate_constants()
    lbv = lb_ref[...]
    e = jnp.exp(lbv - jnp.max(lbv, axis=0, keepdims=True))
    soft = e / jnp.sum(e, axis=0, keepdims=True)
    cum = soft[0:1]
    for i in range(1, layer + 1):
        cum = cum + soft[i:i + 1]
    lb = cum - soft[0:1]
    nw = nw_ref[...]

    def body(c, carry):
        rows = pl.ds(pl.multiple_of(c * TILE, TILE), TILE)
        f = lb + (1.0 - lb) * _sigmoid(f_ref[rows, :])
        q = _silu(q_ref[rows, :]) * (HG_DK ** -0.5)

        def finish(outs):
            g = g_ref[rows, :]
            for h, o in enumerate(outs):
                hs = slice(h * HG_DK, (h + 1) * HG_DK)
                o_ref[rows, hs] = ((_rms(o) * nw) * _silu(g[:, hs])).astype(BF16)

        _gated_tile(q, 1.0 - f, jnp.log(f), i_ref[rows, :], st_ref, consts,
                    HG_HEADS, HG_DK, HG_DK, finish)
        return carry

    lax.fori_loop(0, n_tiles, body, 0)


def _col_spec(t, width, name, nt):
    blk = _DST[name] // width
    return pl.BlockSpec((t, width), lambda b, i: (b * nt + i, blk))


def _mixer_out_spec(t, nt):
    return pl.BlockSpec((t, GROUP_WIDTH), lambda b, i: (b * nt + i, 0))


def _mixer_params():
    return pltpu.CompilerParams(dimension_semantics=("arbitrary", "arbitrary"),
                                vmem_limit_bytes=VMEM_LIMIT)


def _gla(proj, w2_pad, b2, nw, batch, seq, t=512):
    nt = seq // t
    full = lambda shape: pl.BlockSpec(shape, lambda b, i: (0, 0))
    return pl.pallas_call(
        functools.partial(_gla_kernel, n_tiles=t // TILE),
        out_shape=jax.ShapeDtypeStruct((batch * seq, GROUP_WIDTH), BF16),
        grid=(batch, nt),
        in_specs=[_col_spec(t, 256, "a_q", nt), _col_spec(t, 256, "a_k", nt),
                  _col_spec(t, 512, "a_v", nt), _col_spec(t, 512, "a_g", nt),
                  _col_spec(t, LANES, "a_r", nt),
                  full((LANES, GLA_HEADS * GLA_DK)), full((1, GLA_HEADS * GLA_DK)), full((1, GLA_DV))],
        out_specs=_mixer_out_spec(t, nt),
        scratch_shapes=[pltpu.VMEM((GLA_HEADS * GLA_DK // LANES, GLA_DV, LANES), F32)],
        compiler_params=_mixer_params(),
        name="gla_mixer",
    )(proj, proj, proj, proj, proj, w2_pad, b2.reshape(1, -1), nw.reshape(1, -1))


def _hgrn(proj, hgrn_lb, nw, layer, batch, seq, t=512):
    nt = seq // t
    full = lambda shape: pl.BlockSpec(shape, lambda b, i: (0, 0))
    return pl.pallas_call(
        functools.partial(_hgrn_kernel, n_tiles=t // TILE, layer=layer),
        out_shape=jax.ShapeDtypeStruct((batch * seq, GROUP_WIDTH), BF16),
        grid=(batch, nt),
        in_specs=[_col_spec(t, 512, "d_q", nt), _col_spec(t, 512, "d_f", nt),
                  _col_spec(t, 512, "d_i", nt), _col_spec(t, 512, "d_g", nt),
                  full(hgrn_lb.shape), full((1, HG_DK))],
        out_specs=_mixer_out_spec(t, nt),
        scratch_shapes=[pltpu.VMEM((HG_HEADS * HG_DK // LANES, HG_DK, LANES), F32)],
        compiler_params=_mixer_params(),
        name="hgrn_mixer",
    )(proj, proj, proj, proj, hgrn_lb, nw.reshape(1, -1))


def _ret_kernel(q_ref, k_ref, v_ref, g_ref, cos_ref, sin_ref, o_ref, st_ref, *, t):
    @pl.when(pl.program_id(1) == 0)
    def _():
        st_ref[...] = jnp.zeros_like(st_ref)

    cos = cos_ref[0]
    sin = sin_ref[0]
    q = q_ref[...]
    k = k_ref[...]
    v = v_ref[...]
    g = g_ref[...]
    rowm = lax.broadcasted_iota(jnp.int32, (t, t), 0)
    colm = lax.broadcasted_iota(jnp.int32, (t, t), 1)
    rel = (rowm - colm).astype(F32)
    pos = lax.broadcasted_iota(jnp.int32, (t, RET_DK), 0).astype(F32)
    half = RET_DK // 2
    for h in range(RET_HEADS):
        lg = math.log1p(-(2.0 ** (-5.0 - h)))
        hs = slice(h * RET_DK, (h + 1) * RET_DK)
        qh = q[:, hs]
        kh = k[:, hs]
        qr = qh * cos + pltpu.roll(qh, half, axis=1) * sin
        kr = (kh * cos + pltpu.roll(kh, half, axis=1) * sin) * (RET_DK ** -0.5)
        vb = v[:, hs].astype(BF16)
        dmask = jnp.where(rel >= 0, jnp.exp(jnp.minimum(lg * rel, 0.0)), 0.0)
        scores = _dot_nt(qr.astype(BF16), kr.astype(BF16)) * dmask
        st = st_ref[h]
        xi = jnp.exp(lg * (pos + 1.0))
        o = _dot(scores.astype(BF16), vb) + _dot((qr * xi).astype(BF16), st.astype(BF16))
        zeta = jnp.exp(lg * (t - 1.0 - pos))
        st_ref[h] = st * math.exp(lg * t) + _dot_tn((kr * zeta).astype(BF16), vb)
        o_ref[:, hs] = (_rms(o) * _silu(g[:, hs])).astype(BF16)


def _ret(proj, cos, sin, batch, seq, t=256):
    nt = seq // t
    tab = pl.BlockSpec((1, t, RET_DK), lambda b, i: (b, i, 0))
    return pl.pallas_call(
        functools.partial(_ret_kernel, t=t),
        out_shape=jax.ShapeDtypeStruct((batch * seq, GROUP_WIDTH), BF16),
        grid=(batch, nt),
        in_specs=[_col_spec(t, 512, "b_q", nt), _col_spec(t, 512, "b_k", nt),
                  _col_spec(t, 512, "b_v", nt), _col_spec(t, 512, "b_g", nt), tab, tab],
        out_specs=_mixer_out_spec(t, nt),
        scratch_shapes=[pltpu.VMEM((RET_HEADS, RET_DK, RET_DK), F32)],
        compiler_params=_mixer_params(),
        name="ret_mixer",
    )(proj, proj, proj, proj, cos, sin)


def _rope_partial(x, c, s_lo, s_hi):
    n = x.shape[-1]
    half = ROPE_DIM // 2
    return x * c + pltpu.roll(x, n - half, axis=1) * s_lo + pltpu.roll(x, half, axis=1) * s_hi


def _swa_kernel(q_ref, kc_ref, vc_ref, c_ref, slo_ref, shi_ref, sink_ref, o_ref, kprev_ref, vprev_ref):
    w = WINDOW
    has_prev = pl.program_id(1) > 0

    @pl.when(pl.program_id(1) == 0)
    def _():
        kprev_ref[...] = jnp.zeros_like(kprev_ref)
        vprev_ref[...] = jnp.zeros_like(vprev_ref)

    c = c_ref[0]
    s_lo = slo_ref[0]
    s_hi = shi_ref[0]
    kc = _rope_partial(kc_ref[...], c, s_lo, s_hi)
    vc = vc_ref[...]
    lane = lax.broadcasted_iota(jnp.int32, (1, LANES), 1)
    lo_half = lane < SWA_HD
    group = SWA_Q_HEADS // SWA_KV_HEADS
    rowm = lax.broadcasted_iota(jnp.int32, (group * w, w), 0) & (w - 1)
    colm = lax.broadcasted_iota(jnp.int32, (group * w, w), 1)
    cur_ok = colm <= rowm
    prev_ok = (colm > rowm) & has_prev
    sinks = sink_ref[...]

    def both_halves(a, kv):
        swapped = pltpu.roll(a, SWA_HD, axis=1)
        return jnp.where(lo_half, a, swapped) if kv == 0 else jnp.where(lo_half, swapped, a)

    for kv in range(SWA_KV_HEADS):
        k2c = both_halves(kc, kv).astype(BF16)
        k2p = kprev_ref[kv]
        v2c = both_halves(vc, kv)
        v_half = [(jnp.where(lo_half, v2c, jnp.where(lane == SWA_HD, 1.0, 0.0)).astype(BF16), vprev_ref[kv, 0]),
                  (jnp.where(lo_half, jnp.where(lane == 0, 1.0, 0.0), v2c).astype(BF16), vprev_ref[kv, 1])]
        ones_lane = (SWA_HD, 0)
        kprev_ref[kv] = k2c
        vprev_ref[kv, 0] = v_half[0][0]
        vprev_ref[kv, 1] = v_half[1][0]
        q_rows, sink_rows = [], []
        for t in range(kv * group // 2, (kv + 1) * group // 2):
            ls = slice(t * LANES, (t + 1) * LANES)
            qt = _rope_partial(q_ref[:, ls], c, s_lo, s_hi) * (SWA_HD ** -0.5)
            q_rows += [jnp.where(lo_half, qt, 0.0), jnp.where(lo_half, 0.0, qt)]
            sink_rows += [jnp.broadcast_to(sinks[:, 2 * t:2 * t + 1], (w, 1)),
                          jnp.broadcast_to(sinks[:, 2 * t + 1:2 * t + 2], (w, 1))]
        q4 = jnp.concatenate(q_rows, axis=0).astype(BF16)
        sink = jnp.concatenate(sink_rows, axis=0)
        s_c = jnp.where(cur_ok, _dot_nt(q4, k2c), -jnp.inf)
        s_p = jnp.where(prev_ok, _dot_nt(q4, k2p), -jnp.inf)
        m = jnp.maximum(jnp.max(jnp.maximum(s_c, s_p), axis=-1, keepdims=True), sink)
        e_c = jnp.exp(s_c - m).astype(BF16)
        e_p = jnp.exp(s_p - m).astype(BF16)
        e_sink = jnp.exp(sink - m)
        for n, t in enumerate(range(kv * group // 2, (kv + 1) * group // 2)):
            halves = []
            for half in range(2):
                rs = slice((2 * n + half) * w, (2 * n + half + 1) * w)
                vh_c, vh_p = v_half[half]
                o = _dot(e_c[rs], vh_c) + _dot(e_p[rs], vh_p)
                denom = o[:, ones_lane[half]:ones_lane[half] + 1] + e_sink[rs]
                halves.append(o * (1.0 / denom))
            o_ref[:, t * LANES:(t + 1) * LANES] = jnp.where(lo_half, halves[0], halves[1]).astype(BF16)


def _swa(proj, tabs, sinks, batch, seq):
    w = WINDOW
    nt = seq // w
    kv_w = SWA_KV_HEADS * SWA_HD
    cur = lambda name: pl.BlockSpec((w, kv_w), lambda b, i: (b * nt + i, _DST[name] // kv_w))
    tab_c = pl.BlockSpec((1, w, LANES), lambda b, i: (b, i, 0))
    c, s_lo, s_hi = tabs
    return pl.pallas_call(
        _swa_kernel,
        out_shape=jax.ShapeDtypeStruct((batch * seq, GROUP_WIDTH), BF16),
        grid=(batch, nt),
        in_specs=[_col_spec(w, 512, "c_q", nt), cur("c_k"), cur("c_v"), tab_c, tab_c, tab_c,
                  pl.BlockSpec((1, SWA_Q_HEADS), lambda b, i: (0, 0))],
        out_specs=_mixer_out_spec(w, nt),
        scratch_shapes=[pltpu.VMEM((SWA_KV_HEADS, w, LANES), BF16),
                        pltpu.VMEM((SWA_KV_HEADS, 2, w, LANES), BF16)],
        compiler_params=_mixer_params(),
        name="swa_mixer",
    )(proj, proj, proj, c, s_lo, s_hi, sinks.reshape(1, -1))


def _out_proj_kernel(x_ref, mod_ref, a_ref, b_ref, c_ref, d_ref, w_ref, o_ref):
    gw = GROUP_WIDTH
    acc = _dot(a_ref[...], w_ref[0:gw, :])
    acc += _dot(b_ref[...], w_ref[gw:2 * gw, :])
    acc += _dot(c_ref[...], w_ref[2 * gw:3 * gw, :])
    acc += _dot(d_ref[...], w_ref[3 * gw:4 * gw, :])
    gate = mod_ref[0][2:3]
    o_ref[...] = x_ref[...] + gate * acc


def _out_proj(x2, mod_l, mixed, w_out, seq, tm=512):
    m_rows, d = x2.shape
    per_b = seq // tm
    grp = pl.BlockSpec((tm, GROUP_WIDTH), lambda i: (i, 0))
    return pl.pallas_call(
        _out_proj_kernel,
        out_shape=jax.ShapeDtypeStruct((m_rows, d), F32),
        grid=(m_rows // tm,),
        in_specs=[pl.BlockSpec((tm, d), lambda i: (i, 0)),
                  pl.BlockSpec((1, 6, d), lambda i: (i // per_b, 0, 0)),
                  grp, grp, grp, grp,
                  pl.BlockSpec((d, d), lambda i: (0, 0))],
        out_specs=pl.BlockSpec((tm, d), lambda i: (i, 0)),
        compiler_params=pltpu.CompilerParams(
            dimension_semantics=("arbitrary",), vmem_limit_bytes=VMEM_LIMIT),
        name="out_proj",
    )(x2, mod_l, *mixed, w_out)


def _ffn_kernel(x_ref, mod_ref, nw_ref, wg_ref, wu_ref, wd_ref, o_ref, h_ref, acc_ref):
    f = pl.program_id(1)

    @pl.when(f == 0)
    def _():
        m = mod_ref[0]
        h = _modulated_norm(x_ref[...], nw_ref[...], m[3:4], m[4:5])
        h_ref[...] = h.astype(BF16)
        acc_ref[...] = jnp.zeros_like(acc_ref)

    h = h_ref[...]
    g = _dot(h, wg_ref[...])
    u = _dot(h, wu_ref[...])
    act = (_silu(g) * u).astype(BF16)
    acc_ref[...] += _dot(act, wd_ref[...])

    @pl.when(f == pl.num_programs(1) - 1)
    def _():
        o_ref[...] = x_ref[...] + mod_ref[0][5:6] * acc_ref[...]


def _ffn(x2, mod_l, nw, w_in, w_down, seq, tm=512, tf=512):
    m_rows, d = x2.shape
    nf = D_FF // tf
    per_b = seq // tm
    return pl.pallas_call(
        _ffn_kernel,
        out_shape=jax.ShapeDtypeStruct((m_rows, d), F32),
        grid=(m_rows // tm, nf),
        in_specs=[pl.BlockSpec((tm, d), lambda i, f: (i, 0)),
                  pl.BlockSpec((1, 6, d), lambda i, f: (i // per_b, 0, 0)),
                  pl.BlockSpec((1, d), lambda i, f: (0, 0)),
                  pl.BlockSpec((d, tf), lambda i, f: (0, f)),
                  pl.BlockSpec((d, tf), lambda i, f: (0, nf + f)),
                  pl.BlockSpec((tf, d), lambda i, f: (f, 0))],
        out_specs=pl.BlockSpec((tm, d), lambda i, f: (i, 0)),
        scratch_shapes=[pltpu.VMEM((tm, d), BF16), pltpu.VMEM((tm, d), F32)],
        compiler_params=pltpu.CompilerParams(
            dimension_semantics=("arbitrary", "arbitrary"), vmem_limit_bytes=VMEM_LIMIT),
        name="ffn",
    )(x2, mod_l, nw.reshape(1, d), w_in, w_in, w_down)


def _final_norm_kernel(x_ref, w_ref, o_ref):
    o_ref[...] = _rms(x_ref[...]) * w_ref[...]


def _final_norm(x2, w, tm=512):
    m_rows, d = x2.shape
    return pl.pallas_call(
        _final_norm_kernel,
        out_shape=jax.ShapeDtypeStruct((m_rows, d), F32),
        grid=(m_rows // tm,),
        in_specs=[pl.BlockSpec((tm, d), lambda i: (i, 0)), pl.BlockSpec((1, d), lambda i: (0, 0))],
        out_specs=pl.BlockSpec((tm, d), lambda i: (i, 0)),
        compiler_params=pltpu.CompilerParams(
            dimension_semantics=("arbitrary",), vmem_limit_bytes=VMEM_LIMIT),
        name="final_norm",
    )(x2, w.reshape(1, d))


def _permute_w_in(w):
    parts = [w[:, _SRC[name]:_SRC[name] + width] for name, width in _DST_ORDER]
    pad = PROJ_WIDTH - sum(width for _, width in _DST_ORDER)
    parts.append(jnp.zeros((w.shape[0], pad), w.dtype))
    return jnp.concatenate(parts, axis=1).astype(BF16)


def _ret_tables(positions):
    inv = 1.0 / jnp.power(RET_ROT_BASE, jnp.linspace(0.0, 1.0, RET_DK // 2, dtype=F32))
    ang = positions.astype(F32)[:, :, None] * inv[None, None, :]
    cos, sin = jnp.cos(ang), jnp.sin(ang)
    return jnp.concatenate([cos, cos], axis=-1), jnp.concatenate([-sin, sin], axis=-1)


def _swa_tables(positions):
    half = ROPE_DIM // 2
    inv = 1.0 / jnp.power(ROPE_THETA, jnp.arange(half, dtype=F32) / half)
    ang = positions.astype(F32)[:, :, None] * inv[None, None, :]
    cos, sin = jnp.cos(ang), jnp.sin(ang)
    rest = SWA_HD - 2 * half
    shape = ang.shape[:-1]
    c = jnp.concatenate([cos, cos, jnp.ones(shape + (rest,), F32)], axis=-1)
    s_lo = jnp.concatenate([-sin, jnp.zeros(shape + (SWA_HD - half,), F32)], axis=-1)
    s_hi = jnp.concatenate([jnp.zeros(shape + (half,), F32), sin, jnp.zeros(shape + (rest,), F32)], axis=-1)
    tile = lambda a: jnp.concatenate([a] * (LANES // SWA_HD), axis=-1)
    return tile(c), tile(s_lo), tile(s_hi)


def kernel(x, c, positions, w_ada, b_ada, norm1_w, w_in, gla_gate_w2, gla_gate_b2, gla_norm_w, swa_sinks, hgrn_lb, hgrn_norm_w, w_out, norm2_w, w_ffn_in, w_ffn_down, final_norm_w):
    batch, seq, d = x.shape
    depth = w_ada.shape[0]
    mod = _ada_modulation(c, w_ada, b_ada).reshape(depth, batch, 6, d)
    ret_cos, ret_sin = _ret_tables(positions)
    swa_tabs = _swa_tables(positions)
    x2 = x.reshape(batch * seq, d)
    for l in range(depth):
        w2_pad = jnp.zeros((LANES, GLA_HEADS * GLA_DK), F32).at[:GLA_GATE_RANK].set(gla_gate_w2[l])
        proj = _in_proj(x2, mod[l], norm1_w[l], _permute_w_in(w_in[l]), seq)
        mixed = (
            _gla(proj, w2_pad, gla_gate_b2[l], gla_norm_w[l], batch, seq),
            _ret(proj, ret_cos, ret_sin, batch, seq),
            _swa(proj, swa_tabs, swa_sinks[l], batch, seq),
            _hgrn(proj, hgrn_lb, hgrn_norm_w[l], l, batch, seq),
        )
        x2 = _out_proj(x2, mod[l], mixed, w_out[l].astype(BF16), seq)
        x2 = _ffn(x2, mod[l], norm2_w[l], w_ffn_in[l].astype(BF16), w_ffn_down[l].astype(BF16), seq)
    return _final_norm(x2, final_norm_w).reshape(batch, seq, d)
```

```python
import functools
import math

import jax
import jax.numpy as jnp
from jax import lax
from jax.experimental import pallas as pl
from jax.experimental.pallas import tpu as pltpu

F32 = jnp.float32
BF16 = jnp.bfloat16

D_MODEL = 2048
DEPTH = 2
EPS = 1e-6
GROUP_WIDTH = D_MODEL // 4
CHUNK = 64
SUB = 16
N_SUB = CHUNK // SUB
GLA_HEADS = 4
GLA_DV = GROUP_WIDTH // GLA_HEADS
GLA_DK = GLA_DV // 2
GLA_GATE_RANK = 16
GLA_GATE_NORMALIZER = 16.0
RET_HEADS = 4
RET_DK = GROUP_WIDTH // RET_HEADS
RET_ROT_BASE = 10000.0
SWA_Q_HEADS = 8
SWA_KV_HEADS = 2
SWA_HD = GROUP_WIDTH // SWA_Q_HEADS
WINDOW = 128
ROPE_THETA = 500000.0
ROPE_DIM = SWA_HD // 4
HG_HEADS = 4
HG_DK = GROUP_WIDTH // HG_HEADS
D_FF = ((8 * D_MODEL + 3 * 256 - 1) // (3 * 256)) * 256

LANES = 128
VMEM_LIMIT = 48 * 1024 * 1024

_SRC = dict(a_q=0, a_k=256, a_v=512, a_g=1024, a_r=1536,
            b_q=1552, b_k=2064, b_v=2576, b_g=3088,
            c_q=3600, c_k=4112, c_v=4240,
            d_q=4368, d_f=4880, d_i=5392, d_g=5904)
_DST_ORDER = (("a_q", 256), ("a_k", 256), ("a_v", 512), ("a_g", 512), ("c_q", 512),
              ("b_q", 512), ("b_k", 512), ("b_v", 512), ("b_g", 512),
              ("d_q", 512), ("d_f", 512), ("d_i", 512), ("d_g", 512),
              ("c_k", 128), ("c_v", 128), ("a_r", GLA_GATE_RANK))
PROJ_WIDTH = 6528
PROJ_TN = PROJ_WIDTH // 3


def _dst_offsets():
    offs, o = {}, 0
    for name, w in _DST_ORDER:
        offs[name] = o
        o += w
    return offs


_DST = _dst_offsets()


def _dot(a, b):
    return jnp.dot(a, b, preferred_element_type=F32)


def _dot_nt(a, b):
    return lax.dot_general(a, b, (((1,), (1,)), ((), ())), preferred_element_type=F32)


def _dot_tn(a, b):
    return lax.dot_general(a, b, (((0,), (0,)), ((), ())), preferred_element_type=F32)


def _split_bf16(x):
    hi = x.astype(BF16)
    lo = (x - hi.astype(F32)).astype(BF16)
    return hi, lo


def _dot3(a, b):
    a_hi, a_lo = _split_bf16(a)
    b_hi, b_lo = _split_bf16(b)
    return _dot(a_hi, b_hi) + (_dot(a_hi, b_lo) + _dot(a_lo, b_hi))


def _sigmoid(x):
    return 1.0 / (1.0 + jnp.exp(-x))


def _silu(x):
    return (0.5 * x) * (1.0 + jnp.tanh(0.5 * x))


def _log_sigmoid(x):
    return jnp.minimum(x, 0.0) - jnp.log1p(jnp.exp(-jnp.abs(x)))


def _rms(x):
    return x * lax.rsqrt(jnp.mean(x * x, axis=-1, keepdims=True) + EPS)


def _ada_kernel(c_ref, w_ref, b_ref, o_ref):
    cond = _silu(c_ref[...])
    o_ref[0] = _dot3(cond, w_ref[0]) + b_ref[0]


def _ada_modulation(c, w_ada, b_ada):
    depth, d, n = w_ada.shape
    rows = 8
    c_pad = jnp.zeros((rows, d), F32).at[:c.shape[0]].set(c)
    tn = 1024
    out = pl.pallas_call(
        _ada_kernel,
        out_shape=jax.ShapeDtypeStruct((depth, rows, n), F32),
        grid=(depth, n // tn),
        in_specs=[pl.BlockSpec((rows, d), lambda l, j: (0, 0)),
                  pl.BlockSpec((1, d, tn), lambda l, j: (l, 0, j)),
                  pl.BlockSpec((1, 1, tn), lambda l, j: (l, 0, j))],
        out_specs=pl.BlockSpec((1, rows, tn), lambda l, j: (l, 0, j)),
        compiler_params=pltpu.CompilerParams(
            dimension_semantics=("arbitrary", "arbitrary"), vmem_limit_bytes=VMEM_LIMIT),
        name="ada_modulation",
    )(c_pad, w_ada, b_ada.reshape(depth, 1, n))
    return out[:, :c.shape[0]]


def _modulated_norm(x, nw, shift, scale):
    return (_rms(x) * nw) * (1.0 + scale) + shift


def _in_proj_kernel(x_ref, mod_ref, nw_ref, w_ref, o_ref, h_ref):
    @pl.when(pl.program_id(1) == 0)
    def _():
        m = mod_ref[0]
        h = _modulated_norm(x_ref[...], nw_ref[...], m[0:1], m[1:2])
        h_ref[...] = h.astype(BF16)

    o_ref[...] = _dot(h_ref[...], w_ref[...])


def _in_proj(x2, mod_l, nw, w_perm, seq, tm=512):
    m_rows, d = x2.shape
    n = w_perm.shape[1]
    per_b = seq // tm
    return pl.pallas_call(
        _in_proj_kernel,
        out_shape=jax.ShapeDtypeStruct((m_rows, n), F32),
        grid=(m_rows // tm, n // PROJ_TN),
        in_specs=[pl.BlockSpec((tm, d), lambda i, j: (i, 0)),
                  pl.BlockSpec((1, 6, d), lambda i, j: (i // per_b, 0, 0)),
                  pl.BlockSpec((1, d), lambda i, j: (0, 0)),
                  pl.BlockSpec((d, PROJ_TN), lambda i, j: (0, j))],
        out_specs=pl.BlockSpec((tm, PROJ_TN), lambda i, j: (i, j)),
        scratch_shapes=[pltpu.VMEM((tm, d), BF16)],
        compiler_params=pltpu.CompilerParams(
            dimension_semantics=("arbitrary", "arbitrary"), vmem_limit_bytes=VMEM_LIMIT),
        name="in_proj",
    )(x2, mod_l, nw.reshape(1, d), w_perm)


TILE = 2 * CHUNK
GLA_BLOCK = CHUNK
HG_BLOCK = SUB
CHUNK_SHIFT = CHUNK.bit_length() - 1
SUB_SHIFT = SUB.bit_length() - 1
SUBLANES = 8
SLAB = 64
DIAG_RUN = 4
assert DIAG_RUN == 4


def _ones_where(mask):
    return jnp.where(mask, 1.0, 0.0).astype(BF16)


def _levels(block):
    return [hs for hs in (CHUNK // 2, CHUNK // 4) if hs >= block]


def _gate_constants(block):
    i = lax.broadcasted_iota(jnp.int32, (TILE, TILE), 0)
    m = lax.broadcasted_iota(jnp.int32, (TILE, TILE), 1)
    same_chunk = (i >> CHUNK_SHIFT) == (m >> CHUNK_SHIFT)
    sub_end = i | (SUB - 1)
    chunk_start = i & ~(CHUNK - 1)
    after = same_chunk & (m > i)
    mats = [same_chunk & (m <= i),
            after & (m <= sub_end),
            after]
    for j in range(N_SUB - 1):
        mats.append(same_chunk & (m > chunk_start + (SUB * j + SUB - 1)) & (m <= i))
    cum = jnp.concatenate([_ones_where(x) for x in mats], axis=0)
    place = _ones_where(m == ((TILE - (i >> 3)) & (TILE - 1)))
    diag_ok = ((i >> SUB_SHIFT) == (m >> SUB_SHIFT)) & (m <= i)
    off_ok = same_chunk & ((m >> SUB_SHIFT) < (i >> SUB_SHIFT))
    sub_of_row = (lax.broadcasted_iota(jnp.int32, (TILE, 1), 0) >> SUB_SHIFT) & (N_SUB - 1)
    score_masks = []
    for hs in _levels(block):
        same = (i >> (2 * hs).bit_length() - 1) == (m >> (2 * hs).bit_length() - 1)
        score_masks.append(same & ((i & (2 * hs - 1)) >= hs) & ((m & (2 * hs - 1)) < hs))
    score_masks.append(((i >> block.bit_length() - 1) == (m >> block.bit_length() - 1)) & (m <= i))
    return cum, place, diag_ok, off_ok, sub_of_row, score_masks


def _transpose_tiles(x):
    return jnp.concatenate([x[:, t:t + LANES].T for t in range(0, x.shape[1], LANES)], axis=0)


def _diag_partial_sums(q_t, k_t, g_t):
    dk = q_t.shape[0]
    slab = min(dk, SLAB)
    accs = [None] * SUB
    for s0 in range(0, dk, slab):
        qs = q_t[s0:s0 + slab]
        ks = k_t[s0:s0 + slab]
        g1 = g_t[s0:s0 + slab]
        g2 = g1 * pltpu.roll(g1, 1, axis=1)
        g4 = g2 * pltpu.roll(g2, 2, axis=1)
        g4_back = [g4] + [pltpu.roll(g4, DIAG_RUN * n, axis=1) for n in range(1, SUB // DIAG_RUN - 1)]
        for d0 in range(0, SUB, DIAG_RUN):
            w = ks
            if d0 > 0:
                w = pltpu.roll(ks, d0, axis=1)
                for n in range(d0 // DIAG_RUN):
                    w = w * g4_back[n]
            for d in range(d0, d0 + DIAG_RUN):
                if d > d0:
                    w = pltpu.roll(w, 1, axis=1) * g1
                p = jnp.sum((qs * w).reshape(slab // SUBLANES, SUBLANES, TILE), axis=0)
                accs[d] = p if accs[d] is None else accs[d] + p
    return jnp.concatenate(accs, axis=0)


def _split3_bf16(x):
    hi = x.astype(BF16)
    rest = x - hi.astype(F32)
    mid = rest.astype(BF16)
    lo = (rest - mid.astype(F32)).astype(BF16)
    return hi, mid, lo


def _cum_dot(mat, pieces):
    hi, mid, lo = pieces
    return _dot(mat, hi) + (_dot(mat, mid) + _dot(mat, lo))


def _head_lanes(h, dk):
    tile = (h * dk) // LANES
    if dk >= LANES:
        return tile, None
    lane = lax.broadcasted_iota(jnp.int32, (1, LANES), 1)
    start = (h * dk) % LANES
    return tile, (lane >= start) & (lane < start + dk)


def _gated_tile_robust(q, k, lg, lg_pieces, v, st_ref, consts, n_heads, dk, dv):
    cum, place, diag_ok, off_ok, sub_of_row, _ = consts
    sums = _cum_dot(cum, lg_pieces)
    cb = sums[0:TILE]
    q_state = (q * jnp.exp(cb)).astype(BF16)
    k_end = k * jnp.exp(sums[TILE:2 * TILE])
    k_last = (k * jnp.exp(sums[2 * TILE:3 * TILE])).astype(BF16)
    q_off = [(q * jnp.exp(sums[(3 + j) * TILE:(4 + j) * TILE])).astype(BF16) for j in range(N_SUB - 1)]
    k_off = [jnp.where(sub_of_row == j, k_end, 0.0).astype(BF16) for j in range(N_SUB - 1)]
    q_t = _transpose_tiles(q)
    k_t = _transpose_tiles(k)
    g_t = _transpose_tiles(jnp.exp(lg))
    outs = []
    for h in range(n_heads):
        ks = slice(h * dk, (h + 1) * dk)
        tile = (h * dk) // LANES
        ls = slice((h * dk) % LANES, (h * dk) % LANES + dk)
        vb = v[:, h * dv:(h + 1) * dv].astype(BF16)
        a_off = _dot_nt(q_off[0][:, ks], k_off[0][:, ks])
        for j in range(1, N_SUB - 1):
            a_off += _dot_nt(q_off[j][:, ks], k_off[j][:, ks])
        part = _diag_partial_sums(q_t[ks], k_t[ks], g_t[ks])
        p_hi, p_lo = _split_bf16(part)
        skew = _dot_tn(p_hi, place) + _dot_tn(p_lo, place)
        a_diag = pltpu.roll(skew, 0, axis=1, stride=1, stride_axis=0)
        a = jnp.where(diag_ok, a_diag, jnp.where(off_ok, a_off, 0.0)).astype(BF16)
        o = _dot(a, vb)
        st = st_ref[tile][:, ls]
        inter = []
        for c in range(TILE // CHUNK):
            rs = slice(c * CHUNK, (c + 1) * CHUNK)
            inter.append(_dot_nt(q_state[rs, ks], st.astype(BF16)))
            last = cb[c * CHUNK + CHUNK - 1:(c + 1) * CHUNK, ks]
            st = st * jnp.exp(last) + _dot_tn(vb[rs], k_last[rs, ks])
        st_ref[tile, :, ls] = st
        outs.append(o + jnp.concatenate(inter, axis=0))
    return outs


def _rows_from(cb, row, spans):
    out = None
    for lo, src in spans:
        val = jnp.zeros_like(cb[0:1]) if src is None else cb[src:src + 1]
        out = val if out is None else jnp.where(row >= lo, val, out)
    return out


def _block_prefix(cb, block):
    if block == CHUNK:
        return cb
    row = lax.broadcasted_iota(jnp.int32, (TILE, 1), 0)
    spans = [(b0, None if b0 % CHUNK == 0 else b0 - 1) for b0 in range(0, TILE, block)]
    return cb - _rows_from(cb, row, spans)


def _gated_tile_bounded(q, k, cb, pb, v, st_ref, consts, n_heads, dk, dv, block):
    score_masks = consts[-1]
    row = lax.broadcasted_iota(jnp.int32, (TILE, 1), 0)
    ends = [cb[c * CHUNK + CHUNK - 1:(c + 1) * CHUNK, :] for c in range(TILE // CHUNK)]
    end_of_row = _rows_from(cb, row, [(c * CHUNK, c * CHUNK + CHUNK - 1) for c in range(TILE // CHUNK)])
    q_state = q * jnp.exp(cb)
    k_last = k * jnp.exp(end_of_row - cb)
    q_parts, k_parts = [], []
    for hs in _levels(block):
        ref = _rows_from(cb, row, [(b0, b0 + hs - 1) for b0 in range(0, TILE, 2 * hs)])
        lower = (row & (2 * hs - 1)) < hs
        x = jnp.exp(jnp.where(lower, ref - cb, cb - ref))
        q_parts.append(jnp.where(lower, 0.0, q * x))
        k_parts.append(jnp.where(lower, k * x, 0.0).astype(BF16))
    q_parts.append(q_state if block == CHUNK else q * jnp.exp(pb))
    k_parts.append((k * jnp.exp(-pb)).astype(BF16))
    outs = [None] * n_heads
    heads_per_tile = max(LANES // dk, 1)
    for tile in range(n_heads * dk // LANES):
        ls = slice(tile * LANES, (tile + 1) * LANES)
        heads = range(tile * heads_per_tile, (tile + 1) * heads_per_tile)
        qh, kl, vb, intra = {}, {}, {}, {}
        for h in heads:
            _, mask = _head_lanes(h, dk)
            own = (lambda a: a) if mask is None else (lambda a: jnp.where(mask, a, 0.0))
            qh[h] = own(q_state[:, ls]).astype(BF16)
            kl[h] = own(k_last[:, ls]).astype(BF16)
            vb[h] = v[:, h * dv:(h + 1) * dv].astype(BF16)
            a = 0.0
            for qp, kp, ok in zip(q_parts, k_parts, score_masks):
                a = jnp.where(ok, _dot_nt(own(qp[:, ls]).astype(BF16), kp[:, ls]), a)
            intra[h] = _dot(a.astype(BF16), vb[h])
        st = st_ref[tile]
        inter = {h: [] for h in heads}
        for c in range(TILE // CHUNK):
            rs = slice(c * CHUNK, (c + 1) * CHUNK)
            stb = st.astype(BF16)
            upd = None
            for h in heads:
                inter[h].append(_dot_nt(qh[h][rs], stb))
                u = _dot_tn(vb[h][rs], kl[h][rs])
                upd = u if upd is None else upd + u
            st = st * jnp.exp(ends[c][:, ls]) + upd
        st_ref[tile] = st
        for h in heads:
            outs[h] = intra[h] + jnp.concatenate(inter[h], axis=0)
    return outs


SAFE_DECAY = 64.0


def _gated_tile(q, k, lg, v, st_ref, consts, n_heads, dk, dv, block, finish):
    pieces = _split3_bf16(lg)
    cb = _cum_dot(consts[0][0:TILE], pieces)
    pb = _block_prefix(cb, block)
    bounded = jnp.min(pb) >= -SAFE_DECAY

    @pl.when(bounded)
    def _():
        finish(_gated_tile_bounded(q, k, cb, pb, v, st_ref, consts, n_heads, dk, dv, block))

    @pl.when(jnp.logical_not(bounded))
    def _():
        finish(_gated_tile_robust(q, k, lg, pieces, v, st_ref, consts, n_heads, dk, dv))


def _gla_kernel(q_ref, k_ref, v_ref, g_ref, r_ref, w2_ref, b2_ref, nw_ref, o_ref, st_ref, *, n_tiles):
    @pl.when(pl.program_id(1) == 0)
    def _():
        st_ref[...] = jnp.zeros_like(st_ref)

    consts = _gate_constants(GLA_BLOCK)
    w2 = w2_ref[...]
    b2 = b2_ref[...]
    nw = nw_ref[...]

    def body(c, carry):
        rows = pl.ds(pl.multiple_of(c * TILE, TILE), TILE)
        pre = _dot3(r_ref[rows, :], w2) + b2
        lg = _log_sigmoid(pre) * (1.0 / GLA_GATE_NORMALIZER)
        q = q_ref[rows, :] * (GLA_DK ** -0.5)

        def finish(outs):
            g = g_ref[rows, :]
            for h, o in enumerate(outs):
                vs = slice(h * GLA_DV, (h + 1) * GLA_DV)
                o_ref[rows, vs] = ((_rms(o) * nw) * _silu(g[:, vs])).astype(BF16)

        _gated_tile(q, k_ref[rows, :], lg, v_ref[rows, :], st_ref, consts,
                    GLA_HEADS, GLA_DK, GLA_DV, GLA_BLOCK, finish)
        return carry

    lax.fori_loop(0, n_tiles, body, 0)


def _hgrn_kernel(q_ref, f_ref, i_ref, g_ref, lb_ref, nw_ref, o_ref, st_ref, *, n_tiles, layer):
    @pl.when(pl.program_id(1) == 0)
    def _():
        st_ref[...] = jnp.zeros_like(st_ref)

    consts = _gate_constants(HG_BLOCK)
    lbv = lb_ref[...]
    e = jnp.exp(lbv - jnp.max(lbv, axis=0, keepdims=True))
    soft = e / jnp.sum(e, axis=0, keepdims=True)
    cum = soft[0:1]
    for i in range(1, layer + 1):
        cum = cum + soft[i:i + 1]
    lb = cum - soft[0:1]
    nw = nw_ref[...]

    def body(c, carry):
        rows = pl.ds(pl.multiple_of(c * TILE, TILE), TILE)
        f = lb + (1.0 - lb) * _sigmoid(f_ref[rows, :])
        q = _silu(q_ref[rows, :]) * (HG_DK ** -0.5)

        def finish(outs):
            g = g_ref[rows, :]
            for h, o in enumerate(outs):
                hs = slice(h * HG_DK, (h + 1) * HG_DK)
                o_ref[rows, hs] = ((_rms(o) * nw) * _silu(g[:, hs])).astype(BF16)

        _gated_tile(q, 1.0 - f, jnp.log(f), i_ref[rows, :], st_ref, consts,
                    HG_HEADS, HG_DK, HG_DK, HG_BLOCK, finish)
        return carry

    lax.fori_loop(0, n_tiles, body, 0)


def _col_spec(t, width, name, nt):
    blk = _DST[name] // width
    return pl.BlockSpec((t, width), lambda b, i: (b * nt + i, blk))


def _mixer_out_spec(t, nt):
    return pl.BlockSpec((t, GROUP_WIDTH), lambda b, i: (b * nt + i, 0))


def _mixer_params():
    return pltpu.CompilerParams(dimension_semantics=("arbitrary", "arbitrary"),
                                vmem_limit_bytes=VMEM_LIMIT)


def _gla(proj, w2_pad, b2, nw, batch, seq, t=512):
    nt = seq // t
    full = lambda shape: pl.BlockSpec(shape, lambda b, i: (0, 0))
    return pl.pallas_call(
        functools.partial(_gla_kernel, n_tiles=t // TILE),
        out_shape=jax.ShapeDtypeStruct((batch * seq, GROUP_WIDTH), BF16),
        grid=(batch, nt),
        in_specs=[_col_spec(t, 256, "a_q", nt), _col_spec(t, 256, "a_k", nt),
                  _col_spec(t, 512, "a_v", nt), _col_spec(t, 512, "a_g", nt),
                  _col_spec(t, LANES, "a_r", nt),
                  full((LANES, GLA_HEADS * GLA_DK)), full((1, GLA_HEADS * GLA_DK)), full((1, GLA_DV))],
        out_specs=_mixer_out_spec(t, nt),
        scratch_shapes=[pltpu.VMEM((GLA_HEADS * GLA_DK // LANES, GLA_DV, LANES), F32)],
        compiler_params=_mixer_params(),
        name="gla_mixer",
    )(proj, proj, proj, proj, proj, w2_pad, b2.reshape(1, -1), nw.reshape(1, -1))


def _hgrn(proj, hgrn_lb, nw, layer, batch, seq, t=512):
    nt = seq // t
    full = lambda shape: pl.BlockSpec(shape, lambda b, i: (0, 0))
    return pl.pallas_call(
        functools.partial(_hgrn_kernel, n_tiles=t // TILE, layer=layer),
        out_shape=jax.ShapeDtypeStruct((batch * seq, GROUP_WIDTH), BF16),
        grid=(batch, nt),
        in_specs=[_col_spec(t, 512, "d_q", nt), _col_spec(t, 512, "d_f", nt),
                  _col_spec(t, 512, "d_i", nt), _col_spec(t, 512, "d_g", nt),
                  full(hgrn_lb.shape), full((1, HG_DK))],
        out_specs=_mixer_out_spec(t, nt),
        scratch_shapes=[pltpu.VMEM((HG_HEADS * HG_DK // LANES, HG_DK, LANES), F32)],
        compiler_params=_mixer_params(),
        name="hgrn_mixer",
    )(proj, proj, proj, proj, hgrn_lb, nw.reshape(1, -1))


def _ret_kernel(q_ref, k_ref, v_ref, g_ref, cos_ref, sin_ref, o_ref, st_ref, *, t):
    @pl.when(pl.program_id(1) == 0)
    def _():
        st_ref[...] = jnp.zeros_like(st_ref)

    cos = cos_ref[0]
    sin = sin_ref[0]
    q = q_ref[...]
    k = k_ref[...]
    v = v_ref[...]
    g = g_ref[...]
    rowm = lax.broadcasted_iota(jnp.int32, (t, t), 0)
    colm = lax.broadcasted_iota(jnp.int32, (t, t), 1)
    rel = (rowm - colm).astype(F32)
    pos = lax.broadcasted_iota(jnp.int32, (t, RET_DK), 0).astype(F32)
    half = RET_DK // 2
    for h in range(RET_HEADS):
        lg = math.log1p(-(2.0 ** (-5.0 - h)))
        hs = slice(h * RET_DK, (h + 1) * RET_DK)
        qh = q[:, hs]
        kh = k[:, hs]
        qr = qh * cos + pltpu.roll(qh, half, axis=1) * sin
        kr = (kh * cos + pltpu.roll(kh, half, axis=1) * sin) * (RET_DK ** -0.5)
        vb = v[:, hs].astype(BF16)
        dmask = jnp.where(rel >= 0, jnp.exp(jnp.minimum(lg * rel, 0.0)), 0.0)
        scores = _dot_nt(qr.astype(BF16), kr.astype(BF16)) * dmask
        st = st_ref[h]
        xi = jnp.exp(lg * (pos + 1.0))
        o = _dot(scores.astype(BF16), vb) + _dot((qr * xi).astype(BF16), st.astype(BF16))
        zeta = jnp.exp(lg * (t - 1.0 - pos))
        st_ref[h] = st * math.exp(lg * t) + _dot_tn((kr * zeta).astype(BF16), vb)
        o_ref[:, hs] = (_rms(o) * _silu(g[:, hs])).astype(BF16)


def _ret(proj, cos, sin, batch, seq, t=256):
    nt = seq // t
    tab = pl.BlockSpec((1, t, RET_DK), lambda b, i: (b, i, 0))
    return pl.pallas_call(
        functools.partial(_ret_kernel, t=t),
        out_shape=jax.ShapeDtypeStruct((batch * seq, GROUP_WIDTH), BF16),
        grid=(batch, nt),
        in_specs=[_col_spec(t, 512, "b_q", nt), _col_spec(t, 512, "b_k", nt),
                  _col_spec(t, 512, "b_v", nt), _col_spec(t, 512, "b_g", nt), tab, tab],
        out_specs=_mixer_out_spec(t, nt),
        scratch_shapes=[pltpu.VMEM((RET_HEADS, RET_DK, RET_DK), F32)],
        compiler_params=_mixer_params(),
        name="ret_mixer",
    )(proj, proj, proj, proj, cos, sin)


def _rope_partial(x, c, s_lo, s_hi):
    n = x.shape[-1]
    half = ROPE_DIM // 2
    return x * c + pltpu.roll(x, n - half, axis=1) * s_lo + pltpu.roll(x, half, axis=1) * s_hi


def _swa_kernel(q_ref, kc_ref, vc_ref, c_ref, slo_ref, shi_ref, sink_ref, o_ref, kprev_ref, vprev_ref):
    w = WINDOW
    has_prev = pl.program_id(1) > 0

    @pl.when(pl.program_id(1) == 0)
    def _():
        kprev_ref[...] = jnp.zeros_like(kprev_ref)
        vprev_ref[...] = jnp.zeros_like(vprev_ref)

    c = c_ref[0]
    s_lo = slo_ref[0]
    s_hi = shi_ref[0]
    kc = _rope_partial(kc_ref[...], c, s_lo, s_hi)
    vc = vc_ref[...]
    lane = lax.broadcasted_iota(jnp.int32, (1, LANES), 1)
    lo_half = lane < SWA_HD
    group = SWA_Q_HEADS // SWA_KV_HEADS
    rowm = lax.broadcasted_iota(jnp.int32, (group * w, w), 0) & (w - 1)
    colm = lax.broadcasted_iota(jnp.int32, (group * w, w), 1)
    cur_ok = colm <= rowm
    prev_ok = (colm > rowm) & has_prev
    sinks = sink_ref[...]

    def both_halves(a, kv):
        swapped = pltpu.roll(a, SWA_HD, axis=1)
        return jnp.where(lo_half, a, swapped) if kv == 0 else jnp.where(lo_half, swapped, a)

    for kv in range(SWA_KV_HEADS):
        k2c = both_halves(kc, kv).astype(BF16)
        k2p = kprev_ref[kv]
        v2c = both_halves(vc, kv)
        v_half = [(jnp.where(lo_half, v2c, jnp.where(lane == SWA_HD, 1.0, 0.0)).astype(BF16), vprev_ref[kv, 0]),
                  (jnp.where(lo_half, jnp.where(lane == 0, 1.0, 0.0), v2c).astype(BF16), vprev_ref[kv, 1])]
        ones_lane = (SWA_HD, 0)
        kprev_ref[kv] = k2c
        vprev_ref[kv, 0] = v_half[0][0]
        vprev_ref[kv, 1] = v_half[1][0]
        q_rows, sink_rows = [], []
        for t in range(kv * group // 2, (kv + 1) * group // 2):
            ls = slice(t * LANES, (t + 1) * LANES)
            qt = _rope_partial(q_ref[:, ls], c, s_lo, s_hi) * (SWA_HD ** -0.5)
            q_rows += [jnp.where(lo_half, qt, 0.0), jnp.where(lo_half, 0.0, qt)]
            sink_rows += [jnp.broadcast_to(sinks[:, 2 * t:2 * t + 1], (w, 1)),
                          jnp.broadcast_to(sinks[:, 2 * t + 1:2 * t + 2], (w, 1))]
        q4 = jnp.concatenate(q_rows, axis=0).astype(BF16)
        sink = jnp.concatenate(sink_rows, axis=0)
        s_c = jnp.where(cur_ok, _dot_nt(q4, k2c), -jnp.inf)
        s_p = jnp.where(prev_ok, _dot_nt(q4, k2p), -jnp.inf)
        m = jnp.maximum(jnp.max(jnp.maximum(s_c, s_p), axis=-1, keepdims=True), sink)
        e_c = jnp.exp(s_c - m).astype(BF16)
        e_p = jnp.exp(s_p - m).astype(BF16)
        e_sink = jnp.exp(sink - m)
        for n, t in enumerate(range(kv * group // 2, (kv + 1) * group // 2)):
            halves = []
            for half in range(2):
                rs = slice((2 * n + half) * w, (2 * n + half + 1) * w)
                vh_c, vh_p = v_half[half]
                o = _dot(e_c[rs], vh_c) + _dot(e_p[rs], vh_p)
                denom = o[:, ones_lane[half]:ones_lane[half] + 1] + e_sink[rs]
                halves.append(o * (1.0 / denom))
            o_ref[:, t * LANES:(t + 1) * LANES] = jnp.where(lo_half, halves[0], halves[1]).astype(BF16)


def _swa(proj, tabs, sinks, batch, seq):
    w = WINDOW
    nt = seq // w
    kv_w = SWA_KV_HEADS * SWA_HD
    cur = lambda name: pl.BlockSpec((w, kv_w), lambda b, i: (b * nt + i, _DST[name] // kv_w))
    tab_c = pl.BlockSpec((1, w, LANES), lambda b, i: (b, i, 0))
    c, s_lo, s_hi = tabs
    return pl.pallas_call(
        _swa_kernel,
        out_shape=jax.ShapeDtypeStruct((batch * seq, GROUP_WIDTH), BF16),
        grid=(batch, nt),
        in_specs=[_col_spec(w, 512, "c_q", nt), cur("c_k"), cur("c_v"), tab_c, tab_c, tab_c,
                  pl.BlockSpec((1, SWA_Q_HEADS), lambda b, i: (0, 0))],
        out_specs=_mixer_out_spec(w, nt),
        scratch_shapes=[pltpu.VMEM((SWA_KV_HEADS, w, LANES), BF16),
                        pltpu.VMEM((SWA_KV_HEADS, 2, w, LANES), BF16)],
        compiler_params=_mixer_params(),
        name="swa_mixer",
    )(proj, proj, proj, c, s_lo, s_hi, sinks.reshape(1, -1))


def _out_proj_kernel(x_ref, mod_ref, a_ref, b_ref, c_ref, d_ref, w_ref, o_ref):
    gw = GROUP_WIDTH
    acc = _dot(a_ref[...], w_ref[0:gw, :])
    acc += _dot(b_ref[...], w_ref[gw:2 * gw, :])
    acc += _dot(c_ref[...], w_ref[2 * gw:3 * gw, :])
    acc += _dot(d_ref[...], w_ref[3 * gw:4 * gw, :])
    gate = mod_ref[0][2:3]
    o_ref[...] = x_ref[...] + gate * acc


def _out_proj(x2, mod_l, mixed, w_out, seq, tm=512):
    m_rows, d = x2.shape
    per_b = seq // tm
    grp = pl.BlockSpec((tm, GROUP_WIDTH), lambda i: (i, 0))
    return pl.pallas_call(
        _out_proj_kernel,
        out_shape=jax.ShapeDtypeStruct((m_rows, d), F32),
        grid=(m_rows // tm,),
        in_specs=[pl.BlockSpec((tm, d), lambda i: (i, 0)),
                  pl.BlockSpec((1, 6, d), lambda i: (i // per_b, 0, 0)),
                  grp, grp, grp, grp,
                  pl.BlockSpec((d, d), lambda i: (0, 0))],
        out_specs=pl.BlockSpec((tm, d), lambda i: (i, 0)),
        compiler_params=pltpu.CompilerParams(
            dimension_semantics=("arbitrary",), vmem_limit_bytes=VMEM_LIMIT),
        name="out_proj",
    )(x2, mod_l, *mixed, w_out)


def _ffn_kernel(x_ref, mod_ref, nw_ref, wg_ref, wu_ref, wd_ref, o_ref, h_ref, acc_ref):
    f = pl.program_id(1)

    @pl.when(f == 0)
    def _():
        m = mod_ref[0]
        h = _modulated_norm(x_ref[...], nw_ref[...], m[3:4], m[4:5])
        h_ref[...] = h.astype(BF16)
        acc_ref[...] = jnp.zeros_like(acc_ref)

    h = h_ref[...]
    g = _dot(h, wg_ref[...])
    u = _dot(h, wu_ref[...])
    act = (_silu(g) * u).astype(BF16)
    acc_ref[...] += _dot(act, wd_ref[...])

    @pl.when(f == pl.num_programs(1) - 1)
    def _():
        o_ref[...] = x_ref[...] + mod_ref[0][5:6] * acc_ref[...]


def _ffn(x2, mod_l, nw, w_in, w_down, seq, tm=512, tf=512):
    m_rows, d = x2.shape
    nf = D_FF // tf
    per_b = seq // tm
    return pl.pallas_call(
        _ffn_kernel,
        out_shape=jax.ShapeDtypeStruct((m_rows, d), F32),
        grid=(m_rows // tm, nf),
        in_specs=[pl.BlockSpec((tm, d), lambda i, f: (i, 0)),
                  pl.BlockSpec((1, 6, d), lambda i, f: (i // per_b, 0, 0)),
                  pl.BlockSpec((1, d), lambda i, f: (0, 0)),
                  pl.BlockSpec((d, tf), lambda i, f: (0, f)),
                  pl.BlockSpec((d, tf), lambda i, f: (0, nf + f)),
                  pl.BlockSpec((tf, d), lambda i, f: (f, 0))],
        out_specs=pl.BlockSpec((tm, d), lambda i, f: (i, 0)),
        scratch_shapes=[pltpu.VMEM((tm, d), BF16), pltpu.VMEM((tm, d), F32)],
        compiler_params=pltpu.CompilerParams(
            dimension_semantics=("arbitrary", "arbitrary"), vmem_limit_bytes=VMEM_LIMIT),
        name="ffn",
    )(x2, mod_l, nw.reshape(1, d), w_in, w_in, w_down)


def _final_norm_kernel(x_ref, w_ref, o_ref):
    o_ref[...] = _rms(x_ref[...]) * w_ref[...]


def _final_norm(x2, w, tm=512):
    m_rows, d = x2.shape
    return pl.pallas_call(
        _final_norm_kernel,
        out_shape=jax.ShapeDtypeStruct((m_rows, d), F32),
        grid=(m_rows // tm,),
        in_specs=[pl.BlockSpec((tm, d), lambda i: (i, 0)), pl.BlockSpec((1, d), lambda i: (0, 0))],
        out_specs=pl.BlockSpec((tm, d), lambda i: (i, 0)),
        compiler_params=pltpu.CompilerParams(
            dimension_semantics=("arbitrary",), vmem_limit_bytes=VMEM_LIMIT),
        name="final_norm",
    )(x2, w.reshape(1, d))


def _permute_w_in(w):
    parts = [w[:, _SRC[name]:_SRC[name] + width].astype(BF16) for name, width in _DST_ORDER]
    pad = PROJ_WIDTH - sum(width for _, width in _DST_ORDER)
    parts.append(jnp.zeros((w.shape[0], pad), BF16))
    return jnp.concatenate(parts, axis=1)


def _ret_tables(positions):
    inv = 1.0 / jnp.power(RET_ROT_BASE, jnp.linspace(0.0, 1.0, RET_DK // 2, dtype=F32))
    ang = positions.astype(F32)[:, :, None] * inv[None, None, :]
    cos, sin = jnp.cos(ang), jnp.sin(ang)
    return jnp.concatenate([cos, cos], axis=-1), jnp.concatenate([-sin, sin], axis=-1)


def _swa_tables(positions):
    half = ROPE_DIM // 2
    inv = 1.0 / jnp.power(ROPE_THETA, jnp.arange(half, dtype=F32) / half)
    ang = positions.astype(F32)[:, :, None] * inv[None, None, :]
    cos, sin = jnp.cos(ang), jnp.sin(ang)
    rest = SWA_HD - 2 * half
    shape = ang.shape[:-1]
    c = jnp.concatenate([cos, cos, jnp.ones(shape + (rest,), F32)], axis=-1)
    s_lo = jnp.concatenate([-sin, jnp.zeros(shape + (SWA_HD - half,), F32)], axis=-1)
    s_hi = jnp.concatenate([jnp.zeros(shape + (half,), F32), sin, jnp.zeros(shape + (rest,), F32)], axis=-1)
    tile = lambda a: jnp.concatenate([a] * (LANES // SWA_HD), axis=-1)
    return tile(c), tile(s_lo), tile(s_hi)


def kernel(x, c, positions, w_ada, b_ada, norm1_w, w_in, gla_gate_w2, gla_gate_b2, gla_norm_w, swa_sinks, hgrn_lb, hgrn_norm_w, w_out, norm2_w, w_ffn_in, w_ffn_down, final_norm_w):
    batch, seq, d = x.shape
    depth = w_ada.shape[0]
    mod = _ada_modulation(c, w_ada, b_ada).reshape(depth, batch, 6, d)
    ret_cos, ret_sin = _ret_tables(positions)
    swa_tabs = _swa_tables(positions)
    x2 = x.reshape(batch * seq, d)
    for l in range(depth):
        w2_pad = jnp.zeros((LANES, GLA_HEADS * GLA_DK), F32).at[:GLA_GATE_RANK].set(gla_gate_w2[l])
        proj = _in_proj(x2, mod[l], norm1_w[l], _permute_w_in(w_in[l]), seq)
        mixed = (
            _gla(proj, w2_pad, gla_gate_b2[l], gla_norm_w[l], batch, seq),
            _ret(proj, ret_cos, ret_sin, batch, seq),
            _swa(proj, swa_tabs, swa_sinks[l], batch, seq),
            _hgrn(proj, hgrn_lb, hgrn_norm_w[l], l, batch, seq),
        )
        x2 = _out_proj(x2, mod[l], mixed, w_out[l].astype(BF16), seq)
        x2 = _ffn(x2, mod[l], norm2_w[l], w_ffn_in[l].astype(BF16), w_ffn_down[l].astype(BF16), seq)
    return _final_norm(x2, final_norm_w).reshape(batch, seq, d)
```

```python
import functools
import math

import jax
import jax.numpy as jnp
from jax import lax
from jax.experimental import pallas as pl
from jax.experimental.pallas import tpu as pltpu

F32 = jnp.float32
BF16 = jnp.bfloat16

D_MODEL = 2048
DEPTH = 2
EPS = 1e-6
GROUP_WIDTH = D_MODEL // 4
CHUNK = 64
SUB = 16
N_SUB = CHUNK // SUB
GLA_HEADS = 4
GLA_DV = GROUP_WIDTH // GLA_HEADS
GLA_DK = GLA_DV // 2
GLA_GATE_RANK = 16
GLA_GATE_NORMALIZER = 16.0
RET_HEADS = 4
RET_DK = GROUP_WIDTH // RET_HEADS
RET_ROT_BASE = 10000.0
SWA_Q_HEADS = 8
SWA_KV_HEADS = 2
SWA_HD = GROUP_WIDTH // SWA_Q_HEADS
WINDOW = 128
ROPE_THETA = 500000.0
ROPE_DIM = SWA_HD // 4
HG_HEADS = 4
HG_DK = GROUP_WIDTH // HG_HEADS
D_FF = ((8 * D_MODEL + 3 * 256 - 1) // (3 * 256)) * 256

LANES = 128
VMEM_LIMIT = 48 * 1024 * 1024

_SRC = dict(a_q=0, a_k=256, a_v=512, a_g=1024, a_r=1536,
            b_q=1552, b_k=2064, b_v=2576, b_g=3088,
            c_q=3600, c_k=4112, c_v=4240,
            d_q=4368, d_f=4880, d_i=5392, d_g=5904)
_DST_ORDER = (("a_q", 256), ("a_k", 256), ("a_v", 512), ("a_g", 512), ("c_q", 512),
              ("b_q", 512), ("b_k", 512), ("b_v", 512), ("b_g", 512),
              ("d_q", 512), ("d_f", 512), ("d_i", 512), ("d_g", 512),
              ("c_k", 128), ("c_v", 128), ("a_r", GLA_GATE_RANK))
PROJ_WIDTH = 6528
PROJ_TN = PROJ_WIDTH // 3


def _dst_offsets():
    offs, o = {}, 0
    for name, w in _DST_ORDER:
        offs[name] = o
        o += w
    return offs


_DST = _dst_offsets()


def _dot(a, b):
    return jnp.dot(a, b, preferred_element_type=F32)


def _dot_nt(a, b):
    return lax.dot_general(a, b, (((1,), (1,)), ((), ())), preferred_element_type=F32)


def _dot_tn(a, b):
    return lax.dot_general(a, b, (((0,), (0,)), ((), ())), preferred_element_type=F32)


def _split_bf16(x):
    hi = x.astype(BF16)
    lo = (x - hi.astype(F32)).astype(BF16)
    return hi, lo


def _dot3(a, b):
    a_hi, a_lo = _split_bf16(a)
    b_hi, b_lo = _split_bf16(b)
    return _dot(a_hi, b_hi) + (_dot(a_hi, b_lo) + _dot(a_lo, b_hi))


def _sigmoid(x):
    return 1.0 / (1.0 + jnp.exp(-x))


def _silu(x):
    return (0.5 * x) * (1.0 + jnp.tanh(0.5 * x))


def _log_sigmoid(x):
    return jnp.minimum(x, 0.0) - jnp.log1p(jnp.exp(-jnp.abs(x)))


def _rms(x):
    return x * lax.rsqrt(jnp.mean(x * x, axis=-1, keepdims=True) + EPS)


def _ada_kernel(c_ref, w_ref, b_ref, o_ref):
    cond = _silu(c_ref[...])
    o_ref[0] = _dot3(cond, w_ref[0]) + b_ref[0]


def _ada_modulation(c, w_ada, b_ada):
    depth, d, n = w_ada.shape
    rows = 8
    c_pad = jnp.zeros((rows, d), F32).at[:c.shape[0]].set(c)
    tn = 1024
    out = pl.pallas_call(
        _ada_kernel,
        out_shape=jax.ShapeDtypeStruct((depth, rows, n), F32),
        grid=(depth, n // tn),
        in_specs=[pl.BlockSpec((rows, d), lambda l, j: (0, 0)),
                  pl.BlockSpec((1, d, tn), lambda l, j: (l, 0, j)),
                  pl.BlockSpec((1, 1, tn), lambda l, j: (l, 0, j))],
        out_specs=pl.BlockSpec((1, rows, tn), lambda l, j: (l, 0, j)),
        compiler_params=pltpu.CompilerParams(
            dimension_semantics=("arbitrary", "arbitrary"), vmem_limit_bytes=VMEM_LIMIT),
        name="ada_modulation",
    )(c_pad, w_ada, b_ada.reshape(depth, 1, n))
    return out[:, :c.shape[0]]


def _modulated_norm(x, nw, shift, scale):
    return (_rms(x) * nw) * (1.0 + scale) + shift


def _in_proj_kernel(x_ref, mod_ref, nw_ref, w_ref, o_ref, h_ref):
    @pl.when(pl.program_id(1) == 0)
    def _():
        m = mod_ref[0]
        h = _modulated_norm(x_ref[...], nw_ref[...], m[0:1], m[1:2])
        h_ref[...] = h.astype(BF16)

    o_ref[...] = _dot(h_ref[...], w_ref[...])


def _in_proj(x2, mod_l, nw, w_perm, seq, tm=512):
    m_rows, d = x2.shape
    n = w_perm.shape[1]
    per_b = seq // tm
    return pl.pallas_call(
        _in_proj_kernel,
        out_shape=jax.ShapeDtypeStruct((m_rows, n), F32),
        grid=(m_rows // tm, n // PROJ_TN),
        in_specs=[pl.BlockSpec((tm, d), lambda i, j: (i, 0)),
                  pl.BlockSpec((1, 6, d), lambda i, j: (i // per_b, 0, 0)),
                  pl.BlockSpec((1, d), lambda i, j: (0, 0)),
                  pl.BlockSpec((d, PROJ_TN), lambda i, j: (0, j))],
        out_specs=pl.BlockSpec((tm, PROJ_TN), lambda i, j: (i, j)),
        scratch_shapes=[pltpu.VMEM((tm, d), BF16)],
        compiler_params=pltpu.CompilerParams(
            dimension_semantics=("arbitrary", "arbitrary"), vmem_limit_bytes=VMEM_LIMIT),
        name="in_proj",
    )(x2, mod_l, nw.reshape(1, d), w_perm)


TILE = 2 * CHUNK
GLA_BLOCK = CHUNK
HG_BLOCK = SUB
CHUNK_SHIFT = CHUNK.bit_length() - 1
SUB_SHIFT = SUB.bit_length() - 1
SUBLANES = 8
SLAB = 64
DIAG_RUN = 4
assert DIAG_RUN == 4


def _ones_where(mask):
    return jnp.where(mask, 1.0, 0.0).astype(BF16)


def _levels(block):
    return [hs for hs in (CHUNK // 2, CHUNK // 4) if hs >= block]


def _gate_constants(block):
    i = lax.broadcasted_iota(jnp.int32, (TILE, TILE), 0)
    m = lax.broadcasted_iota(jnp.int32, (TILE, TILE), 1)
    same_chunk = (i >> CHUNK_SHIFT) == (m >> CHUNK_SHIFT)
    sub_end = i | (SUB - 1)
    chunk_start = i & ~(CHUNK - 1)
    after = same_chunk & (m > i)
    mats = [same_chunk & (m <= i),
            after & (m <= sub_end),
            after]
    for j in range(N_SUB - 1):
        mats.append(same_chunk & (m > chunk_start + (SUB * j + SUB - 1)) & (m <= i))
    cum = jnp.concatenate([_ones_where(x) for x in mats], axis=0)
    place = _ones_where(m == ((TILE - (i >> 3)) & (TILE - 1)))
    diag_ok = ((i >> SUB_SHIFT) == (m >> SUB_SHIFT)) & (m <= i)
    off_ok = same_chunk & ((m >> SUB_SHIFT) < (i >> SUB_SHIFT))
    sub_of_row = (lax.broadcasted_iota(jnp.int32, (TILE, 1), 0) >> SUB_SHIFT) & (N_SUB - 1)
    score_masks = []
    for hs in _levels(block):
        same = (i >> (2 * hs).bit_length() - 1) == (m >> (2 * hs).bit_length() - 1)
        score_masks.append(same & ((i & (2 * hs - 1)) >= hs) & ((m & (2 * hs - 1)) < hs))
    score_masks.append(((i >> block.bit_length() - 1) == (m >> block.bit_length() - 1)) & (m <= i))
    return cum, place, diag_ok, off_ok, sub_of_row, score_masks


def _transpose_tiles(x):
    return jnp.concatenate([x[:, t:t + LANES].T for t in range(0, x.shape[1], LANES)], axis=0)


def _diag_partial_sums(q_t, k_t, g_t):
    dk = q_t.shape[0]
    slab = min(dk, SLAB)
    accs = [None] * SUB
    for s0 in range(0, dk, slab):
        qs = q_t[s0:s0 + slab]
        ks = k_t[s0:s0 + slab]
        g1 = g_t[s0:s0 + slab]
        g2 = g1 * pltpu.roll(g1, 1, axis=1)
        g4 = g2 * pltpu.roll(g2, 2, axis=1)
        g4_back = [g4] + [pltpu.roll(g4, DIAG_RUN * n, axis=1) for n in range(1, SUB // DIAG_RUN - 1)]
        for d0 in range(0, SUB, DIAG_RUN):
            w = ks
            if d0 > 0:
                w = pltpu.roll(ks, d0, axis=1)
                for n in range(d0 // DIAG_RUN):
                    w = w * g4_back[n]
            for d in range(d0, d0 + DIAG_RUN):
                if d > d0:
                    w = pltpu.roll(w, 1, axis=1) * g1
                p = jnp.sum((qs * w).reshape(slab // SUBLANES, SUBLANES, TILE), axis=0)
                accs[d] = p if accs[d] is None else accs[d] + p
    return jnp.concatenate(accs, axis=0)


def _split3_bf16(x):
    hi = x.astype(BF16)
    rest = x - hi.astype(F32)
    mid = rest.astype(BF16)
    lo = (rest - mid.astype(F32)).astype(BF16)
    return hi, mid, lo


def _cum_dot(mat, pieces):
    hi, mid, lo = pieces
    return _dot(mat, hi) + (_dot(mat, mid) + _dot(mat, lo))


def _head_lanes(h, dk):
    tile = (h * dk) // LANES
    if dk >= LANES:
        return tile, None
    lane = lax.broadcasted_iota(jnp.int32, (1, LANES), 1)
    start = (h * dk) % LANES
    return tile, (lane >= start) & (lane < start + dk)


def _gated_tile_robust(q, k, lg, lg_pieces, v, st_ref, consts, n_heads, dk, dv):
    cum, place, diag_ok, off_ok, sub_of_row, _ = consts
    sums = _cum_dot(cum, lg_pieces)
    cb = sums[0:TILE]
    q_state = (q * jnp.exp(cb)).astype(BF16)
    k_end = k * jnp.exp(sums[TILE:2 * TILE])
    k_last = (k * jnp.exp(sums[2 * TILE:3 * TILE])).astype(BF16)
    q_off = [(q * jnp.exp(sums[(3 + j) * TILE:(4 + j) * TILE])).astype(BF16) for j in range(N_SUB - 1)]
    k_off = [jnp.where(sub_of_row == j, k_end, 0.0).astype(BF16) for j in range(N_SUB - 1)]
    q_t = _transpose_tiles(q)
    k_t = _transpose_tiles(k)
    g_t = _transpose_tiles(jnp.exp(lg))
    outs = []
    for h in range(n_heads):
        ks = slice(h * dk, (h + 1) * dk)
        tile = (h * dk) // LANES
        ls = slice((h * dk) % LANES, (h * dk) % LANES + dk)
        vb = v[:, h * dv:(h + 1) * dv].astype(BF16)
        a_off = _dot_nt(q_off[0][:, ks], k_off[0][:, ks])
        for j in range(1, N_SUB - 1):
            a_off += _dot_nt(q_off[j][:, ks], k_off[j][:, ks])
        part = _diag_partial_sums(q_t[ks], k_t[ks], g_t[ks])
        p_hi, p_lo = _split_bf16(part)
        skew = _dot_tn(p_hi, place) + _dot_tn(p_lo, place)
        a_diag = pltpu.roll(skew, 0, axis=1, stride=1, stride_axis=0)
        a = jnp.where(diag_ok, a_diag, jnp.where(off_ok, a_off, 0.0)).astype(BF16)
        o = _dot(a, vb)
        st = st_ref[tile][:, ls]
        inter = []
        for c in range(TILE // CHUNK):
            rs = slice(c * CHUNK, (c + 1) * CHUNK)
            inter.append(_dot_nt(q_state[rs, ks], st.astype(BF16)))
            last = cb[c * CHUNK + CHUNK - 1:(c + 1) * CHUNK, ks]
            st = st * jnp.exp(last) + _dot_tn(vb[rs], k_last[rs, ks])
        st_ref[tile, :, ls] = st
        outs.append(o + jnp.concatenate(inter, axis=0))
    return outs


def _rows_from(cb, row, spans):
    out = None
    for lo, src in spans:
        val = jnp.zeros_like(cb[0:1]) if src is None else cb[src:src + 1]
        out = val if out is None else jnp.where(row >= lo, val, out)
    return out


def _block_prefix(cb, block):
    if block == CHUNK:
        return cb
    row = lax.broadcasted_iota(jnp.int32, (TILE, 1), 0)
    spans = [(b0, None if b0 % CHUNK == 0 else b0 - 1) for b0 in range(0, TILE, block)]
    return cb - _rows_from(cb, row, spans)


def _gated_tile_bounded(q, k, cb, pb, v, st_ref, consts, n_heads, dk, dv, block):
    score_masks = consts[-1]
    row = lax.broadcasted_iota(jnp.int32, (TILE, 1), 0)
    ends = [cb[c * CHUNK + CHUNK - 1:(c + 1) * CHUNK, :] for c in range(TILE // CHUNK)]
    end_of_row = _rows_from(cb, row, [(c * CHUNK, c * CHUNK + CHUNK - 1) for c in range(TILE // CHUNK)])
    q_state = q * jnp.exp(cb)
    k_last = k * jnp.exp(end_of_row - cb)
    q_parts, k_parts = [], []
    for hs in _levels(block):
        ref = _rows_from(cb, row, [(b0, b0 + hs - 1) for b0 in range(0, TILE, 2 * hs)])
        lower = (row & (2 * hs - 1)) < hs
        x = jnp.exp(jnp.where(lower, ref - cb, cb - ref))
        q_parts.append(jnp.where(lower, 0.0, q * x))
        k_parts.append(jnp.where(lower, k * x, 0.0).astype(BF16))
    q_parts.append(q_state if block == CHUNK else q * jnp.exp(pb))
    k_parts.append((k * jnp.exp(-pb)).astype(BF16))
    outs = [None] * n_heads
    heads_per_tile = max(LANES // dk, 1)
    for tile in range(n_heads * dk // LANES):
        ls = slice(tile * LANES, (tile + 1) * LANES)
        heads = range(tile * heads_per_tile, (tile + 1) * heads_per_tile)
        qh, kl, vb, intra = {}, {}, {}, {}
        for h in heads:
            _, mask = _head_lanes(h, dk)
            own = (lambda a: a) if mask is None else (lambda a: jnp.where(mask, a, 0.0))
            qh[h] = own(q_state[:, ls]).astype(BF16)
            kl[h] = own(k_last[:, ls]).astype(BF16)
            vb[h] = v[:, h * dv:(h + 1) * dv].astype(BF16)
            a = 0.0
            for qp, kp, ok in zip(q_parts, k_parts, score_masks):
                a = jnp.where(ok, _dot_nt(own(qp[:, ls]).astype(BF16), kp[:, ls]), a)
            intra[h] = _dot(a.astype(BF16), vb[h])
        st = st_ref[tile]
        inter = {h: [] for h in heads}
        for c in range(TILE // CHUNK):
            rs = slice(c * CHUNK, (c + 1) * CHUNK)
            stb = st.astype(BF16)
            upd = None
            for h in heads:
                inter[h].append(_dot_nt(qh[h][rs], stb))
                u = _dot_tn(vb[h][rs], kl[h][rs])
                upd = u if upd is None else upd + u
            st = st * jnp.exp(ends[c][:, ls]) + upd
        st_ref[tile] = st
        for h in heads:
            outs[h] = intra[h] + jnp.concatenate(inter[h], axis=0)
    return outs


SAFE_DECAY = 64.0


def _gated_tile(q, k, lg, v, st_ref, consts, n_heads, dk, dv, block, finish):
    pieces = _split3_bf16(lg)
    cb = _cum_dot(consts[0][0:TILE], pieces)
    pb = _block_prefix(cb, block)
    bounded = jnp.min(pb) >= -SAFE_DECAY

    @pl.when(bounded)
    def _():
        finish(_gated_tile_bounded(q, k, cb, pb, v, st_ref, consts, n_heads, dk, dv, block))

    @pl.when(jnp.logical_not(bounded))
    def _():
        finish(_gated_tile_robust(q, k, lg, pieces, v, st_ref, consts, n_heads, dk, dv))


def _gla_kernel(q_ref, k_ref, v_ref, g_ref, r_ref, w2_ref, b2_ref, nw_ref, o_ref, st_ref, *, n_tiles):
    @pl.when(pl.program_id(1) == 0)
    def _():
        st_ref[...] = jnp.zeros_like(st_ref)

    consts = _gate_constants(GLA_BLOCK)
    w2 = w2_ref[...]
    b2 = b2_ref[...]
    nw = nw_ref[...]

    def body(c, carry):
        rows = pl.ds(pl.multiple_of(c * TILE, TILE), TILE)
        pre = _dot3(r_ref[rows, :], w2) + b2
        lg = _log_sigmoid(pre) * (1.0 / GLA_GATE_NORMALIZER)
        q = q_ref[rows, :] * (GLA_DK ** -0.5)

        def finish(outs):
            g = g_ref[rows, :]
            for h, o in enumerate(outs):
                vs = slice(h * GLA_DV, (h + 1) * GLA_DV)
                o_ref[rows, vs] = ((_rms(o) * nw) * _silu(g[:, vs])).astype(BF16)

        _gated_tile(q, k_ref[rows, :], lg, v_ref[rows, :], st_ref, consts,
                    GLA_HEADS, GLA_DK, GLA_DV, GLA_BLOCK, finish)
        return carry

    lax.fori_loop(0, n_tiles, body, 0)


def _hgrn_kernel(q_ref, f_ref, i_ref, g_ref, lb_ref, nw_ref, o_ref, st_ref, *, n_tiles, layer):
    @pl.when(pl.program_id(1) == 0)
    def _():
        st_ref[...] = jnp.zeros_like(st_ref)

    consts = _gate_constants(HG_BLOCK)
    lbv = lb_ref[...]
    e = jnp.exp(lbv - jnp.max(lbv, axis=0, keepdims=True))
    soft = e / jnp.sum(e, axis=0, keepdims=True)
    cum = soft[0:1]
    for i in range(1, layer + 1):
        cum = cum + soft[i:i + 1]
    lb = cum - soft[0:1]
    nw = nw_ref[...]

    def body(c, carry):
        rows = pl.ds(pl.multiple_of(c * TILE, TILE), TILE)
        f = lb + (1.0 - lb) * _sigmoid(f_ref[rows, :])
        q = _silu(q_ref[rows, :]) * (HG_DK ** -0.5)

        def finish(outs):
            g = g_ref[rows, :]
            for h, o in enumerate(outs):
                hs = slice(h * HG_DK, (h + 1) * HG_DK)
                o_ref[rows, hs] = ((_rms(o) * nw) * _silu(g[:, hs])).astype(BF16)

        _gated_tile(q, 1.0 - f, jnp.log(f), i_ref[rows, :], st_ref, consts,
                    HG_HEADS, HG_DK, HG_DK, HG_BLOCK, finish)
        return carry

    lax.fori_loop(0, n_tiles, body, 0)


def _col_spec(t, width, name, nt):
    blk = _DST[name] // width
    return pl.BlockSpec((t, width), lambda b, i: (b * nt + i, blk))


def _mixer_out_spec(t, nt):
    return pl.BlockSpec((t, GROUP_WIDTH), lambda b, i: (b * nt + i, 0))


def _mixer_params():
    return pltpu.CompilerParams(dimension_semantics=("arbitrary", "arbitrary"),
                                vmem_limit_bytes=VMEM_LIMIT)


def _gla(proj, w2_pad, b2, nw, batch, seq, t=512):
    nt = seq // t
    full = lambda shape: pl.BlockSpec(shape, lambda b, i: (0, 0))
    return pl.pallas_call(
        functools.partial(_gla_kernel, n_tiles=t // TILE),
        out_shape=jax.ShapeDtypeStruct((batch * seq, GROUP_WIDTH), BF16),
        grid=(batch, nt),
        in_specs=[_col_spec(t, 256, "a_q", nt), _col_spec(t, 256, "a_k", nt),
                  _col_spec(t, 512, "a_v", nt), _col_spec(t, 512, "a_g", nt),
                  _col_spec(t, LANES, "a_r", nt),
                  full((LANES, GLA_HEADS * GLA_DK)), full((1, GLA_HEADS * GLA_DK)), full((1, GLA_DV))],
        out_specs=_mixer_out_spec(t, nt),
        scratch_shapes=[pltpu.VMEM((GLA_HEADS * GLA_DK // LANES, GLA_DV, LANES), F32)],
        compiler_params=_mixer_params(),
        name="gla_mixer",
    )(proj, proj, proj, proj, proj, w2_pad, b2.reshape(1, -1), nw.reshape(1, -1))


def _hgrn(proj, hgrn_lb, nw, layer, batch, seq, t=512):
    nt = seq // t
    full = lambda shape: pl.BlockSpec(shape, lambda b, i: (0, 0))
    return pl.pallas_call(
        functools.partial(_hgrn_kernel, n_tiles=t // TILE, layer=layer),
        out_shape=jax.ShapeDtypeStruct((batch * seq, GROUP_WIDTH), BF16),
        grid=(batch, nt),
        in_specs=[_col_spec(t, 512, "d_q", nt), _col_spec(t, 512, "d_f", nt),
                  _col_spec(t, 512, "d_i", nt), _col_spec(t, 512, "d_g", nt),
                  full(hgrn_lb.shape), full((1, HG_DK))],
        out_specs=_mixer_out_spec(t, nt),
        scratch_shapes=[pltpu.VMEM((HG_HEADS * HG_DK // LANES, HG_DK, LANES), F32)],
        compiler_params=_mixer_params(),
        name="hgrn_mixer",
    )(proj, proj, proj, proj, hgrn_lb, nw.reshape(1, -1))


def _ret_kernel(q_ref, k_ref, v_ref, g_ref, cos_ref, sin_ref, o_ref, st_ref, *, t):
    @pl.when(pl.program_id(1) == 0)
    def _():
        st_ref[...] = jnp.zeros_like(st_ref)

    cos = cos_ref[0]
    sin = sin_ref[0]
    q = q_ref[...]
    k = k_ref[...]
    v = v_ref[...]
    g = g_ref[...]
    rowm = lax.broadcasted_iota(jnp.int32, (t, t), 0)
    colm = lax.broadcasted_iota(jnp.int32, (t, t), 1)
    rel = (rowm - colm).astype(F32)
    pos = lax.broadcasted_iota(jnp.int32, (t, RET_DK), 0).astype(F32)
    half = RET_DK // 2
    for h in range(RET_HEADS):
        lg = math.log1p(-(2.0 ** (-5.0 - h)))
        hs = slice(h * RET_DK, (h + 1) * RET_DK)
        qh = q[:, hs]
        kh = k[:, hs]
        qr = qh * cos + pltpu.roll(qh, half, axis=1) * sin
        kr = (kh * cos + pltpu.roll(kh, half, axis=1) * sin) * (RET_DK ** -0.5)
        vb = v[:, hs].astype(BF16)
        dmask = jnp.where(rel >= 0, jnp.exp(jnp.minimum(lg * rel, 0.0)), 0.0)
        scores = _dot_nt(qr.astype(BF16), kr.astype(BF16)) * dmask
        st = st_ref[h]
        xi = jnp.exp(lg * (pos + 1.0))
        o = _dot(scores.astype(BF16), vb) + _dot((qr * xi).astype(BF16), st.astype(BF16))
        zeta = jnp.exp(lg * (t - 1.0 - pos))
        st_ref[h] = st * math.exp(lg * t) + _dot_tn((kr * zeta).astype(BF16), vb)
        o_ref[:, hs] = (_rms(o) * _silu(g[:, hs])).astype(BF16)


def _ret(proj, cos, sin, batch, seq, t=256):
    nt = seq // t
    tab = pl.BlockSpec((1, t, RET_DK), lambda b, i: (b, i, 0))
    return pl.pallas_call(
        functools.partial(_ret_kernel, t=t),
        out_shape=jax.ShapeDtypeStruct((batch * seq, GROUP_WIDTH), BF16),
        grid=(batch, nt),
        in_specs=[_col_spec(t, 512, "b_q", nt), _col_spec(t, 512, "b_k", nt),
                  _col_spec(t, 512, "b_v", nt), _col_spec(t, 512, "b_g", nt), tab, tab],
        out_specs=_mixer_out_spec(t, nt),
        scratch_shapes=[pltpu.VMEM((RET_HEADS, RET_DK, RET_DK), F32)],
        compiler_params=_mixer_params(),
        name="ret_mixer",
    )(proj, proj, proj, proj, cos, sin)


def _rope_partial(x, c, s_lo, s_hi):
    n = x.shape[-1]
    half = ROPE_DIM // 2
    return x * c + pltpu.roll(x, n - half, axis=1) * s_lo + pltpu.roll(x, half, axis=1) * s_hi


def _swa_kernel(q_ref, kc_ref, vc_ref, c_ref, slo_ref, shi_ref, sink_ref, o_ref, kprev_ref, vprev_ref):
    w = WINDOW
    has_prev = pl.program_id(1) > 0

    @pl.when(pl.program_id(1) == 0)
    def _():
        kprev_ref[...] = jnp.zeros_like(kprev_ref)
        vprev_ref[...] = jnp.zeros_like(vprev_ref)

    c = c_ref[0]
    s_lo = slo_ref[0]
    s_hi = shi_ref[0]
    kc = _rope_partial(kc_ref[...], c, s_lo, s_hi)
    vc = vc_ref[...]
    lane = lax.broadcasted_iota(jnp.int32, (1, LANES), 1)
    lo_half = lane < SWA_HD
    group = SWA_Q_HEADS // SWA_KV_HEADS
    rowm = lax.broadcasted_iota(jnp.int32, (group * w, w), 0) & (w - 1)
    colm = lax.broadcasted_iota(jnp.int32, (group * w, w), 1)
    cur_ok = colm <= rowm
    prev_ok = (colm > rowm) & has_prev
    sinks = sink_ref[...]

    def both_halves(a, kv):
        swapped = pltpu.roll(a, SWA_HD, axis=1)
        return jnp.where(lo_half, a, swapped) if kv == 0 else jnp.where(lo_half, swapped, a)

    for kv in range(SWA_KV_HEADS):
        k2c = both_halves(kc, kv).astype(BF16)
        k2p = kprev_ref[kv]
        v2c = both_halves(vc, kv)
        v_half = [(jnp.where(lo_half, v2c, jnp.where(lane == SWA_HD, 1.0, 0.0)).astype(BF16), vprev_ref[kv, 0]),
                  (jnp.where(lo_half, jnp.where(lane == 0, 1.0, 0.0), v2c).astype(BF16), vprev_ref[kv, 1])]
        ones_lane = (SWA_HD, 0)
        kprev_ref[kv] = k2c
        vprev_ref[kv, 0] = v_half[0][0]
        vprev_ref[kv, 1] = v_half[1][0]
        q_rows, sink_rows = [], []
        for t in range(kv * group // 2, (kv + 1) * group // 2):
            ls = slice(t * LANES, (t + 1) * LANES)
            qt = _rope_partial(q_ref[:, ls], c, s_lo, s_hi) * (SWA_HD ** -0.5)
            q_rows += [jnp.where(lo_half, qt, 0.0), jnp.where(lo_half, 0.0, qt)]
            sink_rows += [jnp.broadcast_to(sinks[:, 2 * t:2 * t + 1], (w, 1)),
                          jnp.broadcast_to(sinks[:, 2 * t + 1:2 * t + 2], (w, 1))]
        q4 = jnp.concatenate(q_rows, axis=0).astype(BF16)
        sink = jnp.concatenate(sink_rows, axis=0)
        s_c = jnp.where(cur_ok, _dot_nt(q4, k2c), -jnp.inf)
        s_p = jnp.where(prev_ok, _dot_nt(q4, k2p), -jnp.inf)
        m = jnp.maximum(jnp.max(jnp.maximum(s_c, s_p), axis=-1, keepdims=True), sink)
        e_c = jnp.exp(s_c - m).astype(BF16)
        e_p = jnp.exp(s_p - m).astype(BF16)
        e_sink = jnp.exp(sink - m)
        for n, t in enumerate(range(kv * group // 2, (kv + 1) * group // 2)):
            halves = []
            for half in range(2):
                rs = slice((2 * n + half) * w, (2 * n + half + 1) * w)
                vh_c, vh_p = v_half[half]
                o = _dot(e_c[rs], vh_c) + _dot(e_p[rs], vh_p)
                denom = o[:, ones_lane[half]:ones_lane[half] + 1] + e_sink[rs]
                halves.append(o * (1.0 / denom))
            o_ref[:, t * LANES:(t + 1) * LANES] = jnp.where(lo_half, halves[0], halves[1]).astype(BF16)


def _swa(proj, tabs, sinks, batch, seq):
    w = WINDOW
    nt = seq // w
    kv_w = SWA_KV_HEADS * SWA_HD
    cur = lambda name: pl.BlockSpec((w, kv_w), lambda b, i: (b * nt + i, _DST[name] // kv_w))
    tab_c = pl.BlockSpec((1, w, LANES), lambda b, i: (b, i, 0))
    c, s_lo, s_hi = tabs
    return pl.pallas_call(
        _swa_kernel,
        out_shape=jax.ShapeDtypeStruct((batch * seq, GROUP_WIDTH), BF16),
        grid=(batch, nt),
        in_specs=[_col_spec(w, 512, "c_q", nt), cur("c_k"), cur("c_v"), tab_c, tab_c, tab_c,
                  pl.BlockSpec((1, SWA_Q_HEADS), lambda b, i: (0, 0))],
        out_specs=_mixer_out_spec(w, nt),
        scratch_shapes=[pltpu.VMEM((SWA_KV_HEADS, w, LANES), BF16),
                        pltpu.VMEM((SWA_KV_HEADS, 2, w, LANES), BF16)],
        compiler_params=_mixer_params(),
        name="swa_mixer",
    )(proj, proj, proj, c, s_lo, s_hi, sinks.reshape(1, -1))


def _out_proj_kernel(x_ref, mod_ref, a_ref, b_ref, c_ref, d_ref, w_ref, o_ref):
    gw = GROUP_WIDTH
    acc = _dot(a_ref[...], w_ref[0:gw, :])
    acc += _dot(b_ref[...], w_ref[gw:2 * gw, :])
    acc += _dot(c_ref[...], w_ref[2 * gw:3 * gw, :])
    acc += _dot(d_ref[...], w_ref[3 * gw:4 * gw, :])
    gate = mod_ref[0][2:3]
    o_ref[...] = x_ref[...] + gate * acc


def _out_proj(x2, mod_l, mixed, w_out, seq, tm=512):
    m_rows, d = x2.shape
    per_b = seq // tm
    grp = pl.BlockSpec((tm, GROUP_WIDTH), lambda i: (i, 0))
    return pl.pallas_call(
        _out_proj_kernel,
        out_shape=jax.ShapeDtypeStruct((m_rows, d), F32),
        grid=(m_rows // tm,),
        in_specs=[pl.BlockSpec((tm, d), lambda i: (i, 0)),
                  pl.BlockSpec((1, 6, d), lambda i: (i // per_b, 0, 0)),
                  grp, grp, grp, grp,
                  pl.BlockSpec((d, d), lambda i: (0, 0))],
        out_specs=pl.BlockSpec((tm, d), lambda i: (i, 0)),
        compiler_params=pltpu.CompilerParams(
            dimension_semantics=("arbitrary",), vmem_limit_bytes=VMEM_LIMIT),
        name="out_proj",
    )(x2, mod_l, *mixed, w_out)


def _ffn_kernel(x_ref, mod_ref, nw_ref, wg_ref, wu_ref, wd_ref, fw_ref, o_ref, h_ref, acc_ref, *, final):
    f = pl.program_id(1)

    @pl.when(f == 0)
    def _():
        m = mod_ref[0]
        h = _modulated_norm(x_ref[...], nw_ref[...], m[3:4], m[4:5])
        h_ref[...] = h.astype(BF16)
        acc_ref[...] = jnp.zeros_like(acc_ref)

    h = h_ref[...]
    g = _dot(h, wg_ref[...])
    u = _dot(h, wu_ref[...])
    act = (_silu(g) * u).astype(BF16)
    acc_ref[...] += _dot(act, wd_ref[...])

    @pl.when(f == pl.num_programs(1) - 1)
    def _():
        y = x_ref[...] + mod_ref[0][5:6] * acc_ref[...]
        o_ref[...] = _rms(y) * fw_ref[...] if final else y


def _ffn(x2, mod_l, nw, w_in, w_down, final_w, final, seq, tm=512, tf=512):
    m_rows, d = x2.shape
    nf = D_FF // tf
    per_b = seq // tm
    return pl.pallas_call(
        functools.partial(_ffn_kernel, final=final),
        out_shape=jax.ShapeDtypeStruct((m_rows, d), F32),
        grid=(m_rows // tm, nf),
        in_specs=[pl.BlockSpec((tm, d), lambda i, f: (i, 0)),
                  pl.BlockSpec((1, 6, d), lambda i, f: (i // per_b, 0, 0)),
                  pl.BlockSpec((1, d), lambda i, f: (0, 0)),
                  pl.BlockSpec((d, tf), lambda i, f: (0, f)),
                  pl.BlockSpec((d, tf), lambda i, f: (0, nf + f)),
                  pl.BlockSpec((tf, d), lambda i, f: (f, 0)),
                  pl.BlockSpec((1, d), lambda i, f: (0, 0))],
        out_specs=pl.BlockSpec((tm, d), lambda i, f: (i, 0)),
        scratch_shapes=[pltpu.VMEM((tm, d), BF16), pltpu.VMEM((tm, d), F32)],
        compiler_params=pltpu.CompilerParams(
            dimension_semantics=("arbitrary", "arbitrary"), vmem_limit_bytes=VMEM_LIMIT),
        name="ffn",
    )(x2, mod_l, nw.reshape(1, d), w_in, w_in, w_down, final_w.reshape(1, d))


CAST_ROWS = 256
CAST_COLS = 2816


def _cast_kernel(x_ref, o_ref):
    o_ref[...] = x_ref[...].astype(BF16)


def _to_bf16(w):
    depth, k, n = w.shape
    tn = n if n <= CAST_COLS else CAST_COLS
    blk = pl.BlockSpec((1, CAST_ROWS, tn), lambda l, i, j: (l, i, j))
    return pl.pallas_call(
        _cast_kernel,
        out_shape=jax.ShapeDtypeStruct(w.shape, BF16),
        grid=(depth, k // CAST_ROWS, n // tn),
        in_specs=[blk], out_specs=blk,
        compiler_params=pltpu.CompilerParams(
            dimension_semantics=("arbitrary",) * 3, vmem_limit_bytes=VMEM_LIMIT),
        name="cast_weights",
    )(w)


def _permute_kernel(x_ref, o_ref):
    off = 0
    for name, width in _DST_ORDER:
        o_ref[0, :, off:off + width] = x_ref[0, :, _SRC[name]:_SRC[name] + width].astype(BF16)
        off += width
    o_ref[0, :, off:] = jnp.zeros((o_ref.shape[1], PROJ_WIDTH - off), BF16)


def _permute_w_in(w):
    depth, k, n = w.shape
    return pl.pallas_call(
        _permute_kernel,
        out_shape=jax.ShapeDtypeStruct((depth, k, PROJ_WIDTH), BF16),
        grid=(depth, k // CAST_ROWS),
        in_specs=[pl.BlockSpec((1, CAST_ROWS, n), lambda l, i: (l, i, 0))],
        out_specs=pl.BlockSpec((1, CAST_ROWS, PROJ_WIDTH), lambda l, i: (l, i, 0)),
        compiler_params=pltpu.CompilerParams(
            dimension_semantics=("arbitrary", "arbitrary"), vmem_limit_bytes=VMEM_LIMIT),
        name="permute_w_in",
    )(w)


def _ret_tables(positions):
    inv = 1.0 / jnp.power(RET_ROT_BASE, jnp.linspace(0.0, 1.0, RET_DK // 2, dtype=F32))
    ang = positions.astype(F32)[:, :, None] * inv[None, None, :]
    cos, sin = jnp.cos(ang), jnp.sin(ang)
    return jnp.concatenate([cos, cos], axis=-1), jnp.concatenate([-sin, sin], axis=-1)


def _swa_tables(positions):
    half = ROPE_DIM // 2
    inv = 1.0 / jnp.power(ROPE_THETA, jnp.arange(half, dtype=F32) / half)
    ang = positions.astype(F32)[:, :, None] * inv[None, None, :]
    cos, sin = jnp.cos(ang), jnp.sin(ang)
    rest = SWA_HD - 2 * half
    shape = ang.shape[:-1]
    c = jnp.concatenate([cos, cos, jnp.ones(shape + (rest,), F32)], axis=-1)
    s_lo = jnp.concatenate([-sin, jnp.zeros(shape + (SWA_HD - half,), F32)], axis=-1)
    s_hi = jnp.concatenate([jnp.zeros(shape + (half,), F32), sin, jnp.zeros(shape + (rest,), F32)], axis=-1)
    tile = lambda a: jnp.concatenate([a] * (LANES // SWA_HD), axis=-1)
    return tile(c), tile(s_lo), tile(s_hi)


def kernel(x, c, positions, w_ada, b_ada, norm1_w, w_in, gla_gate_w2, gla_gate_b2, gla_norm_w, swa_sinks, hgrn_lb, hgrn_norm_w, w_out, norm2_w, w_ffn_in, w_ffn_down, final_norm_w):
    batch, seq, d = x.shape
    depth = w_ada.shape[0]
    mod = _ada_modulation(c, w_ada, b_ada).reshape(depth, batch, 6, d)
    ret_cos, ret_sin = _ret_tables(positions)
    swa_tabs = _swa_tables(positions)
    x2 = x.reshape(batch * seq, d)
    w_in_b, w_out_b = _permute_w_in(w_in), _to_bf16(w_out)
    w_ffn_in_b, w_ffn_down_b = _to_bf16(w_ffn_in), _to_bf16(w_ffn_down)
    for l in range(depth):
        w2_pad = jnp.zeros((LANES, GLA_HEADS * GLA_DK), F32).at[:GLA_GATE_RANK].set(gla_gate_w2[l])
        proj = _in_proj(x2, mod[l], norm1_w[l], w_in_b[l], seq)
        mixed = (
            _gla(proj, w2_pad, gla_gate_b2[l], gla_norm_w[l], batch, seq),
            _ret(proj, ret_cos, ret_sin, batch, seq),
            _swa(proj, swa_tabs, swa_sinks[l], batch, seq),
            _hgrn(proj, hgrn_lb, hgrn_norm_w[l], l, batch, seq),
        )
        x2 = _out_proj(x2, mod[l], mixed, w_out_b[l], seq)
        x2 = _ffn(x2, mod[l], norm2_w[l], w_ffn_in_b[l], w_ffn_down_b[l], final_norm_w, l == depth - 1, seq)
    return x2.reshape(batch, seq, d)
```

```python
import functools
import math

import jax
import jax.numpy as jnp
from jax import lax
from jax.experimental import pallas as pl
from jax.experimental.pallas import tpu as pltpu

F32 = jnp.float32
BF16 = jnp.bfloat16

D_MODEL = 2048
DEPTH = 2
EPS = 1e-6
GROUP_WIDTH = D_MODEL // 4
CHUNK = 64
SUB = 16
N_SUB = CHUNK // SUB
GLA_HEADS = 4
GLA_DV = GROUP_WIDTH // GLA_HEADS
GLA_DK = GLA_DV // 2
GLA_GATE_RANK = 16
GLA_GATE_NORMALIZER = 16.0
RET_HEADS = 4
RET_DK = GROUP_WIDTH // RET_HEADS
RET_ROT_BASE = 10000.0
SWA_Q_HEADS = 8
SWA_KV_HEADS = 2
SWA_HD = GROUP_WIDTH // SWA_Q_HEADS
WINDOW = 128
ROPE_THETA = 500000.0
ROPE_DIM = SWA_HD // 4
HG_HEADS = 4
HG_DK = GROUP_WIDTH // HG_HEADS
D_FF = ((8 * D_MODEL + 3 * 256 - 1) // (3 * 256)) * 256

LANES = 128
VMEM_LIMIT = 48 * 1024 * 1024
VMEM_LIMIT_WIDE = 56 * 1024 * 1024

_SRC = dict(a_q=0, a_k=256, a_v=512, a_g=1024, a_r=1536,
            b_q=1552, b_k=2064, b_v=2576, b_g=3088,
            c_q=3600, c_k=4112, c_v=4240,
            d_q=4368, d_f=4880, d_i=5392, d_g=5904)
_DST_ORDER = (("a_q", 256), ("a_k", 256), ("a_v", 512), ("a_g", 512), ("c_q", 512),
              ("b_q", 512), ("b_k", 512), ("b_v", 512), ("b_g", 512),
              ("d_q", 512), ("d_f", 512), ("d_i", 512), ("d_g", 512),
              ("c_k", 128), ("c_v", 128), ("a_r", GLA_GATE_RANK))
PROJ_WIDTH = 6528
PROJ_TN = PROJ_WIDTH // 3


def _dst_offsets():
    offs, o = {}, 0
    for name, w in _DST_ORDER:
        offs[name] = o
        o += w
    return offs


_DST = _dst_offsets()


def _dot(a, b):
    return jnp.dot(a, b, preferred_element_type=F32)


def _dot_nt(a, b):
    return lax.dot_general(a, b, (((1,), (1,)), ((), ())), preferred_element_type=F32)


def _dot_tn(a, b):
    return lax.dot_general(a, b, (((0,), (0,)), ((), ())), preferred_element_type=F32)


def _split_bf16(x):
    hi = x.astype(BF16)
    lo = (x - hi.astype(F32)).astype(BF16)
    return hi, lo


def _dot3(a, b):
    a_hi, a_lo = _split_bf16(a)
    b_hi, b_lo = _split_bf16(b)
    return _dot(a_hi, b_hi) + (_dot(a_hi, b_lo) + _dot(a_lo, b_hi))


def _sigmoid(x):
    return 1.0 / (1.0 + jnp.exp(-x))


def _silu(x):
    return (0.5 * x) * (1.0 + jnp.tanh(0.5 * x))


def _log_sigmoid(x):
    return jnp.minimum(x, 0.0) - jnp.log1p(jnp.exp(-jnp.abs(x)))


def _rms(x):
    return x * lax.rsqrt(jnp.mean(x * x, axis=-1, keepdims=True) + EPS)


def _ada_kernel(c_ref, w_ref, b_ref, o_ref):
    cond = _silu(c_ref[...])
    o_ref[0] = _dot3(cond, w_ref[0]) + b_ref[0]


def _ada_modulation(c, w_ada, b_ada):
    depth, d, n = w_ada.shape
    rows = 8
    c_pad = jnp.zeros((rows, d), F32).at[:c.shape[0]].set(c)
    tn = 1024
    out = pl.pallas_call(
        _ada_kernel,
        out_shape=jax.ShapeDtypeStruct((depth, rows, n), F32),
        grid=(depth, n // tn),
        in_specs=[pl.BlockSpec((rows, d), lambda l, j: (0, 0)),
                  pl.BlockSpec((1, d, tn), lambda l, j: (l, 0, j)),
                  pl.BlockSpec((1, 1, tn), lambda l, j: (l, 0, j))],
        out_specs=pl.BlockSpec((1, rows, tn), lambda l, j: (l, 0, j)),
        compiler_params=pltpu.CompilerParams(
            dimension_semantics=("arbitrary", "arbitrary"), vmem_limit_bytes=VMEM_LIMIT),
        name="ada_modulation",
    )(c_pad, w_ada, b_ada.reshape(depth, 1, n))
    return out[:, :c.shape[0]]


def _modulated_norm(x, nw, shift, scale):
    return (_rms(x) * nw) * (1.0 + scale) + shift


def _norm_rows_ahead(xn_ref, modn_ref, nw_ref, h_next, step, n_chunks, shift_row):
    rows_per = xn_ref.shape[0] // n_chunks
    r0 = pl.multiple_of(jnp.minimum(step, n_chunks - 1) * rows_per, rows_per)
    rows = pl.ds(r0, rows_per)
    m = modn_ref[0]
    h = _modulated_norm(xn_ref[rows, :], nw_ref[...], m[shift_row:shift_row + 1], m[shift_row + 1:shift_row + 2])
    h_next[rows, :] = h.astype(BF16)


def _in_proj_kernel(x0_ref, xn_ref, mod0_ref, modn_ref, nw_ref, w_ref, o_ref, h_ref):
    i, j = pl.program_id(0), pl.program_id(1)

    @pl.when((i == 0) & (j == 0))
    def _():
        m = mod0_ref[0]
        h_ref[0] = _modulated_norm(x0_ref[...], nw_ref[...], m[0:1], m[1:2]).astype(BF16)

    slot = i % 2
    _norm_rows_ahead(xn_ref, modn_ref, nw_ref, h_ref.at[1 - slot], j, IN_PROJ_NORM_CHUNKS, 0)
    o_ref[...] = _dot(h_ref[slot], w_ref[...])


IN_PROJ_NORM_CHUNKS = 2


def _in_proj(x2, mod_l, nw, w_perm, layer, seq, tm=512):
    m_rows, d = x2.shape
    n = w_perm.shape[2]
    per_b = seq // tm
    n_i = m_rows // tm
    nxt = lambda i: jnp.minimum(i + 1, n_i - 1)
    return pl.pallas_call(
        _in_proj_kernel,
        out_shape=jax.ShapeDtypeStruct((m_rows, n), F32),
        grid=(n_i, n // PROJ_TN),
        in_specs=[pl.BlockSpec((tm, d), lambda i, j: (0, 0)),
                  pl.BlockSpec((tm, d), lambda i, j: (nxt(i), 0)),
                  pl.BlockSpec((1, 6, d), lambda i, j: (0, 0, 0)),
                  pl.BlockSpec((1, 6, d), lambda i, j: (nxt(i) // per_b, 0, 0)),
                  pl.BlockSpec((1, d), lambda i, j: (0, 0)),
                  pl.BlockSpec((None, d, PROJ_TN), lambda i, j: (layer, 0, j))],
        out_specs=pl.BlockSpec((tm, PROJ_TN), lambda i, j: (i, j)),
        scratch_shapes=[pltpu.VMEM((2, tm, d), BF16)],
        compiler_params=pltpu.CompilerParams(
            dimension_semantics=("arbitrary", "arbitrary"), vmem_limit_bytes=VMEM_LIMIT_WIDE),
        name="in_proj",
    )(x2, x2, mod_l, mod_l, nw.reshape(1, d), w_perm)


TILE = 2 * CHUNK
GLA_BLOCK = CHUNK
HG_BLOCK = SUB
CHUNK_SHIFT = CHUNK.bit_length() - 1
SUB_SHIFT = SUB.bit_length() - 1
SUBLANES = 8
SLAB = 64
DIAG_RUN = 4
assert DIAG_RUN == 4


def _ones_where(mask):
    return jnp.where(mask, 1.0, 0.0).astype(BF16)


def _levels(block):
    return [hs for hs in (CHUNK // 2, CHUNK // 4) if hs >= block]


def _gate_constants(block):
    i = lax.broadcasted_iota(jnp.int32, (TILE, TILE), 0)
    m = lax.broadcasted_iota(jnp.int32, (TILE, TILE), 1)
    same_chunk = (i >> CHUNK_SHIFT) == (m >> CHUNK_SHIFT)
    sub_end = i | (SUB - 1)
    chunk_start = i & ~(CHUNK - 1)
    after = same_chunk & (m > i)
    mats = [same_chunk & (m <= i),
            after & (m <= sub_end),
            after]
    for j in range(N_SUB - 1):
        mats.append(same_chunk & (m > chunk_start + (SUB * j + SUB - 1)) & (m <= i))
    cum = jnp.concatenate([_ones_where(x) for x in mats], axis=0)
    place = _ones_where(m == ((TILE - (i >> 3)) & (TILE - 1)))
    diag_ok = ((i >> SUB_SHIFT) == (m >> SUB_SHIFT)) & (m <= i)
    off_ok = same_chunk & ((m >> SUB_SHIFT) < (i >> SUB_SHIFT))
    sub_of_row = (lax.broadcasted_iota(jnp.int32, (TILE, 1), 0) >> SUB_SHIFT) & (N_SUB - 1)
    score_masks = []
    for hs in _levels(block):
        same = (i >> (2 * hs).bit_length() - 1) == (m >> (2 * hs).bit_length() - 1)
        score_masks.append(same & ((i & (2 * hs - 1)) >= hs) & ((m & (2 * hs - 1)) < hs))
    score_masks.append(((i >> block.bit_length() - 1) == (m >> block.bit_length() - 1)) & (m <= i))
    return cum, place, diag_ok, off_ok, sub_of_row, score_masks


def _transpose_tiles(x):
    return jnp.concatenate([x[:, t:t + LANES].T for t in range(0, x.shape[1], LANES)], axis=0)


def _diag_partial_sums(q_t, k_t, g_t):
    dk = q_t.shape[0]
    slab = min(dk, SLAB)
    accs = [None] * SUB
    for s0 in range(0, dk, slab):
        qs = q_t[s0:s0 + slab]
        ks = k_t[s0:s0 + slab]
        g1 = g_t[s0:s0 + slab]
        g2 = g1 * pltpu.roll(g1, 1, axis=1)
        g4 = g2 * pltpu.roll(g2, 2, axis=1)
        g4_back = [g4] + [pltpu.roll(g4, DIAG_RUN * n, axis=1) for n in range(1, SUB // DIAG_RUN - 1)]
        for d0 in range(0, SUB, DIAG_RUN):
            w = ks
            if d0 > 0:
                w = pltpu.roll(ks, d0, axis=1)
                for n in range(d0 // DIAG_RUN):
                    w = w * g4_back[n]
            for d in range(d0, d0 + DIAG_RUN):
                if d > d0:
                    w = pltpu.roll(w, 1, axis=1) * g1
                p = jnp.sum((qs * w).reshape(slab // SUBLANES, SUBLANES, TILE), axis=0)
                accs[d] = p if accs[d] is None else accs[d] + p
    return jnp.concatenate(accs, axis=0)


def _split3_bf16(x):
    hi = x.astype(BF16)
    rest = x - hi.astype(F32)
    mid = rest.astype(BF16)
    lo = (rest - mid.astype(F32)).astype(BF16)
    return hi, mid, lo


def _cum_dot(mat, pieces):
    hi, mid, lo = pieces
    return _dot(mat, hi) + (_dot(mat, mid) + _dot(mat, lo))


def _head_lanes(h, dk):
    tile = (h * dk) // LANES
    if dk >= LANES:
        return tile, None
    lane = lax.broadcasted_iota(jnp.int32, (1, LANES), 1)
    start = (h * dk) % LANES
    return tile, (lane >= start) & (lane < start + dk)


def _gated_tile_robust(q, k, lg, lg_pieces, v, st_ref, consts, n_heads, dk, dv):
    cum, place, diag_ok, off_ok, sub_of_row, _ = consts
    sums = _cum_dot(cum, lg_pieces)
    cb = sums[0:TILE]
    q_state = (q * jnp.exp(cb)).astype(BF16)
    k_end = k * jnp.exp(sums[TILE:2 * TILE])
    k_last = (k * jnp.exp(sums[2 * TILE:3 * TILE])).astype(BF16)
    q_off = [(q * jnp.exp(sums[(3 + j) * TILE:(4 + j) * TILE])).astype(BF16) for j in range(N_SUB - 1)]
    k_off = [jnp.where(sub_of_row == j, k_end, 0.0).astype(BF16) for j in range(N_SUB - 1)]
    q_t = _transpose_tiles(q)
    k_t = _transpose_tiles(k)
    g_t = _transpose_tiles(jnp.exp(lg))
    outs = []
    for h in range(n_heads):
        ks = slice(h * dk, (h + 1) * dk)
        tile = (h * dk) // LANES
        ls = slice((h * dk) % LANES, (h * dk) % LANES + dk)
        vb = v[:, h * dv:(h + 1) * dv].astype(BF16)
        a_off = _dot_nt(q_off[0][:, ks], k_off[0][:, ks])
        for j in range(1, N_SUB - 1):
            a_off += _dot_nt(q_off[j][:, ks], k_off[j][:, ks])
        part = _diag_partial_sums(q_t[ks], k_t[ks], g_t[ks])
        p_hi, p_lo = _split_bf16(part)
        skew = _dot_tn(p_hi, place) + _dot_tn(p_lo, place)
        a_diag = pltpu.roll(skew, 0, axis=1, stride=1, stride_axis=0)
        a = jnp.where(diag_ok, a_diag, jnp.where(off_ok, a_off, 0.0)).astype(BF16)
        o = _dot(a, vb)
        st = st_ref[tile][:, ls]
        inter = []
        for c in range(TILE // CHUNK):
            rs = slice(c * CHUNK, (c + 1) * CHUNK)
            inter.append(_dot_nt(q_state[rs, ks], st.astype(BF16)))
            last = cb[c * CHUNK + CHUNK - 1:(c + 1) * CHUNK, ks]
            st = st * jnp.exp(last) + _dot_tn(vb[rs], k_last[rs, ks])
        st_ref[tile, :, ls] = st
        outs.append(o + jnp.concatenate(inter, axis=0))
    return outs


def _rows_from(cb, row, spans):
    out = None
    for lo, src in spans:
        val = jnp.zeros_like(cb[0:1]) if src is None else cb[src:src + 1]
        out = val if out is None else jnp.where(row >= lo, val, out)
    return out


def _block_prefix(cb, block):
    if block == CHUNK:
        return cb
    row = lax.broadcasted_iota(jnp.int32, (TILE, 1), 0)
    spans = [(b0, None if b0 % CHUNK == 0 else b0 - 1) for b0 in range(0, TILE, block)]
    return cb - _rows_from(cb, row, spans)


def _gated_tile_bounded(q, k, cb, pb, v, st_ref, consts, n_heads, dk, dv, block):
    score_masks = consts[-1]
    row = lax.broadcasted_iota(jnp.int32, (TILE, 1), 0)
    ends = [cb[c * CHUNK + CHUNK - 1:(c + 1) * CHUNK, :] for c in range(TILE // CHUNK)]
    end_of_row = _rows_from(cb, row, [(c * CHUNK, c * CHUNK + CHUNK - 1) for c in range(TILE // CHUNK)])
    q_state = q * jnp.exp(cb)
    k_last = k * jnp.exp(end_of_row - cb)
    q_parts, k_parts = [], []
    for hs in _levels(block):
        ref = _rows_from(cb, row, [(b0, b0 + hs - 1) for b0 in range(0, TILE, 2 * hs)])
        lower = (row & (2 * hs - 1)) < hs
        x = jnp.exp(jnp.where(lower, ref - cb, cb - ref))
        q_parts.append(jnp.where(lower, 0.0, q * x))
        k_parts.append(jnp.where(lower, k * x, 0.0).astype(BF16))
    q_parts.append(q_state if block == CHUNK else q * jnp.exp(pb))
    k_parts.append((k * jnp.exp(-pb)).astype(BF16))
    outs = [None] * n_heads
    heads_per_tile = max(LANES // dk, 1)
    for tile in range(n_heads * dk // LANES):
        ls = slice(tile * LANES, (tile + 1) * LANES)
        heads = range(tile * heads_per_tile, (tile + 1) * heads_per_tile)
        qh, kl, vb, intra = {}, {}, {}, {}
        for h in heads:
            _, mask = _head_lanes(h, dk)
            own = (lambda a: a) if mask is None else (lambda a: jnp.where(mask, a, 0.0))
            qh[h] = own(q_state[:, ls]).astype(BF16)
            kl[h] = own(k_last[:, ls]).astype(BF16)
            vb[h] = v[:, h * dv:(h + 1) * dv].astype(BF16)
            a = 0.0
            for qp, kp, ok in zip(q_parts, k_parts, score_masks):
                a = jnp.where(ok, _dot_nt(own(qp[:, ls]).astype(BF16), kp[:, ls]), a)
            intra[h] = _dot(a.astype(BF16), vb[h])
        st = st_ref[tile]
        inter = {h: [] for h in heads}
        for c in range(TILE // CHUNK):
            rs = slice(c * CHUNK, (c + 1) * CHUNK)
            stb = st.astype(BF16)
            upd = None
            for h in heads:
                inter[h].append(_dot_nt(qh[h][rs], stb))
                u = _dot_tn(vb[h][rs], kl[h][rs])
                upd = u if upd is None else upd + u
            st = st * jnp.exp(ends[c][:, ls]) + upd
        st_ref[tile] = st
        for h in heads:
            outs[h] = intra[h] + jnp.concatenate(inter[h], axis=0)
    return outs


SAFE_DECAY = 64.0


def _gated_tile(q, k, lg, v, st_ref, consts, n_heads, dk, dv, block, finish):
    pieces = _split3_bf16(lg)
    cb = _cum_dot(consts[0][0:TILE], pieces)
    pb = _block_prefix(cb, block)
    bounded = jnp.min(pb) >= -SAFE_DECAY

    @pl.when(bounded)
    def _():
        finish(_gated_tile_bounded(q, k, cb, pb, v, st_ref, consts, n_heads, dk, dv, block))

    @pl.when(jnp.logical_not(bounded))
    def _():
        finish(_gated_tile_robust(q, k, lg, pieces, v, st_ref, consts, n_heads, dk, dv))


def _gla_kernel(q_ref, k_ref, v_ref, g_ref, r_ref, w2_ref, b2_ref, nw_ref, o_ref, st_ref, *, n_tiles):
    @pl.when(pl.program_id(1) == 0)
    def _():
        st_ref[...] = jnp.zeros_like(st_ref)

    consts = _gate_constants(GLA_BLOCK)
    w2 = w2_ref[...]
    b2 = b2_ref[...]
    nw = nw_ref[...]

    def body(c, carry):
        rows = pl.ds(pl.multiple_of(c * TILE, TILE), TILE)
        pre = _dot3(r_ref[rows, :], w2) + b2
        lg = _log_sigmoid(pre) * (1.0 / GLA_GATE_NORMALIZER)
        q = q_ref[rows, :] * (GLA_DK ** -0.5)

        def finish(outs):
            g = g_ref[rows, :]
            for h, o in enumerate(outs):
                vs = slice(h * GLA_DV, (h + 1) * GLA_DV)
                o_ref[rows, vs] = ((_rms(o) * nw) * _silu(g[:, vs])).astype(BF16)

        _gated_tile(q, k_ref[rows, :], lg, v_ref[rows, :], st_ref, consts,
                    GLA_HEADS, GLA_DK, GLA_DV, GLA_BLOCK, finish)
        return carry

    lax.fori_loop(0, n_tiles, body, 0)


def _hgrn_kernel(q_ref, f_ref, i_ref, g_ref, lb_ref, nw_ref, o_ref, st_ref, *, n_tiles, layer):
    @pl.when(pl.program_id(1) == 0)
    def _():
        st_ref[...] = jnp.zeros_like(st_ref)

    consts = _gate_constants(HG_BLOCK)
    lbv = lb_ref[...]
    e = jnp.exp(lbv - jnp.max(lbv, axis=0, keepdims=True))
    soft = e / jnp.sum(e, axis=0, keepdims=True)
    cum = soft[0:1]
    for i in range(1, layer + 1):
        cum = cum + soft[i:i + 1]
    lb = cum - soft[0:1]
    nw = nw_ref[...]

    def body(c, carry):
        rows = pl.ds(pl.multiple_of(c * TILE, TILE), TILE)
        f = lb + (1.0 - lb) * _sigmoid(f_ref[rows, :])
        q = _silu(q_ref[rows, :]) * (HG_DK ** -0.5)

        def finish(outs):
            g = g_ref[rows, :]
            for h, o in enumerate(outs):
                hs = slice(h * HG_DK, (h + 1) * HG_DK)
                o_ref[rows, hs] = ((_rms(o) * nw) * _silu(g[:, hs])).astype(BF16)

        _gated_tile(q, 1.0 - f, jnp.log(f), i_ref[rows, :], st_ref, consts,
                    HG_HEADS, HG_DK, HG_DK, HG_BLOCK, finish)
        return carry

    lax.fori_loop(0, n_tiles, body, 0)


def _col_spec(t, width, name, nt):
    blk = _DST[name] // width
    return pl.BlockSpec((t, width), lambda b, i: (b * nt + i, blk))


def _mixer_out_spec(t, nt):
    return pl.BlockSpec((t, GROUP_WIDTH), lambda b, i: (b * nt + i, 0))


def _mixer_params():
    return pltpu.CompilerParams(dimension_semantics=("arbitrary", "arbitrary"),
                                vmem_limit_bytes=VMEM_LIMIT)


def _gla(proj, w2_pad, b2, nw, batch, seq, t=512):
    nt = seq // t
    full = lambda shape: pl.BlockSpec(shape, lambda b, i: (0, 0))
    return pl.pallas_call(
        functools.partial(_gla_kernel, n_tiles=t // TILE),
        out_shape=jax.ShapeDtypeStruct((batch * seq, GROUP_WIDTH), BF16),
        grid=(batch, nt),
        in_specs=[_col_spec(t, 256, "a_q", nt), _col_spec(t, 256, "a_k", nt),
                  _col_spec(t, 512, "a_v", nt), _col_spec(t, 512, "a_g", nt),
                  _col_spec(t, LANES, "a_r", nt),
                  full((LANES, GLA_HEADS * GLA_DK)), full((1, GLA_HEADS * GLA_DK)), full((1, GLA_DV))],
        out_specs=_mixer_out_spec(t, nt),
        scratch_shapes=[pltpu.VMEM((GLA_HEADS * GLA_DK // LANES, GLA_DV, LANES), F32)],
        compiler_params=_mixer_params(),
        name="gla_mixer",
    )(proj, proj, proj, proj, proj, w2_pad, b2.reshape(1, -1), nw.reshape(1, -1))


def _hgrn(proj, hgrn_lb, nw, layer, batch, seq, t=512):
    nt = seq // t
    full = lambda shape: pl.BlockSpec(shape, lambda b, i: (0, 0))
    return pl.pallas_call(
        functools.partial(_hgrn_kernel, n_tiles=t // TILE, layer=layer),
        out_shape=jax.ShapeDtypeStruct((batch * seq, GROUP_WIDTH), BF16),
        grid=(batch, nt),
        in_specs=[_col_spec(t, 512, "d_q", nt), _col_spec(t, 512, "d_f", nt),
                  _col_spec(t, 512, "d_i", nt), _col_spec(t, 512, "d_g", nt),
                  full(hgrn_lb.shape), full((1, HG_DK))],
        out_specs=_mixer_out_spec(t, nt),
        scratch_shapes=[pltpu.VMEM((HG_HEADS * HG_DK // LANES, HG_DK, LANES), F32)],
        compiler_params=_mixer_params(),
        name="hgrn_mixer",
    )(proj, proj, proj, proj, hgrn_lb, nw.reshape(1, -1))


def _ret_kernel(q_ref, k_ref, v_ref, g_ref, pos_ref, inv_ref, o_ref, st_ref, *, t):
    @pl.when(pl.program_id(1) == 0)
    def _():
        st_ref[...] = jnp.zeros_like(st_ref)

    ang = pos_ref[0] * inv_ref[...]
    lane = lax.broadcasted_iota(jnp.int32, (1, RET_DK), 1)
    cos = jnp.cos(ang)
    sin = jnp.where(lane < RET_DK // 2, -jnp.sin(ang), jnp.sin(ang))
    q = q_ref[...]
    k = k_ref[...]
    v = v_ref[...]
    g = g_ref[...]
    rowm = lax.broadcasted_iota(jnp.int32, (t, t), 0)
    colm = lax.broadcasted_iota(jnp.int32, (t, t), 1)
    rel = (rowm - colm).astype(F32)
    pos = lax.broadcasted_iota(jnp.int32, (t, RET_DK), 0).astype(F32)
    half = RET_DK // 2
    for h in range(RET_HEADS):
        lg = math.log1p(-(2.0 ** (-5.0 - h)))
        hs = slice(h * RET_DK, (h + 1) * RET_DK)
        qh = q[:, hs]
        kh = k[:, hs]
        qr = qh * cos + pltpu.roll(qh, half, axis=1) * sin
        kr = (kh * cos + pltpu.roll(kh, half, axis=1) * sin) * (RET_DK ** -0.5)
        vb = v[:, hs].astype(BF16)
        dmask = jnp.where(rel >= 0, jnp.exp(jnp.minimum(lg * rel, 0.0)), 0.0)
        scores = _dot_nt(qr.astype(BF16), kr.astype(BF16)) * dmask
        st = st_ref[h]
        xi = jnp.exp(lg * (pos + 1.0))
        o = _dot(scores.astype(BF16), vb) + _dot((qr * xi).astype(BF16), st.astype(BF16))
        zeta = jnp.exp(lg * (t - 1.0 - pos))
        st_ref[h] = st * math.exp(lg * t) + _dot_tn((kr * zeta).astype(BF16), vb)
        o_ref[:, hs] = (_rms(o) * _silu(g[:, hs])).astype(BF16)


def _ret(proj, pos, inv, batch, seq, t=256):
    nt = seq // t
    pos_spec = pl.BlockSpec((1, t, 1), lambda b, i: (b, i, 0))
    inv_spec = pl.BlockSpec((1, RET_DK), lambda b, i: (0, 0))
    return pl.pallas_call(
        functools.partial(_ret_kernel, t=t),
        out_shape=jax.ShapeDtypeStruct((batch * seq, GROUP_WIDTH), BF16),
        grid=(batch, nt),
        in_specs=[_col_spec(t, 512, "b_q", nt), _col_spec(t, 512, "b_k", nt),
                  _col_spec(t, 512, "b_v", nt), _col_spec(t, 512, "b_g", nt), pos_spec, inv_spec],
        out_specs=_mixer_out_spec(t, nt),
        scratch_shapes=[pltpu.VMEM((RET_HEADS, RET_DK, RET_DK), F32)],
        compiler_params=_mixer_params(),
        name="ret_mixer",
    )(proj, proj, proj, proj, pos, inv)


def _rope_partial(x, c, s_lo, s_hi):
    n = x.shape[-1]
    half = ROPE_DIM // 2
    return x * c + pltpu.roll(x, n - half, axis=1) * s_lo + pltpu.roll(x, half, axis=1) * s_hi


def _swa_kernel(q_ref, kc_ref, vc_ref, pos_ref, inv_ref, sink_ref, o_ref, kprev_ref, vprev_ref):
    w = WINDOW
    has_prev = pl.program_id(1) > 0

    @pl.when(pl.program_id(1) == 0)
    def _():
        kprev_ref[...] = jnp.zeros_like(kprev_ref)
        vprev_ref[...] = jnp.zeros_like(vprev_ref)

    lane = lax.broadcasted_iota(jnp.int32, (1, LANES), 1)
    lo_half = lane < SWA_HD
    ang = pos_ref[0] * inv_ref[...]
    in_head = lane & (SWA_HD - 1)
    c = jnp.cos(ang)
    s_lo = jnp.where(in_head < ROPE_DIM // 2, -jnp.sin(ang), 0.0)
    s_hi = jnp.where((in_head >= ROPE_DIM // 2) & (in_head < ROPE_DIM), jnp.sin(ang), 0.0)
    kc = _rope_partial(kc_ref[...], c, s_lo, s_hi)
    vc = vc_ref[...]
    group = SWA_Q_HEADS // SWA_KV_HEADS
    rowm = lax.broadcasted_iota(jnp.int32, (group * w, w), 0) & (w - 1)
    colm = lax.broadcasted_iota(jnp.int32, (group * w, w), 1)
    cur_ok = colm <= rowm
    prev_ok = (colm > rowm) & has_prev
    sinks = sink_ref[...]

    def both_halves(a, kv):
        swapped = pltpu.roll(a, SWA_HD, axis=1)
        return jnp.where(lo_half, a, swapped) if kv == 0 else jnp.where(lo_half, swapped, a)

    for kv in range(SWA_KV_HEADS):
        k2c = both_halves(kc, kv).astype(BF16)
        k2p = kprev_ref[kv]
        v2c = both_halves(vc, kv)
        v_half = [(jnp.where(lo_half, v2c, jnp.where(lane == SWA_HD, 1.0, 0.0)).astype(BF16), vprev_ref[kv, 0]),
                  (jnp.where(lo_half, jnp.where(lane == 0, 1.0, 0.0), v2c).astype(BF16), vprev_ref[kv, 1])]
        ones_lane = (SWA_HD, 0)
        kprev_ref[kv] = k2c
        vprev_ref[kv, 0] = v_half[0][0]
        vprev_ref[kv, 1] = v_half[1][0]
        q_rows, sink_rows = [], []
        for t in range(kv * group // 2, (kv + 1) * group // 2):
            ls = slice(t * LANES, (t + 1) * LANES)
            qt = _rope_partial(q_ref[:, ls], c, s_lo, s_hi) * (SWA_HD ** -0.5)
            q_rows += [jnp.where(lo_half, qt, 0.0), jnp.where(lo_half, 0.0, qt)]
            sink_rows += [jnp.broadcast_to(sinks[:, 2 * t:2 * t + 1], (w, 1)),
                          jnp.broadcast_to(sinks[:, 2 * t + 1:2 * t + 2], (w, 1))]
        q4 = jnp.concatenate(q_rows, axis=0).astype(BF16)
        sink = jnp.concatenate(sink_rows, axis=0)
        s_c = jnp.where(cur_ok, _dot_nt(q4, k2c), -jnp.inf)
        s_p = jnp.where(prev_ok, _dot_nt(q4, k2p), -jnp.inf)
        m = jnp.maximum(jnp.max(jnp.maximum(s_c, s_p), axis=-1, keepdims=True), sink)
        e_c = jnp.exp(s_c - m).astype(BF16)
        e_p = jnp.exp(s_p - m).astype(BF16)
        e_sink = jnp.exp(sink - m)
        for n, t in enumerate(range(kv * group // 2, (kv + 1) * group // 2)):
            halves = []
            for half in range(2):
                rs = slice((2 * n + half) * w, (2 * n + half + 1) * w)
                vh_c, vh_p = v_half[half]
                o = _dot(e_c[rs], vh_c) + _dot(e_p[rs], vh_p)
                denom = o[:, ones_lane[half]:ones_lane[half] + 1] + e_sink[rs]
                halves.append(o * (1.0 / denom))
            o_ref[:, t * LANES:(t + 1) * LANES] = jnp.where(lo_half, halves[0], halves[1]).astype(BF16)


def _swa(proj, pos, inv, sinks, batch, seq):
    w = WINDOW
    nt = seq // w
    kv_w = SWA_KV_HEADS * SWA_HD
    cur = lambda name: pl.BlockSpec((w, kv_w), lambda b, i: (b * nt + i, _DST[name] // kv_w))
    pos_spec = pl.BlockSpec((1, w, 1), lambda b, i: (b, i, 0))
    inv_spec = pl.BlockSpec((1, LANES), lambda b, i: (0, 0))
    return pl.pallas_call(
        _swa_kernel,
        out_shape=jax.ShapeDtypeStruct((batch * seq, GROUP_WIDTH), BF16),
        grid=(batch, nt),
        in_specs=[_col_spec(w, 512, "c_q", nt), cur("c_k"), cur("c_v"), pos_spec, inv_spec,
                  pl.BlockSpec((1, SWA_Q_HEADS), lambda b, i: (0, 0))],
        out_specs=_mixer_out_spec(w, nt),
        scratch_shapes=[pltpu.VMEM((SWA_KV_HEADS, w, LANES), BF16),
                        pltpu.VMEM((SWA_KV_HEADS, 2, w, LANES), BF16)],
        compiler_params=_mixer_params(),
        name="swa_mixer",
    )(proj, proj, proj, pos, inv, sinks.reshape(1, -1))


def _out_proj_kernel(x_ref, mod_ref, a_ref, b_ref, c_ref, d_ref, w_ref, o_ref):
    gw = GROUP_WIDTH
    acc = _dot(a_ref[...], w_ref[0:gw, :])
    acc += _dot(b_ref[...], w_ref[gw:2 * gw, :])
    acc += _dot(c_ref[...], w_ref[2 * gw:3 * gw, :])
    acc += _dot(d_ref[...], w_ref[3 * gw:4 * gw, :])
    gate = mod_ref[0][2:3]
    o_ref[...] = x_ref[...] + gate * acc


def _out_proj(x2, mod_l, mixed, w_out, layer, seq, tm=512):
    m_rows, d = x2.shape
    per_b = seq // tm
    grp = pl.BlockSpec((tm, GROUP_WIDTH), lambda i: (i, 0))
    return pl.pallas_call(
        _out_proj_kernel,
        out_shape=jax.ShapeDtypeStruct((m_rows, d), F32),
        grid=(m_rows // tm,),
        in_specs=[pl.BlockSpec((tm, d), lambda i: (i, 0)),
                  pl.BlockSpec((1, 6, d), lambda i: (i // per_b, 0, 0)),
                  grp, grp, grp, grp,
                  pl.BlockSpec((None, d, d), lambda i: (layer, 0, 0))],
        out_specs=pl.BlockSpec((tm, d), lambda i: (i, 0)),
        compiler_params=pltpu.CompilerParams(
            dimension_semantics=("arbitrary",), vmem_limit_bytes=VMEM_LIMIT),
        name="out_proj",
    )(x2, mod_l, *mixed, w_out)


def _ffn_kernel(x_ref, xn_ref, mod_ref, modn_ref, nw_ref, wg_ref, wu_ref, wd_ref, fw_ref, o_ref,
                h_ref, acc_ref, *, final):
    i, f = pl.program_id(0), pl.program_id(1)

    @pl.when((i == 0) & (f == 0))
    def _():
        m = mod_ref[0]
        h_ref[0] = _modulated_norm(x_ref[...], nw_ref[...], m[3:4], m[4:5]).astype(BF16)

    @pl.when(f == 0)
    def _():
        acc_ref[...] = jnp.zeros_like(acc_ref)

    slot = i % 2
    _norm_rows_ahead(xn_ref, modn_ref, nw_ref, h_ref.at[1 - slot], f, FFN_NORM_CHUNKS, 3)
    h = h_ref[slot]
    g = _dot(h, wg_ref[...])
    u = _dot(h, wu_ref[...])
    act = (_silu(g) * u).astype(BF16)
    acc_ref[...] += _dot(act, wd_ref[...])

    @pl.when(f == pl.num_programs(1) - 1)
    def _():
        y = x_ref[...] + mod_ref[0][5:6] * acc_ref[...]
        o_ref[...] = _rms(y) * fw_ref[...] if final else y


FFN_NORM_CHUNKS = 4


def _ffn(x2, mod_l, nw, w_in, w_down, layer, final_w, final, seq, tm=512, tf=512):
    m_rows, d = x2.shape
    nf = D_FF // tf
    per_b = seq // tm
    n_i = m_rows // tm
    nxt = lambda i: jnp.minimum(i + 1, n_i - 1)
    return pl.pallas_call(
        functools.partial(_ffn_kernel, final=final),
        out_shape=jax.ShapeDtypeStruct((m_rows, d), F32),
        grid=(n_i, nf),
        in_specs=[pl.BlockSpec((tm, d), lambda i, f: (i, 0)),
                  pl.BlockSpec((tm, d), lambda i, f: (nxt(i), 0)),
                  pl.BlockSpec((1, 6, d), lambda i, f: (i // per_b, 0, 0)),
                  pl.BlockSpec((1, 6, d), lambda i, f: (nxt(i) // per_b, 0, 0)),
                  pl.BlockSpec((1, d), lambda i, f: (0, 0)),
                  pl.BlockSpec((None, d, tf), lambda i, f: (layer, 0, f)),
                  pl.BlockSpec((None, d, tf), lambda i, f: (layer, 0, nf + f)),
                  pl.BlockSpec((None, tf, d), lambda i, f: (layer, f, 0)),
                  pl.BlockSpec((1, d), lambda i, f: (0, 0))],
        out_specs=pl.BlockSpec((tm, d), lambda i, f: (i, 0)),
        scratch_shapes=[pltpu.VMEM((2, tm, d), BF16), pltpu.VMEM((tm, d), F32)],
        compiler_params=pltpu.CompilerParams(
            dimension_semantics=("arbitrary", "arbitrary"), vmem_limit_bytes=VMEM_LIMIT_WIDE),
        name="ffn",
    )(x2, x2, mod_l, mod_l, nw.reshape(1, d), w_in, w_in, w_down, final_w.reshape(1, d))


CAST_ROWS = 256
CAST_COLS = 2816


def _cast_kernel(x_ref, o_ref):
    o_ref[...] = x_ref[...].astype(BF16)


def _to_bf16(w):
    depth, k, n = w.shape
    tn = n if n <= CAST_COLS else CAST_COLS
    blk = pl.BlockSpec((1, CAST_ROWS, tn), lambda l, i, j: (l, i, j))
    return pl.pallas_call(
        _cast_kernel,
        out_shape=jax.ShapeDtypeStruct(w.shape, BF16),
        grid=(depth, k // CAST_ROWS, n // tn),
        in_specs=[blk], out_specs=blk,
        compiler_params=pltpu.CompilerParams(
            dimension_semantics=("arbitrary",) * 3, vmem_limit_bytes=VMEM_LIMIT),
        name="cast_weights",
    )(w)


def _permute_kernel(x_ref, o_ref):
    off = 0
    for name, width in _DST_ORDER:
        o_ref[0, :, off:off + width] = x_ref[0, :, _SRC[name]:_SRC[name] + width].astype(BF16)
        off += width
    o_ref[0, :, off:] = jnp.zeros((o_ref.shape[1], PROJ_WIDTH - off), BF16)


def _permute_w_in(w):
    depth, k, n = w.shape
    return pl.pallas_call(
        _permute_kernel,
        out_shape=jax.ShapeDtypeStruct((depth, k, PROJ_WIDTH), BF16),
        grid=(depth, k // CAST_ROWS),
        in_specs=[pl.BlockSpec((1, CAST_ROWS, n), lambda l, i: (l, i, 0))],
        out_specs=pl.BlockSpec((1, CAST_ROWS, PROJ_WIDTH), lambda l, i: (l, i, 0)),
        compiler_params=pltpu.CompilerParams(
            dimension_semantics=("arbitrary", "arbitrary"), vmem_limit_bytes=VMEM_LIMIT),
        name="permute_w_in",
    )(w)


def _ret_inv_freq():
    inv = 1.0 / jnp.power(RET_ROT_BASE, jnp.linspace(0.0, 1.0, RET_DK // 2, dtype=F32))
    return jnp.concatenate([inv, inv])[None, :]


def _swa_inv_freq():
    half = ROPE_DIM // 2
    inv = 1.0 / jnp.power(ROPE_THETA, jnp.arange(half, dtype=F32) / half)
    per_head = jnp.concatenate([inv, inv, jnp.zeros((SWA_HD - ROPE_DIM,), F32)])
    return jnp.concatenate([per_head] * (LANES // SWA_HD))[None, :]


def kernel(x, c, positions, w_ada, b_ada, norm1_w, w_in, gla_gate_w2, gla_gate_b2, gla_norm_w, swa_sinks, hgrn_lb, hgrn_norm_w, w_out, norm2_w, w_ffn_in, w_ffn_down, final_norm_w):
    batch, seq, d = x.shape
    depth = w_ada.shape[0]
    mod = _ada_modulation(c, w_ada, b_ada).reshape(depth, batch, 6, d)
    pos = positions.astype(F32)[:, :, None]
    ret_inv, swa_inv = _ret_inv_freq(), _swa_inv_freq()
    x2 = x.reshape(batch * seq, d)
    w_in_b, w_out_b = _permute_w_in(w_in), _to_bf16(w_out)
    w_ffn_in_b, w_ffn_down_b = _to_bf16(w_ffn_in), _to_bf16(w_ffn_down)
    for l in range(depth):
        w2_pad = jnp.zeros((LANES, GLA_HEADS * GLA_DK), F32).at[:GLA_GATE_RANK].set(gla_gate_w2[l])
        proj = _in_proj(x2, mod[l], norm1_w[l], w_in_b, l, seq)
        mixed = (
            _gla(proj, w2_pad, gla_gate_b2[l], gla_norm_w[l], batch, seq),
            _ret(proj, pos, ret_inv, batch, seq),
            _swa(proj, pos, swa_inv, swa_sinks[l], batch, seq),
            _hgrn(proj, hgrn_lb, hgrn_norm_w[l], l, batch, seq),
        )
        x2 = _out_proj(x2, mod[l], mixed, w_out_b, l, seq)
        x2 = _ffn(x2, mod[l], norm2_w[l], w_ffn_in_b, w_ffn_down_b, l, final_norm_w, l == depth - 1, seq)
    return x2.reshape(batch, seq, d)
```

```python
import functools
import math

import jax
import jax.numpy as jnp
from jax import lax
from jax.experimental import pallas as pl
from jax.experimental.pallas import tpu as pltpu

F32 = jnp.float32
BF16 = jnp.bfloat16

D_MODEL = 2048
DEPTH = 2
EPS = 1e-6
GROUP_WIDTH = D_MODEL // 4
CHUNK = 64
SUB = 16
N_SUB = CHUNK // SUB
GLA_HEADS = 4
GLA_DV = GROUP_WIDTH // GLA_HEADS
GLA_DK = GLA_DV // 2
GLA_GATE_RANK = 16
GLA_GATE_NORMALIZER = 16.0
RET_HEADS = 4
RET_DK = GROUP_WIDTH // RET_HEADS
RET_ROT_BASE = 10000.0
SWA_Q_HEADS = 8
SWA_KV_HEADS = 2
SWA_HD = GROUP_WIDTH // SWA_Q_HEADS
WINDOW = 128
ROPE_THETA = 500000.0
ROPE_DIM = SWA_HD // 4
HG_HEADS = 4
HG_DK = GROUP_WIDTH // HG_HEADS
D_FF = ((8 * D_MODEL + 3 * 256 - 1) // (3 * 256)) * 256

LANES = 128
VMEM_LIMIT = 48 * 1024 * 1024

_SRC = dict(a_q=0, a_k=256, a_v=512, a_g=1024, a_r=1536,
            b_q=1552, b_k=2064, b_v=2576, b_g=3088,
            c_q=3600, c_k=4112, c_v=4240,
            d_q=4368, d_f=4880, d_i=5392, d_g=5904)
_DST_ORDER = (("a_q", 256), ("a_k", 256), ("a_v", 512), ("a_g", 512), ("c_q", 512),
              ("b_q", 512), ("b_k", 512), ("b_v", 512), ("b_g", 512),
              ("d_q", 512), ("d_f", 512), ("d_i", 512), ("d_g", 512),
              ("c_k", 128), ("c_v", 128), ("a_r", GLA_GATE_RANK))
PROJ_WIDTH = 6528
PROJ_TN = PROJ_WIDTH // 3


def _dst_offsets():
    offs, o = {}, 0
    for name, w in _DST_ORDER:
        offs[name] = o
        o += w
    return offs


_DST = _dst_offsets()


def _dot(a, b):
    return jnp.dot(a, b, preferred_element_type=F32)


def _dot_nt(a, b):
    return lax.dot_general(a, b, (((1,), (1,)), ((), ())), preferred_element_type=F32)


def _dot_tn(a, b):
    return lax.dot_general(a, b, (((0,), (0,)), ((), ())), preferred_element_type=F32)


def _split_bf16(x):
    hi = x.astype(BF16)
    lo = (x - hi.astype(F32)).astype(BF16)
    return hi, lo


def _dot3(a, b):
    a_hi, a_lo = _split_bf16(a)
    b_hi, b_lo = _split_bf16(b)
    return _dot(a_hi, b_hi) + (_dot(a_hi, b_lo) + _dot(a_lo, b_hi))


def _sigmoid(x):
    return 1.0 / (1.0 + jnp.exp(-x))


def _silu(x):
    return (0.5 * x) * (1.0 + jnp.tanh(0.5 * x))


def _log_sigmoid(x):
    return jnp.minimum(x, 0.0) - jnp.log1p(jnp.exp(-jnp.abs(x)))


def _rms(x):
    return x * lax.rsqrt(jnp.mean(x * x, axis=-1, keepdims=True) + EPS)


def _ada_kernel(c_ref, w_ref, b_ref, o_ref):
    cond = _silu(c_ref[...])
    o_ref[0] = _dot3(cond, w_ref[0]) + b_ref[0]


def _ada_modulation(c, w_ada, b_ada):
    depth, d, n = w_ada.shape
    rows = 8
    c_pad = jnp.zeros((rows, d), F32).at[:c.shape[0]].set(c)
    tn = 1024
    out = pl.pallas_call(
        _ada_kernel,
        out_shape=jax.ShapeDtypeStruct((depth, rows, n), F32),
        grid=(depth, n // tn),
        in_specs=[pl.BlockSpec((rows, d), lambda l, j: (0, 0)),
                  pl.BlockSpec((1, d, tn), lambda l, j: (l, 0, j)),
                  pl.BlockSpec((1, 1, tn), lambda l, j: (l, 0, j))],
        out_specs=pl.BlockSpec((1, rows, tn), lambda l, j: (l, 0, j)),
        compiler_params=pltpu.CompilerParams(
            dimension_semantics=("arbitrary", "arbitrary"), vmem_limit_bytes=VMEM_LIMIT),
        name="ada_modulation",
    )(c_pad, w_ada, b_ada.reshape(depth, 1, n))
    return out[:, :c.shape[0]]


def _modulated_norm(x, nw, shift, scale):
    return (_rms(x) * nw) * (1.0 + scale) + shift


def _in_proj_kernel(x_ref, mod_ref, nw_ref, w_ref, o_ref, h_ref):
    @pl.when(pl.program_id(1) == 0)
    def _():
        m = mod_ref[0]
        h = _modulated_norm(x_ref[...], nw_ref[...], m[0:1], m[1:2])
        h_ref[...] = h.astype(BF16)

    o_ref[...] = _dot(h_ref[...], w_ref[...])


def _in_proj(x2, mod_l, nw, w_perm, layer, seq, tm=512):
    m_rows, d = x2.shape
    n = w_perm.shape[2]
    per_b = seq // tm
    return pl.pallas_call(
        _in_proj_kernel,
        out_shape=jax.ShapeDtypeStruct((m_rows, n), F32),
        grid=(m_rows // tm, n // PROJ_TN),
        in_specs=[pl.BlockSpec((tm, d), lambda i, j: (i, 0)),
                  pl.BlockSpec((1, 6, d), lambda i, j: (i // per_b, 0, 0)),
                  pl.BlockSpec((1, d), lambda i, j: (0, 0)),
                  pl.BlockSpec((None, d, PROJ_TN), lambda i, j: (layer, 0, j))],
        out_specs=pl.BlockSpec((tm, PROJ_TN), lambda i, j: (i, j)),
        scratch_shapes=[pltpu.VMEM((tm, d), BF16)],
        compiler_params=pltpu.CompilerParams(
            dimension_semantics=("arbitrary", "arbitrary"), vmem_limit_bytes=VMEM_LIMIT),
        name="in_proj",
    )(x2, mod_l, nw.reshape(1, d), w_perm)


TILE = 2 * CHUNK
GLA_BLOCK = CHUNK
HG_BLOCK = CHUNK // 2
CHUNK_SHIFT = CHUNK.bit_length() - 1
SUB_SHIFT = SUB.bit_length() - 1
SUBLANES = 8
SLAB = 64
DIAG_RUN = 4
assert DIAG_RUN == 4


def _ones_where(mask):
    return jnp.where(mask, 1.0, 0.0).astype(BF16)


def _levels(block):
    return [hs for hs in (CHUNK // 2, CHUNK // 4) if hs >= block]


def _gate_constants(block):
    i = lax.broadcasted_iota(jnp.int32, (TILE, TILE), 0)
    m = lax.broadcasted_iota(jnp.int32, (TILE, TILE), 1)
    same_chunk = (i >> CHUNK_SHIFT) == (m >> CHUNK_SHIFT)
    sub_end = i | (SUB - 1)
    chunk_start = i & ~(CHUNK - 1)
    after = same_chunk & (m > i)
    mats = [same_chunk & (m <= i),
            after & (m <= sub_end),
            after]
    for j in range(N_SUB - 1):
        mats.append(same_chunk & (m > chunk_start + (SUB * j + SUB - 1)) & (m <= i))
    cum = jnp.concatenate([_ones_where(x) for x in mats], axis=0)
    place = _ones_where(m == ((TILE - (i >> 3)) & (TILE - 1)))
    diag_ok = ((i >> SUB_SHIFT) == (m >> SUB_SHIFT)) & (m <= i)
    off_ok = same_chunk & ((m >> SUB_SHIFT) < (i >> SUB_SHIFT))
    sub_of_row = (lax.broadcasted_iota(jnp.int32, (TILE, 1), 0) >> SUB_SHIFT) & (N_SUB - 1)
    score_masks = []
    for hs in _levels(block):
        same = (i >> (2 * hs).bit_length() - 1) == (m >> (2 * hs).bit_length() - 1)
        score_masks.append(same & ((i & (2 * hs - 1)) >= hs) & ((m & (2 * hs - 1)) < hs))
    score_masks.append(((i >> block.bit_length() - 1) == (m >> block.bit_length() - 1)) & (m <= i))
    return cum, place, diag_ok, off_ok, sub_of_row, score_masks


def _transpose_tiles(x):
    return jnp.concatenate([x[:, t:t + LANES].T for t in range(0, x.shape[1], LANES)], axis=0)


def _diag_partial_sums(q_t, k_t, g_t):
    dk = q_t.shape[0]
    slab = min(dk, SLAB)
    accs = [None] * SUB
    for s0 in range(0, dk, slab):
        qs = q_t[s0:s0 + slab]
        ks = k_t[s0:s0 + slab]
        g1 = g_t[s0:s0 + slab]
        g2 = g1 * pltpu.roll(g1, 1, axis=1)
        g4 = g2 * pltpu.roll(g2, 2, axis=1)
        g4_back = [g4] + [pltpu.roll(g4, DIAG_RUN * n, axis=1) for n in range(1, SUB // DIAG_RUN - 1)]
        for d0 in range(0, SUB, DIAG_RUN):
            w = ks
            if d0 > 0:
                w = pltpu.roll(ks, d0, axis=1)
                for n in range(d0 // DIAG_RUN):
                    w = w * g4_back[n]
            for d in range(d0, d0 + DIAG_RUN):
                if d > d0:
                    w = pltpu.roll(w, 1, axis=1) * g1
                p = jnp.sum((qs * w).reshape(slab // SUBLANES, SUBLANES, TILE), axis=0)
                accs[d] = p if accs[d] is None else accs[d] + p
    return jnp.concatenate(accs, axis=0)


def _split3_bf16(x):
    hi = x.astype(BF16)
    rest = x - hi.astype(F32)
    mid = rest.astype(BF16)
    lo = (rest - mid.astype(F32)).astype(BF16)
    return hi, mid, lo


def _cum_dot(mat, pieces):
    hi, mid, lo = pieces
    return _dot(mat, hi) + (_dot(mat, mid) + _dot(mat, lo))


def _head_lanes(h, dk):
    tile = (h * dk) // LANES
    if dk >= LANES:
        return tile, None
    lane = lax.broadcasted_iota(jnp.int32, (1, LANES), 1)
    start = (h * dk) % LANES
    return tile, (lane >= start) & (lane < start + dk)


def _gated_tile_robust(q, k, lg, lg_pieces, v, st_ref, consts, n_heads, dk, dv):
    cum, place, diag_ok, off_ok, sub_of_row, _ = consts
    sums = _cum_dot(cum, lg_pieces)
    cb = sums[0:TILE]
    q_state = (q * jnp.exp(cb)).astype(BF16)
    k_end = k * jnp.exp(sums[TILE:2 * TILE])
    k_last = (k * jnp.exp(sums[2 * TILE:3 * TILE])).astype(BF16)
    q_off = [(q * jnp.exp(sums[(3 + j) * TILE:(4 + j) * TILE])).astype(BF16) for j in range(N_SUB - 1)]
    k_off = [jnp.where(sub_of_row == j, k_end, 0.0).astype(BF16) for j in range(N_SUB - 1)]
    q_t = _transpose_tiles(q)
    k_t = _transpose_tiles(k)
    g_t = _transpose_tiles(jnp.exp(lg))
    outs = []
    for h in range(n_heads):
        ks = slice(h * dk, (h + 1) * dk)
        tile = (h * dk) // LANES
        ls = slice((h * dk) % LANES, (h * dk) % LANES + dk)
        vb = v[:, h * dv:(h + 1) * dv].astype(BF16)
        a_off = _dot_nt(q_off[0][:, ks], k_off[0][:, ks])
        for j in range(1, N_SUB - 1):
            a_off += _dot_nt(q_off[j][:, ks], k_off[j][:, ks])
        part = _diag_partial_sums(q_t[ks], k_t[ks], g_t[ks])
        p_hi, p_lo = _split_bf16(part)
        skew = _dot_tn(p_hi, place) + _dot_tn(p_lo, place)
        a_diag = pltpu.roll(skew, 0, axis=1, stride=1, stride_axis=0)
        a = jnp.where(diag_ok, a_diag, jnp.where(off_ok, a_off, 0.0)).astype(BF16)
        o = _dot(a, vb)
        st = st_ref[tile][:, ls]
        inter = []
        for c in range(TILE // CHUNK):
            rs = slice(c * CHUNK, (c + 1) * CHUNK)
            inter.append(_dot_nt(q_state[rs, ks], st.astype(BF16)))
            last = cb[c * CHUNK + CHUNK - 1:(c + 1) * CHUNK, ks]
            st = st * jnp.exp(last) + _dot_tn(vb[rs], k_last[rs, ks])
        st_ref[tile, :, ls] = st
        outs.append(o + jnp.concatenate(inter, axis=0))
    return outs


def _rows_from(cb, row, spans):
    out = None
    for lo, src in spans:
        val = jnp.zeros_like(cb[0:1]) if src is None else cb[src:src + 1]
        out = val if out is None else jnp.where(row >= lo, val, out)
    return out


def _block_prefix(cb, block):
    if block == CHUNK:
        return cb
    row = lax.broadcasted_iota(jnp.int32, (TILE, 1), 0)
    spans = [(b0, None if b0 % CHUNK == 0 else b0 - 1) for b0 in range(0, TILE, block)]
    return cb - _rows_from(cb, row, spans)


def _gated_tile_bounded(q, k, cb, pb, v, st_ref, consts, n_heads, dk, dv, block):
    score_masks = consts[-1]
    row = lax.broadcasted_iota(jnp.int32, (TILE, 1), 0)
    ends = [cb[c * CHUNK + CHUNK - 1:(c + 1) * CHUNK, :] for c in range(TILE // CHUNK)]
    end_of_row = _rows_from(cb, row, [(c * CHUNK, c * CHUNK + CHUNK - 1) for c in range(TILE // CHUNK)])
    q_state = q * jnp.exp(cb)
    k_last = k * jnp.exp(end_of_row - cb)
    q_parts, k_parts = [], []
    for hs in _levels(block):
        ref = _rows_from(cb, row, [(b0, b0 + hs - 1) for b0 in range(0, TILE, 2 * hs)])
        lower = (row & (2 * hs - 1)) < hs
        x = jnp.exp(jnp.where(lower, ref - cb, cb - ref))
        q_parts.append(jnp.where(lower, 0.0, q * x))
        k_parts.append(jnp.where(lower, k * x, 0.0).astype(BF16))
    q_parts.append(q_state if block == CHUNK else q * jnp.exp(pb))
    k_parts.append((k * jnp.exp(-pb)).astype(BF16))
    outs = [None] * n_heads
    heads_per_tile = max(LANES // dk, 1)
    for tile in range(n_heads * dk // LANES):
        ls = slice(tile * LANES, (tile + 1) * LANES)
        heads = range(tile * heads_per_tile, (tile + 1) * heads_per_tile)
        qh, kl, vb, intra = {}, {}, {}, {}
        for h in heads:
            _, mask = _head_lanes(h, dk)
            own = (lambda a: a) if mask is None else (lambda a: jnp.where(mask, a, 0.0))
            qh[h] = own(q_state[:, ls]).astype(BF16)
            kl[h] = own(k_last[:, ls]).astype(BF16)
            vb[h] = v[:, h * dv:(h + 1) * dv].astype(BF16)
            a = 0.0
            for qp, kp, ok in zip(q_parts, k_parts, score_masks):
                a = jnp.where(ok, _dot_nt(own(qp[:, ls]).astype(BF16), kp[:, ls]), a)
            intra[h] = _dot(a.astype(BF16), vb[h])
        st = st_ref[tile]
        inter = {h: [] for h in heads}
        for c in range(TILE // CHUNK):
            rs = slice(c * CHUNK, (c + 1) * CHUNK)
            stb = st.astype(BF16)
            upd = None
            for h in heads:
                inter[h].append(_dot_nt(qh[h][rs], stb))
                u = _dot_tn(vb[h][rs], kl[h][rs])
                upd = u if upd is None else upd + u
            st = st * jnp.exp(ends[c][:, ls]) + upd
        st_ref[tile] = st
        for h in heads:
            outs[h] = intra[h] + jnp.concatenate(inter[h], axis=0)
    return outs


SAFE_DECAY = 64.0


def _gated_tile(q, k, lg, v, st_ref, consts, n_heads, dk, dv, block, finish):
    pieces = _split3_bf16(lg)
    cb = _cum_dot(consts[0][0:TILE], pieces)
    pb = _block_prefix(cb, block)
    bounded = jnp.min(pb) >= -SAFE_DECAY

    @pl.when(bounded)
    def _():
        finish(_gated_tile_bounded(q, k, cb, pb, v, st_ref, consts, n_heads, dk, dv, block))

    @pl.when(jnp.logical_not(bounded))
    def _():
        finish(_gated_tile_robust(q, k, lg, pieces, v, st_ref, consts, n_heads, dk, dv))


def _gla_kernel(q_ref, k_ref, v_ref, g_ref, r_ref, w2_ref, b2_ref, nw_ref, o_ref, st_ref, *, n_tiles):
    @pl.when(pl.program_id(1) == 0)
    def _():
        st_ref[...] = jnp.zeros_like(st_ref)

    consts = _gate_constants(GLA_BLOCK)
    w2 = w2_ref[...]
    b2 = b2_ref[...]
    nw = nw_ref[...]

    def body(c, carry):
        rows = pl.ds(pl.multiple_of(c * TILE, TILE), TILE)
        pre = _dot3(r_ref[rows, :], w2) + b2
        lg = _log_sigmoid(pre) * (1.0 / GLA_GATE_NORMALIZER)
        q = q_ref[rows, :] * (GLA_DK ** -0.5)

        def finish(outs):
            g = g_ref[rows, :]
            for h, o in enumerate(outs):
                vs = slice(h * GLA_DV, (h + 1) * GLA_DV)
                o_ref[rows, vs] = ((_rms(o) * nw) * _silu(g[:, vs])).astype(BF16)

        _gated_tile(q, k_ref[rows, :], lg, v_ref[rows, :], st_ref, consts,
                    GLA_HEADS, GLA_DK, GLA_DV, GLA_BLOCK, finish)
        return carry

    lax.fori_loop(0, n_tiles, body, 0)


def _hgrn_kernel(q_ref, f_ref, i_ref, g_ref, lb_ref, nw_ref, o_ref, st_ref, *, n_tiles, layer):
    @pl.when(pl.program_id(1) == 0)
    def _():
        st_ref[...] = jnp.zeros_like(st_ref)

    consts = _gate_constants(HG_BLOCK)
    lbv = lb_ref[...]
    e = jnp.exp(lbv - jnp.max(lbv, axis=0, keepdims=True))
    soft = e / jnp.sum(e, axis=0, keepdims=True)
    cum = soft[0:1]
    for i in range(1, layer + 1):
        cum = cum + soft[i:i + 1]
    lb = cum - soft[0:1]
    nw = nw_ref[...]

    def body(c, carry):
        rows = pl.ds(pl.multiple_of(c * TILE, TILE), TILE)
        f = lb + (1.0 - lb) * _sigmoid(f_ref[rows, :])
        q = _silu(q_ref[rows, :]) * (HG_DK ** -0.5)

        def finish(outs):
            g = g_ref[rows, :]
            for h, o in enumerate(outs):
                hs = slice(h * HG_DK, (h + 1) * HG_DK)
                o_ref[rows, hs] = ((_rms(o) * nw) * _silu(g[:, hs])).astype(BF16)

        _gated_tile(q, 1.0 - f, jnp.log(f), i_ref[rows, :], st_ref, consts,
                    HG_HEADS, HG_DK, HG_DK, HG_BLOCK, finish)
        return carry

    lax.fori_loop(0, n_tiles, body, 0)


def _col_spec(t, width, name, nt):
    blk = _DST[name] // width
    return pl.BlockSpec((t, width), lambda b, i: (b * nt + i, blk))


def _mixer_out_spec(t, nt):
    return pl.BlockSpec((t, GROUP_WIDTH), lambda b, i: (b * nt + i, 0))


def _mixer_params():
    return pltpu.CompilerParams(dimension_semantics=("arbitrary", "arbitrary"),
                                vmem_limit_bytes=VMEM_LIMIT)


def _gla(proj, w2_pad, b2, nw, batch, seq, t=512):
    nt = seq // t
    full = lambda shape: pl.BlockSpec(shape, lambda b, i: (0, 0))
    return pl.pallas_call(
        functools.partial(_gla_kernel, n_tiles=t // TILE),
        out_shape=jax.ShapeDtypeStruct((batch * seq, GROUP_WIDTH), BF16),
        grid=(batch, nt),
        in_specs=[_col_spec(t, 256, "a_q", nt), _col_spec(t, 256, "a_k", nt),
                  _col_spec(t, 512, "a_v", nt), _col_spec(t, 512, "a_g", nt),
                  _col_spec(t, LANES, "a_r", nt),
                  full((LANES, GLA_HEADS * GLA_DK)), full((1, GLA_HEADS * GLA_DK)), full((1, GLA_DV))],
        out_specs=_mixer_out_spec(t, nt),
        scratch_shapes=[pltpu.VMEM((GLA_HEADS * GLA_DK // LANES, GLA_DV, LANES), F32)],
        compiler_params=_mixer_params(),
        name="gla_mixer",
    )(proj, proj, proj, proj, proj, w2_pad, b2.reshape(1, -1), nw.reshape(1, -1))


def _hgrn(proj, hgrn_lb, nw, layer, batch, seq, t=512):
    nt = seq // t
    full = lambda shape: pl.BlockSpec(shape, lambda b, i: (0, 0))
    return pl.pallas_call(
        functools.partial(_hgrn_kernel, n_tiles=t // TILE, layer=layer),
        out_shape=jax.ShapeDtypeStruct((batch * seq, GROUP_WIDTH), BF16),
        grid=(batch, nt),
        in_specs=[_col_spec(t, 512, "d_q", nt), _col_spec(t, 512, "d_f", nt),
                  _col_spec(t, 512, "d_i", nt), _col_spec(t, 512, "d_g", nt),
                  full(hgrn_lb.shape), full((1, HG_DK))],
        out_specs=_mixer_out_spec(t, nt),
        scratch_shapes=[pltpu.VMEM((HG_HEADS * HG_DK // LANES, HG_DK, LANES), F32)],
        compiler_params=_mixer_params(),
        name="hgrn_mixer",
    )(proj, proj, proj, proj, hgrn_lb, nw.reshape(1, -1))


def _ret_kernel(q_ref, k_ref, v_ref, g_ref, pos_ref, inv_ref, o_ref, st_ref, *, t):
    @pl.when(pl.program_id(1) == 0)
    def _():
        st_ref[...] = jnp.zeros_like(st_ref)

    ang = pos_ref[0] * inv_ref[...]
    lane = lax.broadcasted_iota(jnp.int32, (1, RET_DK), 1)
    cos = jnp.cos(ang)
    sin = jnp.where(lane < RET_DK // 2, -jnp.sin(ang), jnp.sin(ang))
    q = q_ref[...]
    k = k_ref[...]
    v = v_ref[...]
    g = g_ref[...]
    rowm = lax.broadcasted_iota(jnp.int32, (t, t), 0)
    colm = lax.broadcasted_iota(jnp.int32, (t, t), 1)
    rel = (rowm - colm).astype(F32)
    pos = lax.broadcasted_iota(jnp.int32, (t, RET_DK), 0).astype(F32)
    half = RET_DK // 2
    for h in range(RET_HEADS):
        lg = math.log1p(-(2.0 ** (-5.0 - h)))
        hs = slice(h * RET_DK, (h + 1) * RET_DK)
        qh = q[:, hs]
        kh = k[:, hs]
        qr = qh * cos + pltpu.roll(qh, half, axis=1) * sin
        kr = (kh * cos + pltpu.roll(kh, half, axis=1) * sin) * (RET_DK ** -0.5)
        vb = v[:, hs].astype(BF16)
        dmask = jnp.where(rel >= 0, jnp.exp(jnp.minimum(lg * rel, 0.0)), 0.0)
        scores = _dot_nt(qr.astype(BF16), kr.astype(BF16)) * dmask
        st = st_ref[h]
        xi = jnp.exp(lg * (pos + 1.0))
        o = _dot(scores.astype(BF16), vb) + _dot((qr * xi).astype(BF16), st.astype(BF16))
        zeta = jnp.exp(lg * (t - 1.0 - pos))
        st_ref[h] = st * math.exp(lg * t) + _dot_tn((kr * zeta).astype(BF16), vb)
        o_ref[:, hs] = (_rms(o) * _silu(g[:, hs])).astype(BF16)


def _ret(proj, pos, inv, batch, seq, t=256):
    nt = seq // t
    pos_spec = pl.BlockSpec((1, t, 1), lambda b, i: (b, i, 0))
    inv_spec = pl.BlockSpec((1, RET_DK), lambda b, i: (0, 0))
    return pl.pallas_call(
        functools.partial(_ret_kernel, t=t),
        out_shape=jax.ShapeDtypeStruct((batch * seq, GROUP_WIDTH), BF16),
        grid=(batch, nt),
        in_specs=[_col_spec(t, 512, "b_q", nt), _col_spec(t, 512, "b_k", nt),
                  _col_spec(t, 512, "b_v", nt), _col_spec(t, 512, "b_g", nt), pos_spec, inv_spec],
        out_specs=_mixer_out_spec(t, nt),
        scratch_shapes=[pltpu.VMEM((RET_HEADS, RET_DK, RET_DK), F32)],
        compiler_params=_mixer_params(),
        name="ret_mixer",
    )(proj, proj, proj, proj, pos, inv)


def _rope_partial(x, c, s_lo, s_hi):
    n = x.shape[-1]
    half = ROPE_DIM // 2
    return x * c + pltpu.roll(x, n - half, axis=1) * s_lo + pltpu.roll(x, half, axis=1) * s_hi


def _swa_window(q_of, kc_raw, vc, pos, inv, has_prev, sinks, kprev_ref, vprev_ref, store):
    w = WINDOW
    lane = lax.broadcasted_iota(jnp.int32, (1, LANES), 1)
    lo_half = lane < SWA_HD
    ang = pos * inv
    in_head = lane & (SWA_HD - 1)
    c = jnp.cos(ang)
    s_lo = jnp.where(in_head < ROPE_DIM // 2, -jnp.sin(ang), 0.0)
    s_hi = jnp.where((in_head >= ROPE_DIM // 2) & (in_head < ROPE_DIM), jnp.sin(ang), 0.0)
    kc = _rope_partial(kc_raw, c, s_lo, s_hi)
    group = SWA_Q_HEADS // SWA_KV_HEADS
    rowm = lax.broadcasted_iota(jnp.int32, (group * w, w), 0) & (w - 1)
    colm = lax.broadcasted_iota(jnp.int32, (group * w, w), 1)
    cur_ok = colm <= rowm
    prev_ok = (colm > rowm) & has_prev

    def both_halves(a, kv):
        swapped = pltpu.roll(a, SWA_HD, axis=1)
        return jnp.where(lo_half, a, swapped) if kv == 0 else jnp.where(lo_half, swapped, a)

    for kv in range(SWA_KV_HEADS):
        k2c = both_halves(kc, kv).astype(BF16)
        k2p = kprev_ref[kv]
        v2c = both_halves(vc, kv)
        v_half = [(jnp.where(lo_half, v2c, jnp.where(lane == SWA_HD, 1.0, 0.0)).astype(BF16), vprev_ref[kv, 0]),
                  (jnp.where(lo_half, jnp.where(lane == 0, 1.0, 0.0), v2c).astype(BF16), vprev_ref[kv, 1])]
        ones_lane = (SWA_HD, 0)
        kprev_ref[kv] = k2c
        vprev_ref[kv, 0] = v_half[0][0]
        vprev_ref[kv, 1] = v_half[1][0]
        q_rows, sink_rows = [], []
        for t in range(kv * group // 2, (kv + 1) * group // 2):
            qt = _rope_partial(q_of(t), c, s_lo, s_hi) * (SWA_HD ** -0.5)
            q_rows += [jnp.where(lo_half, qt, 0.0), jnp.where(lo_half, 0.0, qt)]
            sink_rows += [jnp.broadcast_to(sinks[:, 2 * t:2 * t + 1], (w, 1)),
                          jnp.broadcast_to(sinks[:, 2 * t + 1:2 * t + 2], (w, 1))]
        q4 = jnp.concatenate(q_rows, axis=0).astype(BF16)
        sink = jnp.concatenate(sink_rows, axis=0)
        s_c = jnp.where(cur_ok, _dot_nt(q4, k2c), -jnp.inf)
        s_p = jnp.where(prev_ok, _dot_nt(q4, k2p), -jnp.inf)
        m = jnp.maximum(jnp.max(jnp.maximum(s_c, s_p), axis=-1, keepdims=True), sink)
        e_c = jnp.exp(s_c - m).astype(BF16)
        e_p = jnp.exp(s_p - m).astype(BF16)
        e_sink = jnp.exp(sink - m)
        for n, t in enumerate(range(kv * group // 2, (kv + 1) * group // 2)):
            halves = []
            for half in range(2):
                rs = slice((2 * n + half) * w, (2 * n + half + 1) * w)
                vh_c, vh_p = v_half[half]
                o = _dot(e_c[rs], vh_c) + _dot(e_p[rs], vh_p)
                denom = o[:, ones_lane[half]:ones_lane[half] + 1] + e_sink[rs]
                halves.append(o * (1.0 / denom))
            store(t, jnp.where(lo_half, halves[0], halves[1]).astype(BF16))


def _swa_kernel(q_ref, kc_ref, vc_ref, pos_ref, inv_ref, sink_ref, o_ref, kprev_ref, vprev_ref, *, t):
    first = pl.program_id(1) == 0

    @pl.when(first)
    def _():
        kprev_ref[...] = jnp.zeros_like(kprev_ref)
        vprev_ref[...] = jnp.zeros_like(vprev_ref)

    sinks = sink_ref[...]
    inv = inv_ref[...]
    for i in range(t // WINDOW):
        rows = slice(i * WINDOW, (i + 1) * WINDOW)
        has_prev = jnp.logical_not(first) if i == 0 else True

        def store(tile, value, rows=rows):
            o_ref[rows, tile * LANES:(tile + 1) * LANES] = value

        _swa_window(lambda tile, rows=rows: q_ref[rows, tile * LANES:(tile + 1) * LANES],
                    kc_ref[rows, :], vc_ref[rows, :], pos_ref[0, rows], inv, has_prev, sinks,
                    kprev_ref, vprev_ref, store)


def _swa(proj, pos, inv, sinks, batch, seq, t=512):
    nt = seq // t
    kv_w = SWA_KV_HEADS * SWA_HD
    cur = lambda name: pl.BlockSpec((t, kv_w), lambda b, i: (b * nt + i, _DST[name] // kv_w))
    pos_spec = pl.BlockSpec((1, t, 1), lambda b, i: (b, i, 0))
    inv_spec = pl.BlockSpec((1, LANES), lambda b, i: (0, 0))
    return pl.pallas_call(
        functools.partial(_swa_kernel, t=t),
        out_shape=jax.ShapeDtypeStruct((batch * seq, GROUP_WIDTH), BF16),
        grid=(batch, nt),
        in_specs=[_col_spec(t, 512, "c_q", nt), cur("c_k"), cur("c_v"), pos_spec, inv_spec,
                  pl.BlockSpec((1, SWA_Q_HEADS), lambda b, i: (0, 0))],
        out_specs=_mixer_out_spec(t, nt),
        scratch_shapes=[pltpu.VMEM((SWA_KV_HEADS, WINDOW, LANES), BF16),
                        pltpu.VMEM((SWA_KV_HEADS, 2, WINDOW, LANES), BF16)],
        compiler_params=_mixer_params(),
        name="swa_mixer",
    )(proj, proj, proj, pos, inv, sinks.reshape(1, -1))


def _out_proj_kernel(x_ref, mod_ref, a_ref, b_ref, c_ref, d_ref, w_ref, o_ref):
    gw = GROUP_WIDTH
    acc = _dot(a_ref[...], w_ref[0:gw, :])
    acc += _dot(b_ref[...], w_ref[gw:2 * gw, :])
    acc += _dot(c_ref[...], w_ref[2 * gw:3 * gw, :])
    acc += _dot(d_ref[...], w_ref[3 * gw:4 * gw, :])
    gate = mod_ref[0][2:3]
    o_ref[...] = x_ref[...] + gate * acc


def _out_proj(x2, mod_l, mixed, w_out, layer, seq, tm=512):
    m_rows, d = x2.shape
    per_b = seq // tm
    grp = pl.BlockSpec((tm, GROUP_WIDTH), lambda i: (i, 0))
    return pl.pallas_call(
        _out_proj_kernel,
        out_shape=jax.ShapeDtypeStruct((m_rows, d), F32),
        grid=(m_rows // tm,),
        in_specs=[pl.BlockSpec((tm, d), lambda i: (i, 0)),
                  pl.BlockSpec((1, 6, d), lambda i: (i // per_b, 0, 0)),
                  grp, grp, grp, grp,
                  pl.BlockSpec((None, d, d), lambda i: (layer, 0, 0))],
        out_specs=pl.BlockSpec((tm, d), lambda i: (i, 0)),
        compiler_params=pltpu.CompilerParams(
            dimension_semantics=("arbitrary",), vmem_limit_bytes=VMEM_LIMIT),
        name="out_proj",
    )(x2, mod_l, *mixed, w_out)


def _ffn_kernel(x_ref, mod_ref, nw_ref, wg_ref, wu_ref, wd_ref, fw_ref, o_ref, h_ref, acc_ref, *, final):
    f = pl.program_id(1)

    @pl.when(f == 0)
    def _():
        m = mod_ref[0]
        h = _modulated_norm(x_ref[...], nw_ref[...], m[3:4], m[4:5])
        h_ref[...] = h.astype(BF16)
        acc_ref[...] = jnp.zeros_like(acc_ref)

    h = h_ref[...]
    g = _dot(h, wg_ref[...])
    u = _dot(h, wu_ref[...])
    act = (_silu(g) * u).astype(BF16)
    acc_ref[...] += _dot(act, wd_ref[...])

    @pl.when(f == pl.num_programs(1) - 1)
    def _():
        y = x_ref[...] + mod_ref[0][5:6] * acc_ref[...]
        o_ref[...] = _rms(y) * fw_ref[...] if final else y


def _ffn(x2, mod_l, nw, w_in, w_down, layer, final_w, final, seq, tm=512, tf=512):
    m_rows, d = x2.shape
    nf = D_FF // tf
    per_b = seq // tm
    return pl.pallas_call(
        functools.partial(_ffn_kernel, final=final),
        out_shape=jax.ShapeDtypeStruct((m_rows, d), F32),
        grid=(m_rows // tm, nf),
        in_specs=[pl.BlockSpec((tm, d), lambda i, f: (i, 0)),
                  pl.BlockSpec((1, 6, d), lambda i, f: (i // per_b, 0, 0)),
                  pl.BlockSpec((1, d), lambda i, f: (0, 0)),
                  pl.BlockSpec((None, d, tf), lambda i, f: (layer, 0, f)),
                  pl.BlockSpec((None, d, tf), lambda i, f: (layer, 0, nf + f)),
                  pl.BlockSpec((None, tf, d), lambda i, f: (layer, f, 0)),
                  pl.BlockSpec((1, d), lambda i, f: (0, 0))],
        out_specs=pl.BlockSpec((tm, d), lambda i, f: (i, 0)),
        scratch_shapes=[pltpu.VMEM((tm, d), BF16), pltpu.VMEM((tm, d), F32)],
        compiler_params=pltpu.CompilerParams(
            dimension_semantics=("arbitrary", "arbitrary"), vmem_limit_bytes=VMEM_LIMIT),
        name="ffn",
    )(x2, mod_l, nw.reshape(1, d), w_in, w_in, w_down, final_w.reshape(1, d))


CAST_ROWS = 256
CAST_COLS = 2816


def _cast_kernel(x_ref, o_ref):
    o_ref[...] = x_ref[...].astype(BF16)


def _to_bf16(w):
    depth, k, n = w.shape
    tn = n if n <= CAST_COLS else CAST_COLS
    blk = pl.BlockSpec((1, CAST_ROWS, tn), lambda l, i, j: (l, i, j))
    return pl.pallas_call(
        _cast_kernel,
        out_shape=jax.ShapeDtypeStruct(w.shape, BF16),
        grid=(depth, k // CAST_ROWS, n // tn),
        in_specs=[blk], out_specs=blk,
        compiler_params=pltpu.CompilerParams(
            dimension_semantics=("arbitrary",) * 3, vmem_limit_bytes=VMEM_LIMIT),
        name="cast_weights",
    )(w)


def _permute_kernel(x_ref, o_ref):
    off = 0
    for name, width in _DST_ORDER:
        o_ref[0, :, off:off + width] = x_ref[0, :, _SRC[name]:_SRC[name] + width].astype(BF16)
        off += width
    o_ref[0, :, off:] = jnp.zeros((o_ref.shape[1], PROJ_WIDTH - off), BF16)


def _permute_w_in(w):
    depth, k, n = w.shape
    return pl.pallas_call(
        _permute_kernel,
        out_shape=jax.ShapeDtypeStruct((depth, k, PROJ_WIDTH), BF16),
        grid=(depth, k // CAST_ROWS),
        in_specs=[pl.BlockSpec((1, CAST_ROWS, n), lambda l, i: (l, i, 0))],
        out_specs=pl.BlockSpec((1, CAST_ROWS, PROJ_WIDTH), lambda l, i: (l, i, 0)),
        compiler_params=pltpu.CompilerParams(
            dimension_semantics=("arbitrary", "arbitrary"), vmem_limit_bytes=VMEM_LIMIT),
        name="permute_w_in",
    )(w)


def _ret_inv_freq():
    inv = 1.0 / jnp.power(RET_ROT_BASE, jnp.linspace(0.0, 1.0, RET_DK // 2, dtype=F32))
    return jnp.concatenate([inv, inv])[None, :]


def _swa_inv_freq():
    half = ROPE_DIM // 2
    inv = 1.0 / jnp.power(ROPE_THETA, jnp.arange(half, dtype=F32) / half)
    per_head = jnp.concatenate([inv, inv, jnp.zeros((SWA_HD - ROPE_DIM,), F32)])
    return jnp.concatenate([per_head] * (LANES // SWA_HD))[None, :]


def kernel(x, c, positions, w_ada, b_ada, norm1_w, w_in, gla_gate_w2, gla_gate_b2, gla_norm_w, swa_sinks, hgrn_lb, hgrn_norm_w, w_out, norm2_w, w_ffn_in, w_ffn_down, final_norm_w):
    batch, seq, d = x.shape
    depth = w_ada.shape[0]
    mod = _ada_modulation(c, w_ada, b_ada).reshape(depth, batch, 6, d)
    pos = positions.astype(F32)[:, :, None]
    ret_inv, swa_inv = _ret_inv_freq(), _swa_inv_freq()
    x2 = x.reshape(batch * seq, d)
    w_in_b, w_out_b = _permute_w_in(w_in), _to_bf16(w_out)
    w_ffn_in_b, w_ffn_down_b = _to_bf16(w_ffn_in), _to_bf16(w_ffn_down)
    for l in range(depth):
        w2_pad = jnp.zeros((LANES, GLA_HEADS * GLA_DK), F32).at[:GLA_GATE_RANK].set(gla_gate_w2[l])
        proj = _in_proj(x2, mod[l], norm1_w[l], w_in_b, l, seq)
        mixed = (
            _gla(proj, w2_pad, gla_gate_b2[l], gla_norm_w[l], batch, seq),
            _ret(proj, pos, ret_inv, batch, seq),
            _swa(proj, pos, swa_inv, swa_sinks[l], batch, seq),
            _hgrn(proj, hgrn_lb, hgrn_norm_w[l], l, batch, seq),
        )
        x2 = _out_proj(x2, mod[l], mixed, w_out_b, l, seq)
        x2 = _ffn(x2, mod[l], norm2_w[l], w_ffn_in_b, w_ffn_down_b, l, final_norm_w, l == depth - 1, seq)
    return x2.reshape(batch, seq, d)
```

```python
import functools
import math

import jax
import jax.numpy as jnp
from jax import lax
from jax.experimental import pallas as pl
from jax.experimental.pallas import tpu as pltpu

F32 = jnp.float32
BF16 = jnp.bfloat16

D_MODEL = 2048
DEPTH = 2
EPS = 1e-6
GROUP_WIDTH = D_MODEL // 4
CHUNK = 64
SUB = 16
N_SUB = CHUNK // SUB
GLA_HEADS = 4
GLA_DV = GROUP_WIDTH // GLA_HEADS
GLA_DK = GLA_DV // 2
GLA_GATE_RANK = 16
GLA_GATE_NORMALIZER = 16.0
RET_HEADS = 4
RET_DK = GROUP_WIDTH // RET_HEADS
RET_ROT_BASE = 10000.0
SWA_Q_HEADS = 8
SWA_KV_HEADS = 2
SWA_HD = GROUP_WIDTH // SWA_Q_HEADS
WINDOW = 128
ROPE_THETA = 500000.0
ROPE_DIM = SWA_HD // 4
HG_HEADS = 4
HG_DK = GROUP_WIDTH // HG_HEADS
D_FF = ((8 * D_MODEL + 3 * 256 - 1) // (3 * 256)) * 256

LANES = 128
VMEM_LIMIT = 48 * 1024 * 1024
VMEM_LIMIT_IN_PROJ = 56 * 1024 * 1024

_SRC = dict(a_q=0, a_k=256, a_v=512, a_g=1024, a_r=1536,
            b_q=1552, b_k=2064, b_v=2576, b_g=3088,
            c_q=3600, c_k=4112, c_v=4240,
            d_q=4368, d_f=4880, d_i=5392, d_g=5904)
_DST_ORDER = (("a_q", 256), ("a_k", 256), ("a_v", 512), ("a_g", 512), ("c_q", 512),
              ("b_q", 512), ("b_k", 512), ("b_v", 512), ("b_g", 512),
              ("d_q", 512), ("d_f", 512), ("d_i", 512), ("d_g", 512),
              ("c_k", 128), ("c_v", 128), ("a_r", GLA_GATE_RANK))
PROJ_WIDTH = 6528
PROJ_TN = PROJ_WIDTH // 3


def _dst_offsets():
    offs, o = {}, 0
    for name, w in _DST_ORDER:
        offs[name] = o
        o += w
    return offs


_DST = _dst_offsets()


def _dot(a, b):
    return jnp.dot(a, b, preferred_element_type=F32)


def _dot_nt(a, b):
    return lax.dot_general(a, b, (((1,), (1,)), ((), ())), preferred_element_type=F32)


def _dot_tn(a, b):
    return lax.dot_general(a, b, (((0,), (0,)), ((), ())), preferred_element_type=F32)


def _split_bf16(x):
    hi = x.astype(BF16)
    lo = (x - hi.astype(F32)).astype(BF16)
    return hi, lo


def _dot3(a, b):
    a_hi, a_lo = _split_bf16(a)
    b_hi, b_lo = _split_bf16(b)
    return _dot(a_hi, b_hi) + (_dot(a_hi, b_lo) + _dot(a_lo, b_hi))


def _sigmoid(x):
    return 1.0 / (1.0 + jnp.exp(-x))


def _silu(x):
    return (0.5 * x) * (1.0 + jnp.tanh(0.5 * x))


def _log_sigmoid(x):
    return jnp.minimum(x, 0.0) - jnp.log1p(jnp.exp(-jnp.abs(x)))


def _f32(x):
    return x.astype(F32)


def _rms(x):
    return x * lax.rsqrt(jnp.mean(x * x, axis=-1, keepdims=True) + EPS)


def _ada_kernel(c_ref, w_ref, b_ref, o_ref):
    cond = _silu(c_ref[...])
    o_ref[0] = _dot3(cond, w_ref[0]) + b_ref[0]


def _ada_modulation(c, w_ada, b_ada):
    depth, d, n = w_ada.shape
    rows = 8
    c_pad = jnp.zeros((rows, d), F32).at[:c.shape[0]].set(c)
    tn = 1024
    out = pl.pallas_call(
        _ada_kernel,
        out_shape=jax.ShapeDtypeStruct((depth, rows, n), F32),
        grid=(depth, n // tn),
        in_specs=[pl.BlockSpec((rows, d), lambda l, j: (0, 0)),
                  pl.BlockSpec((1, d, tn), lambda l, j: (l, 0, j)),
                  pl.BlockSpec((1, 1, tn), lambda l, j: (l, 0, j))],
        out_specs=pl.BlockSpec((1, rows, tn), lambda l, j: (l, 0, j)),
        compiler_params=pltpu.CompilerParams(
            dimension_semantics=("arbitrary", "arbitrary"), vmem_limit_bytes=VMEM_LIMIT),
        name="ada_modulation",
    )(c_pad, w_ada, b_ada.reshape(depth, 1, n))
    return out[:, :c.shape[0]]


def _modulated_norm(x, nw, shift, scale):
    return (_rms(x) * nw) * (1.0 + scale) + shift


def _in_proj_kernel(x_ref, mod_ref, nw_ref, w_ref, o_ref, h_ref):
    @pl.when(pl.program_id(1) == 0)
    def _():
        m = mod_ref[0]
        h = _modulated_norm(x_ref[...], nw_ref[...], m[0:1], m[1:2])
        h_ref[...] = h.astype(BF16)

    half = h_ref.shape[0] // 2
    for r0 in (0, half):
        o_ref[r0:r0 + half, :] = _dot(h_ref[r0:r0 + half, :], w_ref[...]).astype(BF16)


def _in_proj(x2, mod_l, nw, w_perm, layer, seq, tm=1024):
    m_rows, d = x2.shape
    n = w_perm.shape[2]
    per_b = seq // tm
    return pl.pallas_call(
        _in_proj_kernel,
        out_shape=jax.ShapeDtypeStruct((m_rows, n), BF16),
        grid=(m_rows // tm, n // PROJ_TN),
        in_specs=[pl.BlockSpec((tm, d), lambda i, j: (i, 0)),
                  pl.BlockSpec((1, 6, d), lambda i, j: (i // per_b, 0, 0)),
                  pl.BlockSpec((1, d), lambda i, j: (0, 0)),
                  pl.BlockSpec((None, d, PROJ_TN), lambda i, j: (layer, 0, j))],
        out_specs=pl.BlockSpec((tm, PROJ_TN), lambda i, j: (i, j)),
        scratch_shapes=[pltpu.VMEM((tm, d), BF16)],
        compiler_params=pltpu.CompilerParams(
            dimension_semantics=("arbitrary", "arbitrary"), vmem_limit_bytes=VMEM_LIMIT_IN_PROJ),
        name="in_proj",
    )(x2, mod_l, nw.reshape(1, d), w_perm)


TILE = 2 * CHUNK
GLA_BLOCK = CHUNK
HG_BLOCK = CHUNK // 2
CHUNK_SHIFT = CHUNK.bit_length() - 1
SUB_SHIFT = SUB.bit_length() - 1
SUBLANES = 8
SLAB = 64
DIAG_RUN = 4
assert DIAG_RUN == 4


def _ones_where(mask):
    return jnp.where(mask, 1.0, 0.0).astype(BF16)


def _levels(block):
    return [hs for hs in (CHUNK // 2, CHUNK // 4) if hs >= block]


def _gate_constants(block):
    i = lax.broadcasted_iota(jnp.int32, (TILE, TILE), 0)
    m = lax.broadcasted_iota(jnp.int32, (TILE, TILE), 1)
    same_chunk = (i >> CHUNK_SHIFT) == (m >> CHUNK_SHIFT)
    sub_end = i | (SUB - 1)
    chunk_start = i & ~(CHUNK - 1)
    after = same_chunk & (m > i)
    mats = [same_chunk & (m <= i),
            after & (m <= sub_end),
            after]
    for j in range(N_SUB - 1):
        mats.append(same_chunk & (m > chunk_start + (SUB * j + SUB - 1)) & (m <= i))
    cum = jnp.concatenate([_ones_where(x) for x in mats], axis=0)
    place = _ones_where(m == ((TILE - (i >> 3)) & (TILE - 1)))
    diag_ok = ((i >> SUB_SHIFT) == (m >> SUB_SHIFT)) & (m <= i)
    off_ok = same_chunk & ((m >> SUB_SHIFT) < (i >> SUB_SHIFT))
    sub_of_row = (lax.broadcasted_iota(jnp.int32, (TILE, 1), 0) >> SUB_SHIFT) & (N_SUB - 1)
    score_masks = []
    for hs in _levels(block):
        same = (i >> (2 * hs).bit_length() - 1) == (m >> (2 * hs).bit_length() - 1)
        score_masks.append(same & ((i & (2 * hs - 1)) >= hs) & ((m & (2 * hs - 1)) < hs))
    score_masks.append(((i >> block.bit_length() - 1) == (m >> block.bit_length() - 1)) & (m <= i))
    return cum, place, diag_ok, off_ok, sub_of_row, score_masks


def _transpose_tiles(x):
    return jnp.concatenate([x[:, t:t + LANES].T for t in range(0, x.shape[1], LANES)], axis=0)


def _diag_partial_sums(q_t, k_t, g_t):
    dk = q_t.shape[0]
    slab = min(dk, SLAB)
    accs = [None] * SUB
    for s0 in range(0, dk, slab):
        qs = q_t[s0:s0 + slab]
        ks = k_t[s0:s0 + slab]
        g1 = g_t[s0:s0 + slab]
        g2 = g1 * pltpu.roll(g1, 1, axis=1)
        g4 = g2 * pltpu.roll(g2, 2, axis=1)
        g4_back = [g4] + [pltpu.roll(g4, DIAG_RUN * n, axis=1) for n in range(1, SUB // DIAG_RUN - 1)]
        for d0 in range(0, SUB, DIAG_RUN):
            w = ks
            if d0 > 0:
                w = pltpu.roll(ks, d0, axis=1)
                for n in range(d0 // DIAG_RUN):
                    w = w * g4_back[n]
            for d in range(d0, d0 + DIAG_RUN):
                if d > d0:
                    w = pltpu.roll(w, 1, axis=1) * g1
                p = jnp.sum((qs * w).reshape(slab // SUBLANES, SUBLANES, TILE), axis=0)
                accs[d] = p if accs[d] is None else accs[d] + p
    return jnp.concatenate(accs, axis=0)


def _split3_bf16(x):
    hi = x.astype(BF16)
    rest = x - hi.astype(F32)
    mid = rest.astype(BF16)
    lo = (rest - mid.astype(F32)).astype(BF16)
    return hi, mid, lo


def _cum_dot(mat, pieces):
    hi, mid, lo = pieces
    return _dot(mat, hi) + (_dot(mat, mid) + _dot(mat, lo))


def _head_lanes(h, dk):
    tile = (h * dk) // LANES
    if dk >= LANES:
        return tile, None
    lane = lax.broadcasted_iota(jnp.int32, (1, LANES), 1)
    start = (h * dk) % LANES
    return tile, (lane >= start) & (lane < start + dk)


def _gated_tile_robust(q, k, lg, lg_pieces, v, st_ref, consts, n_heads, dk, dv):
    cum, place, diag_ok, off_ok, sub_of_row, _ = consts
    sums = _cum_dot(cum, lg_pieces)
    cb = sums[0:TILE]
    q_state = (q * jnp.exp(cb)).astype(BF16)
    k_end = k * jnp.exp(sums[TILE:2 * TILE])
    k_last = (k * jnp.exp(sums[2 * TILE:3 * TILE])).astype(BF16)
    q_off = [(q * jnp.exp(sums[(3 + j) * TILE:(4 + j) * TILE])).astype(BF16) for j in range(N_SUB - 1)]
    k_off = [jnp.where(sub_of_row == j, k_end, 0.0).astype(BF16) for j in range(N_SUB - 1)]
    q_t = _transpose_tiles(q)
    k_t = _transpose_tiles(k)
    g_t = _transpose_tiles(jnp.exp(lg))
    outs = []
    for h in range(n_heads):
        ks = slice(h * dk, (h + 1) * dk)
        tile = (h * dk) // LANES
        ls = slice((h * dk) % LANES, (h * dk) % LANES + dk)
        vb = v[:, h * dv:(h + 1) * dv].astype(BF16)
        a_off = _dot_nt(q_off[0][:, ks], k_off[0][:, ks])
        for j in range(1, N_SUB - 1):
            a_off += _dot_nt(q_off[j][:, ks], k_off[j][:, ks])
        part = _diag_partial_sums(q_t[ks], k_t[ks], g_t[ks])
        p_hi, p_lo = _split_bf16(part)
        skew = _dot_tn(p_hi, place) + _dot_tn(p_lo, place)
        a_diag = pltpu.roll(skew, 0, axis=1, stride=1, stride_axis=0)
        a = jnp.where(diag_ok, a_diag, jnp.where(off_ok, a_off, 0.0)).astype(BF16)
        o = _dot(a, vb)
        st = st_ref[tile][:, ls]
        inter = []
        for c in range(TILE // CHUNK):
            rs = slice(c * CHUNK, (c + 1) * CHUNK)
            inter.append(_dot_nt(q_state[rs, ks], st.astype(BF16)))
            last = cb[c * CHUNK + CHUNK - 1:(c + 1) * CHUNK, ks]
            st = st * jnp.exp(last) + _dot_tn(vb[rs], k_last[rs, ks])
        st_ref[tile, :, ls] = st
        outs.append(o + jnp.concatenate(inter, axis=0))
    return outs


def _rows_from(cb, row, spans):
    out = None
    for lo, src in spans:
        val = jnp.zeros_like(cb[0:1]) if src is None else cb[src:src + 1]
        out = val if out is None else jnp.where(row >= lo, val, out)
    return out


def _block_prefix(cb, block):
    if block == CHUNK:
        return cb
    row = lax.broadcasted_iota(jnp.int32, (TILE, 1), 0)
    spans = [(b0, None if b0 % CHUNK == 0 else b0 - 1) for b0 in range(0, TILE, block)]
    return cb - _rows_from(cb, row, spans)


def _gated_tile_bounded(q, k, cb, pb, v, st_ref, consts, n_heads, dk, dv, block):
    score_masks = consts[-1]
    row = lax.broadcasted_iota(jnp.int32, (TILE, 1), 0)
    ends = [cb[c * CHUNK + CHUNK - 1:(c + 1) * CHUNK, :] for c in range(TILE // CHUNK)]
    end_of_row = _rows_from(cb, row, [(c * CHUNK, c * CHUNK + CHUNK - 1) for c in range(TILE // CHUNK)])
    q_state = q * jnp.exp(cb)
    k_last = k * jnp.exp(end_of_row - cb)
    q_parts, k_parts = [], []
    for hs in _levels(block):
        ref = _rows_from(cb, row, [(b0, b0 + hs - 1) for b0 in range(0, TILE, 2 * hs)])
        lower = (row & (2 * hs - 1)) < hs
        x = jnp.exp(jnp.where(lower, ref - cb, cb - ref))
        q_parts.append(jnp.where(lower, 0.0, q * x))
        k_parts.append(jnp.where(lower, k * x, 0.0).astype(BF16))
    q_parts.append(q_state if block == CHUNK else q * jnp.exp(pb))
    k_parts.append((k * jnp.exp(-pb)).astype(BF16))
    outs = [None] * n_heads
    heads_per_tile = max(LANES // dk, 1)
    for tile in range(n_heads * dk // LANES):
        ls = slice(tile * LANES, (tile + 1) * LANES)
        heads = range(tile * heads_per_tile, (tile + 1) * heads_per_tile)
        qh, kl, vb, intra = {}, {}, {}, {}
        for h in heads:
            _, mask = _head_lanes(h, dk)
            own = (lambda a: a) if mask is None else (lambda a: jnp.where(mask, a, 0.0))
            qh[h] = own(q_state[:, ls]).astype(BF16)
            kl[h] = own(k_last[:, ls]).astype(BF16)
            vb[h] = v[:, h * dv:(h + 1) * dv].astype(BF16)
            a = 0.0
            for qp, kp, ok in zip(q_parts, k_parts, score_masks):
                a = jnp.where(ok, _dot_nt(own(qp[:, ls]).astype(BF16), kp[:, ls]), a)
            intra[h] = _dot(a.astype(BF16), vb[h])
        st = st_ref[tile]
        inter = {h: [] for h in heads}
        for c in range(TILE // CHUNK):
            rs = slice(c * CHUNK, (c + 1) * CHUNK)
            stb = st.astype(BF16)
            upd = None
            for h in heads:
                inter[h].append(_dot_nt(qh[h][rs], stb))
                u = _dot_tn(vb[h][rs], kl[h][rs])
                upd = u if upd is None else upd + u
            st = st * jnp.exp(ends[c][:, ls]) + upd
        st_ref[tile] = st
        for h in heads:
            outs[h] = intra[h] + jnp.concatenate(inter[h], axis=0)
    return outs


SAFE_DECAY = 64.0


def _gated_tile(q, k, lg, v, st_ref, consts, n_heads, dk, dv, block, finish):
    pieces = _split3_bf16(lg)
    cb = _cum_dot(consts[0][0:TILE], pieces)
    pb = _block_prefix(cb, block)
    bounded = jnp.min(pb) >= -SAFE_DECAY

    @pl.when(bounded)
    def _():
        finish(_gated_tile_bounded(q, k, cb, pb, v, st_ref, consts, n_heads, dk, dv, block))

    @pl.when(jnp.logical_not(bounded))
    def _():
        finish(_gated_tile_robust(q, k, lg, pieces, v, st_ref, consts, n_heads, dk, dv))


def _gla_kernel(q_ref, k_ref, v_ref, g_ref, r_ref, w2_ref, b2_ref, nw_ref, o_ref, st_ref, *, n_tiles):
    @pl.when(pl.program_id(1) == 0)
    def _():
        st_ref[...] = jnp.zeros_like(st_ref)

    consts = _gate_constants(GLA_BLOCK)
    w2 = w2_ref[...]
    b2 = b2_ref[...]
    nw = nw_ref[...]

    def body(c, carry):
        rows = pl.ds(pl.multiple_of(c * TILE, TILE), TILE)
        pre = _dot3(_f32(r_ref[rows, :]), w2) + b2
        lg = _log_sigmoid(pre) * (1.0 / GLA_GATE_NORMALIZER)
        q = _f32(q_ref[rows, :]) * (GLA_DK ** -0.5)

        def finish(outs):
            g = _f32(g_ref[rows, :])
            for h, o in enumerate(outs):
                vs = slice(h * GLA_DV, (h + 1) * GLA_DV)
                o_ref[rows, vs] = ((_rms(o) * nw) * _silu(g[:, vs])).astype(BF16)

        _gated_tile(q, _f32(k_ref[rows, :]), lg, _f32(v_ref[rows, :]), st_ref, consts,
                    GLA_HEADS, GLA_DK, GLA_DV, GLA_BLOCK, finish)
        return carry

    lax.fori_loop(0, n_tiles, body, 0)


def _hgrn_kernel(q_ref, f_ref, i_ref, g_ref, lb_ref, nw_ref, o_ref, st_ref, *, n_tiles, layer):
    @pl.when(pl.program_id(1) == 0)
    def _():
        st_ref[...] = jnp.zeros_like(st_ref)

    consts = _gate_constants(HG_BLOCK)
    lbv = lb_ref[...]
    e = jnp.exp(lbv - jnp.max(lbv, axis=0, keepdims=True))
    soft = e / jnp.sum(e, axis=0, keepdims=True)
    cum = soft[0:1]
    for i in range(1, layer + 1):
        cum = cum + soft[i:i + 1]
    lb = cum - soft[0:1]
    nw = nw_ref[...]

    def body(c, carry):
        rows = pl.ds(pl.multiple_of(c * TILE, TILE), TILE)
        f = lb + (1.0 - lb) * _sigmoid(_f32(f_ref[rows, :]))
        q = _silu(_f32(q_ref[rows, :])) * (HG_DK ** -0.5)

        def finish(outs):
            g = _f32(g_ref[rows, :])
            for h, o in enumerate(outs):
                hs = slice(h * HG_DK, (h + 1) * HG_DK)
                o_ref[rows, hs] = ((_rms(o) * nw) * _silu(g[:, hs])).astype(BF16)

        _gated_tile(q, 1.0 - f, jnp.log(f), _f32(i_ref[rows, :]), st_ref, consts,
                    HG_HEADS, HG_DK, HG_DK, HG_BLOCK, finish)
        return carry

    lax.fori_loop(0, n_tiles, body, 0)


def _col_spec(t, width, name, nt):
    blk = _DST[name] // width
    return pl.BlockSpec((t, width), lambda b, i: (b * nt + i, blk))


def _mixer_out_spec(t, nt):
    return pl.BlockSpec((t, GROUP_WIDTH), lambda b, i: (b * nt + i, 0))


def _mixer_params():
    return pltpu.CompilerParams(dimension_semantics=("arbitrary", "arbitrary"),
                                vmem_limit_bytes=VMEM_LIMIT)


def _gla(proj, w2_pad, b2, nw, batch, seq, t=512):
    nt = seq // t
    full = lambda shape: pl.BlockSpec(shape, lambda b, i: (0, 0))
    return pl.pallas_call(
        functools.partial(_gla_kernel, n_tiles=t // TILE),
        out_shape=jax.ShapeDtypeStruct((batch * seq, GROUP_WIDTH), BF16),
        grid=(batch, nt),
        in_specs=[_col_spec(t, 256, "a_q", nt), _col_spec(t, 256, "a_k", nt),
                  _col_spec(t, 512, "a_v", nt), _col_spec(t, 512, "a_g", nt),
                  _col_spec(t, LANES, "a_r", nt),
                  full((LANES, GLA_HEADS * GLA_DK)), full((1, GLA_HEADS * GLA_DK)), full((1, GLA_DV))],
        out_specs=_mixer_out_spec(t, nt),
        scratch_shapes=[pltpu.VMEM((GLA_HEADS * GLA_DK // LANES, GLA_DV, LANES), F32)],
        compiler_params=_mixer_params(),
        name="gla_mixer",
    )(proj, proj, proj, proj, proj, w2_pad, b2.reshape(1, -1), nw.reshape(1, -1))


def _hgrn(proj, hgrn_lb, nw, layer, batch, seq, t=512):
    nt = seq // t
    full = lambda shape: pl.BlockSpec(shape, lambda b, i: (0, 0))
    return pl.pallas_call(
        functools.partial(_hgrn_kernel, n_tiles=t // TILE, layer=layer),
        out_shape=jax.ShapeDtypeStruct((batch * seq, GROUP_WIDTH), BF16),
        grid=(batch, nt),
        in_specs=[_col_spec(t, 512, "d_q", nt), _col_spec(t, 512, "d_f", nt),
                  _col_spec(t, 512, "d_i", nt), _col_spec(t, 512, "d_g", nt),
                  full(hgrn_lb.shape), full((1, HG_DK))],
        out_specs=_mixer_out_spec(t, nt),
        scratch_shapes=[pltpu.VMEM((HG_HEADS * HG_DK // LANES, HG_DK, LANES), F32)],
        compiler_params=_mixer_params(),
        name="hgrn_mixer",
    )(proj, proj, proj, proj, hgrn_lb, nw.reshape(1, -1))


def _ret_kernel(q_ref, k_ref, v_ref, g_ref, pos_ref, inv_ref, o_ref, st_ref, *, t):
    @pl.when(pl.program_id(1) == 0)
    def _():
        st_ref[...] = jnp.zeros_like(st_ref)

    ang = pos_ref[0] * inv_ref[...]
    lane = lax.broadcasted_iota(jnp.int32, (1, RET_DK), 1)
    cos = jnp.cos(ang)
    sin = jnp.where(lane < RET_DK // 2, -jnp.sin(ang), jnp.sin(ang))
    q = _f32(q_ref[...])
    k = _f32(k_ref[...])
    v = v_ref[...]
    g = _f32(g_ref[...])
    rowm = lax.broadcasted_iota(jnp.int32, (t, t), 0)
    colm = lax.broadcasted_iota(jnp.int32, (t, t), 1)
    rel = (rowm - colm).astype(F32)
    pos = lax.broadcasted_iota(jnp.int32, (t, RET_DK), 0).astype(F32)
    half = RET_DK // 2
    for h in range(RET_HEADS):
        lg = math.log1p(-(2.0 ** (-5.0 - h)))
        hs = slice(h * RET_DK, (h + 1) * RET_DK)
        qh = q[:, hs]
        kh = k[:, hs]
        qr = qh * cos + pltpu.roll(qh, half, axis=1) * sin
        kr = (kh * cos + pltpu.roll(kh, half, axis=1) * sin) * (RET_DK ** -0.5)
        vb = v[:, hs].astype(BF16)
        dmask = jnp.where(rel >= 0, jnp.exp(jnp.minimum(lg * rel, 0.0)), 0.0)
        scores = _dot_nt(qr.astype(BF16), kr.astype(BF16)) * dmask
        st = st_ref[h]
        xi = jnp.exp(lg * (pos + 1.0))
        o = _dot(scores.astype(BF16), vb) + _dot((qr * xi).astype(BF16), st.astype(BF16))
        zeta = jnp.exp(lg * (t - 1.0 - pos))
        st_ref[h] = st * math.exp(lg * t) + _dot_tn((kr * zeta).astype(BF16), vb)
        o_ref[:, hs] = (_rms(o) * _silu(g[:, hs])).astype(BF16)


def _ret(proj, pos, inv, batch, seq, t=256):
    nt = seq // t
    pos_spec = pl.BlockSpec((1, t, 1), lambda b, i: (b, i, 0))
    inv_spec = pl.BlockSpec((1, RET_DK), lambda b, i: (0, 0))
    return pl.pallas_call(
        functools.partial(_ret_kernel, t=t),
        out_shape=jax.ShapeDtypeStruct((batch * seq, GROUP_WIDTH), BF16),
        grid=(batch, nt),
        in_specs=[_col_spec(t, 512, "b_q", nt), _col_spec(t, 512, "b_k", nt),
                  _col_spec(t, 512, "b_v", nt), _col_spec(t, 512, "b_g", nt), pos_spec, inv_spec],
        out_specs=_mixer_out_spec(t, nt),
        scratch_shapes=[pltpu.VMEM((RET_HEADS, RET_DK, RET_DK), F32)],
        compiler_params=_mixer_params(),
        name="ret_mixer",
    )(proj, proj, proj, proj, pos, inv)


def _rope_partial(x, c, s_lo, s_hi):
    n = x.shape[-1]
    half = ROPE_DIM // 2
    return x * c + pltpu.roll(x, n - half, axis=1) * s_lo + pltpu.roll(x, half, axis=1) * s_hi


def _swa_window(q_of, kc_raw, vc, pos, inv, has_prev, sinks, kprev_ref, vprev_ref, store):
    w = WINDOW
    lane = lax.broadcasted_iota(jnp.int32, (1, LANES), 1)
    lo_half = lane < SWA_HD
    ang = pos * inv
    in_head = lane & (SWA_HD - 1)
    c = jnp.cos(ang)
    s_lo = jnp.where(in_head < ROPE_DIM // 2, -jnp.sin(ang), 0.0)
    s_hi = jnp.where((in_head >= ROPE_DIM // 2) & (in_head < ROPE_DIM), jnp.sin(ang), 0.0)
    kc = _rope_partial(kc_raw, c, s_lo, s_hi)
    group = SWA_Q_HEADS // SWA_KV_HEADS
    rowm = lax.broadcasted_iota(jnp.int32, (group * w, w), 0) & (w - 1)
    colm = lax.broadcasted_iota(jnp.int32, (group * w, w), 1)
    cur_ok = colm <= rowm
    prev_ok = (colm > rowm) & has_prev

    def both_halves(a, kv):
        swapped = pltpu.roll(a, SWA_HD, axis=1)
        return jnp.where(lo_half, a, swapped) if kv == 0 else jnp.where(lo_half, swapped, a)

    for kv in range(SWA_KV_HEADS):
        k2c = both_halves(kc, kv).astype(BF16)
        k2p = kprev_ref[kv]
        v2c = both_halves(vc, kv)
        v_half = [(jnp.where(lo_half, v2c, jnp.where(lane == SWA_HD, 1.0, 0.0)).astype(BF16), vprev_ref[kv, 0]),
                  (jnp.where(lo_half, jnp.where(lane == 0, 1.0, 0.0), v2c).astype(BF16), vprev_ref[kv, 1])]
        ones_lane = (SWA_HD, 0)
        kprev_ref[kv] = k2c
        vprev_ref[kv, 0] = v_half[0][0]
        vprev_ref[kv, 1] = v_half[1][0]
        q_rows, sink_rows = [], []
        for t in range(kv * group // 2, (kv + 1) * group // 2):
            qt = _rope_partial(q_of(t), c, s_lo, s_hi) * (SWA_HD ** -0.5)
            q_rows += [jnp.where(lo_half, qt, 0.0), jnp.where(lo_half, 0.0, qt)]
            sink_rows += [jnp.broadcast_to(sinks[:, 2 * t:2 * t + 1], (w, 1)),
                          jnp.broadcast_to(sinks[:, 2 * t + 1:2 * t + 2], (w, 1))]
        q4 = jnp.concatenate(q_rows, axis=0).astype(BF16)
        sink = jnp.concatenate(sink_rows, axis=0)
        s_c = jnp.where(cur_ok, _dot_nt(q4, k2c), -jnp.inf)
        s_p = jnp.where(prev_ok, _dot_nt(q4, k2p), -jnp.inf)
        m = jnp.maximum(jnp.max(jnp.maximum(s_c, s_p), axis=-1, keepdims=True), sink)
        e_c = jnp.exp(s_c - m).astype(BF16)
        e_p = jnp.exp(s_p - m).astype(BF16)
        e_sink = jnp.exp(sink - m)
        for n, t in enumerate(range(kv * group // 2, (kv + 1) * group // 2)):
            halves = []
            for half in range(2):
                rs = slice((2 * n + half) * w, (2 * n + half + 1) * w)
                vh_c, vh_p = v_half[half]
                o = _dot(e_c[rs], vh_c) + _dot(e_p[rs], vh_p)
                denom = o[:, ones_lane[half]:ones_lane[half] + 1] + e_sink[rs]
                halves.append(o * (1.0 / denom))
            store(t, jnp.where(lo_half, halves[0], halves[1]).astype(BF16))


def _swa_kernel(q_ref, kc_ref, vc_ref, pos_ref, inv_ref, sink_ref, o_ref, kprev_ref, vprev_ref, *, t):
    first = pl.program_id(1) == 0

    @pl.when(first)
    def _():
        kprev_ref[...] = jnp.zeros_like(kprev_ref)
        vprev_ref[...] = jnp.zeros_like(vprev_ref)

    sinks = sink_ref[...]
    inv = inv_ref[...]
    for i in range(t // WINDOW):
        rows = slice(i * WINDOW, (i + 1) * WINDOW)
        has_prev = jnp.logical_not(first) if i == 0 else True

        def store(tile, value, rows=rows):
            o_ref[rows, tile * LANES:(tile + 1) * LANES] = value

        _swa_window(lambda tile, rows=rows: _f32(q_ref[rows, tile * LANES:(tile + 1) * LANES]),
                    _f32(kc_ref[rows, :]), _f32(vc_ref[rows, :]), pos_ref[0, rows], inv, has_prev, sinks,
                    kprev_ref, vprev_ref, store)


def _swa(proj, pos, inv, sinks, batch, seq, t=512):
    nt = seq // t
    kv_w = SWA_KV_HEADS * SWA_HD
    cur = lambda name: pl.BlockSpec((t, kv_w), lambda b, i: (b * nt + i, _DST[name] // kv_w))
    pos_spec = pl.BlockSpec((1, t, 1), lambda b, i: (b, i, 0))
    inv_spec = pl.BlockSpec((1, LANES), lambda b, i: (0, 0))
    return pl.pallas_call(
        functools.partial(_swa_kernel, t=t),
        out_shape=jax.ShapeDtypeStruct((batch * seq, GROUP_WIDTH), BF16),
        grid=(batch, nt),
        in_specs=[_col_spec(t, 512, "c_q", nt), cur("c_k"), cur("c_v"), pos_spec, inv_spec,
                  pl.BlockSpec((1, SWA_Q_HEADS), lambda b, i: (0, 0))],
        out_specs=_mixer_out_spec(t, nt),
        scratch_shapes=[pltpu.VMEM((SWA_KV_HEADS, WINDOW, LANES), BF16),
                        pltpu.VMEM((SWA_KV_HEADS, 2, WINDOW, LANES), BF16)],
        compiler_params=_mixer_params(),
        name="swa_mixer",
    )(proj, proj, proj, pos, inv, sinks.reshape(1, -1))


def _out_proj_kernel(x_ref, mod_ref, a_ref, b_ref, c_ref, d_ref, w_ref, o_ref):
    gw = GROUP_WIDTH
    acc = _dot(a_ref[...], w_ref[0:gw, :])
    acc += _dot(b_ref[...], w_ref[gw:2 * gw, :])
    acc += _dot(c_ref[...], w_ref[2 * gw:3 * gw, :])
    acc += _dot(d_ref[...], w_ref[3 * gw:4 * gw, :])
    gate = mod_ref[0][2:3]
    o_ref[...] = x_ref[...] + gate * acc


def _out_proj(x2, mod_l, mixed, w_out, layer, seq, tm=512):
    m_rows, d = x2.shape
    per_b = seq // tm
    grp = pl.BlockSpec((tm, GROUP_WIDTH), lambda i: (i, 0))
    return pl.pallas_call(
        _out_proj_kernel,
        out_shape=jax.ShapeDtypeStruct((m_rows, d), F32),
        grid=(m_rows // tm,),
        in_specs=[pl.BlockSpec((tm, d), lambda i: (i, 0)),
                  pl.BlockSpec((1, 6, d), lambda i: (i // per_b, 0, 0)),
                  grp, grp, grp, grp,
                  pl.BlockSpec((None, d, d), lambda i: (layer, 0, 0))],
        out_specs=pl.BlockSpec((tm, d), lambda i: (i, 0)),
        compiler_params=pltpu.CompilerParams(
            dimension_semantics=("arbitrary",), vmem_limit_bytes=VMEM_LIMIT),
        name="out_proj",
    )(x2, mod_l, *mixed, w_out)


def _ffn_kernel(x_ref, mod_ref, nw_ref, wg_ref, wu_ref, wd_ref, fw_ref, o_ref, h_ref, acc_ref, *, final):
    f = pl.program_id(1)

    @pl.when(f == 0)
    def _():
        m = mod_ref[0]
        h = _modulated_norm(x_ref[...], nw_ref[...], m[3:4], m[4:5])
        h_ref[...] = h.astype(BF16)
        acc_ref[...] = jnp.zeros_like(acc_ref)

    h = h_ref[...]
    g = _dot(h, wg_ref[...])
    u = _dot(h, wu_ref[...])
    act = (_silu(g) * u).astype(BF16)
    acc_ref[...] += _dot(act, wd_ref[...])

    @pl.when(f == pl.num_programs(1) - 1)
    def _():
        y = x_ref[...] + mod_ref[0][5:6] * acc_ref[...]
        o_ref[...] = _rms(y) * fw_ref[...] if final else y


def _ffn(x2, mod_l, nw, w_in, w_down, layer, final_w, final, seq, tm=512, tf=512):
    m_rows, d = x2.shape
    nf = D_FF // tf
    per_b = seq // tm
    return pl.pallas_call(
        functools.partial(_ffn_kernel, final=final),
        out_shape=jax.ShapeDtypeStruct((m_rows, d), F32),
        grid=(m_rows // tm, nf),
        in_specs=[pl.BlockSpec((tm, d), lambda i, f: (i, 0)),
                  pl.BlockSpec((1, 6, d), lambda i, f: (i // per_b, 0, 0)),
                  pl.BlockSpec((1, d), lambda i, f: (0, 0)),
                  pl.BlockSpec((None, d, tf), lambda i, f: (layer, 0, f)),
                  pl.BlockSpec((None, d, tf), lambda i, f: (layer, 0, nf + f)),
                  pl.BlockSpec((None, tf, d), lambda i, f: (layer, f, 0)),
                  pl.BlockSpec((1, d), lambda i, f: (0, 0))],
        out_specs=pl.BlockSpec((tm, d), lambda i, f: (i, 0)),
        scratch_shapes=[pltpu.VMEM((tm, d), BF16), pltpu.VMEM((tm, d), F32)],
        compiler_params=pltpu.CompilerParams(
            dimension_semantics=("arbitrary", "arbitrary"), vmem_limit_bytes=VMEM_LIMIT),
        name="ffn",
    )(x2, mod_l, nw.reshape(1, d), w_in, w_in, w_down, final_w.reshape(1, d))


CAST_ROWS = 256
CAST_COLS = 2816


def _cast_kernel(x_ref, o_ref):
    o_ref[...] = x_ref[...].astype(BF16)


def _to_bf16(w):
    depth, k, n = w.shape
    tn = n if n <= CAST_COLS else CAST_COLS
    blk = pl.BlockSpec((1, CAST_ROWS, tn), lambda l, i, j: (l, i, j))
    return pl.pallas_call(
        _cast_kernel,
        out_shape=jax.ShapeDtypeStruct(w.shape, BF16),
        grid=(depth, k // CAST_ROWS, n // tn),
        in_specs=[blk], out_specs=blk,
        compiler_params=pltpu.CompilerParams(
            dimension_semantics=("arbitrary",) * 3, vmem_limit_bytes=VMEM_LIMIT),
        name="cast_weights",
    )(w)


def _permute_kernel(x_ref, o_ref):
    off = 0
    for name, width in _DST_ORDER:
        o_ref[0, :, off:off + width] = x_ref[0, :, _SRC[name]:_SRC[name] + width].astype(BF16)
        off += width
    o_ref[0, :, off:] = jnp.zeros((o_ref.shape[1], PROJ_WIDTH - off), BF16)


def _permute_w_in(w):
    depth, k, n = w.shape
    return pl.pallas_call(
        _permute_kernel,
        out_shape=jax.ShapeDtypeStruct((depth, k, PROJ_WIDTH), BF16),
        grid=(depth, k // CAST_ROWS),
        in_specs=[pl.BlockSpec((1, CAST_ROWS, n), lambda l, i: (l, i, 0))],
        out_specs=pl.BlockSpec((1, CAST_ROWS, PROJ_WIDTH), lambda l, i: (l, i, 0)),
        compiler_params=pltpu.CompilerParams(
            dimension_semantics=("arbitrary", "arbitrary"), vmem_limit_bytes=VMEM_LIMIT),
        name="permute_w_in",
    )(w)


def _ret_inv_freq():
    inv = 1.0 / jnp.power(RET_ROT_BASE, jnp.linspace(0.0, 1.0, RET_DK // 2, dtype=F32))
    return jnp.concatenate([inv, inv])[None, :]


def _swa_inv_freq():
    half = ROPE_DIM // 2
    inv = 1.0 / jnp.power(ROPE_THETA, jnp.arange(half, dtype=F32) / half)
    per_head = jnp.concatenate([inv, inv, jnp.zeros((SWA_HD - ROPE_DIM,), F32)])
    return jnp.concatenate([per_head] * (LANES // SWA_HD))[None, :]


def kernel(x, c, positions, w_ada, b_ada, norm1_w, w_in, gla_gate_w2, gla_gate_b2, gla_norm_w, swa_sinks, hgrn_lb, hgrn_norm_w, w_out, norm2_w, w_ffn_in, w_ffn_down, final_norm_w):
    batch, seq, d = x.shape
    depth = w_ada.shape[0]
    mod = _ada_modulation(c, w_ada, b_ada).reshape(depth, batch, 6, d)
    pos = positions.astype(F32)[:, :, None]
    ret_inv, swa_inv = _ret_inv_freq(), _swa_inv_freq()
    x2 = x.reshape(batch * seq, d)
    w_in_b, w_out_b = _permute_w_in(w_in), _to_bf16(w_out)
    w_ffn_in_b, w_ffn_down_b = _to_bf16(w_ffn_in), _to_bf16(w_ffn_down)
    for l in range(depth):
        w2_pad = jnp.zeros((LANES, GLA_HEADS * GLA_DK), F32).at[:GLA_GATE_RANK].set(gla_gate_w2[l])
        proj = _in_proj(x2, mod[l], norm1_w[l], w_in_b, l, seq)
        mixed = (
            _gla(proj, w2_pad, gla_gate_b2[l], gla_norm_w[l], batch, seq),
            _ret(proj, pos, ret_inv, batch, seq),
            _swa(proj, pos, swa_inv, swa_sinks[l], batch, seq),
            _hgrn(proj, hgrn_lb, hgrn_norm_w[l], l, batch, seq),
        )
        x2 = _out_proj(x2, mod[l], mixed, w_out_b, l, seq)
        x2 = _ffn(x2, mod[l], norm2_w[l], w_ffn_in_b, w_ffn_down_b, l, final_norm_w, l == depth - 1, seq)
    return x2.reshape(batch, seq, d)
```

```python
import functools
import math

import jax
import jax.numpy as jnp
from jax import lax
from jax.experimental import pallas as pl
from jax.experimental.pallas import tpu as pltpu

F32 = jnp.float32
BF16 = jnp.bfloat16

D_MODEL = 2048
DEPTH = 2
EPS = 1e-6
GROUP_WIDTH = D_MODEL // 4
CHUNK = 64
SUB = 16
N_SUB = CHUNK // SUB
GLA_HEADS = 4
GLA_DV = GROUP_WIDTH // GLA_HEADS
GLA_DK = GLA_DV // 2
GLA_GATE_RANK = 16
GLA_GATE_NORMALIZER = 16.0
RET_HEADS = 4
RET_DK = GROUP_WIDTH // RET_HEADS
RET_ROT_BASE = 10000.0
SWA_Q_HEADS = 8
SWA_KV_HEADS = 2
SWA_HD = GROUP_WIDTH // SWA_Q_HEADS
WINDOW = 128
ROPE_THETA = 500000.0
ROPE_DIM = SWA_HD // 4
HG_HEADS = 4
HG_DK = GROUP_WIDTH // HG_HEADS
D_FF = ((8 * D_MODEL + 3 * 256 - 1) // (3 * 256)) * 256

LANES = 128
VMEM_LIMIT = 48 * 1024 * 1024
VMEM_LIMIT_IN_PROJ = 56 * 1024 * 1024

_SRC = dict(a_q=0, a_k=256, a_v=512, a_g=1024, a_r=1536,
            b_q=1552, b_k=2064, b_v=2576, b_g=3088,
            c_q=3600, c_k=4112, c_v=4240,
            d_q=4368, d_f=4880, d_i=5392, d_g=5904)
_DST_ORDER = (("a_q", 256), ("a_k", 256), ("a_v", 512), ("a_g", 512), ("c_q", 512),
              ("b_q", 512), ("b_k", 512), ("b_v", 512), ("b_g", 512),
              ("d_q", 512), ("d_f", 512), ("d_i", 512), ("d_g", 512),
              ("c_k", 128), ("c_v", 128), ("a_r", GLA_GATE_RANK))
PROJ_WIDTH = 6528
PROJ_TN = PROJ_WIDTH // 3


def _dst_offsets():
    offs, o = {}, 0
    for name, w in _DST_ORDER:
        offs[name] = o
        o += w
    return offs


_DST = _dst_offsets()


def _dot(a, b):
    return jnp.dot(a, b, preferred_element_type=F32)


def _dot_nt(a, b):
    return lax.dot_general(a, b, (((1,), (1,)), ((), ())), preferred_element_type=F32)


def _dot_tn(a, b):
    return lax.dot_general(a, b, (((0,), (0,)), ((), ())), preferred_element_type=F32)


def _split_bf16(x):
    hi = x.astype(BF16)
    lo = (x - hi.astype(F32)).astype(BF16)
    return hi, lo


def _dot3(a, b):
    a_hi, a_lo = _split_bf16(a)
    b_hi, b_lo = _split_bf16(b)
    return _dot(a_hi, b_hi) + (_dot(a_hi, b_lo) + _dot(a_lo, b_hi))


def _sigmoid(x):
    return 1.0 / (1.0 + jnp.exp(-x))


def _silu(x):
    return (0.5 * x) * (1.0 + jnp.tanh(0.5 * x))


def _log_sigmoid(x):
    return jnp.minimum(x, 0.0) - jnp.log1p(jnp.exp(-jnp.abs(x)))


def _f32(x):
    return x.astype(F32)


def _rms(x):
    return x * lax.rsqrt(jnp.mean(x * x, axis=-1, keepdims=True) + EPS)


def _ada_kernel(c_ref, w_ref, b_ref, o_ref):
    cond = _silu(c_ref[...])
    o_ref[0] = _dot3(cond, w_ref[0]) + b_ref[0]


def _ada_modulation(c, w_ada, b_ada):
    depth, d, n = w_ada.shape
    rows = 8
    c_pad = jnp.zeros((rows, d), F32).at[:c.shape[0]].set(c)
    tn = 1024
    out = pl.pallas_call(
        _ada_kernel,
        out_shape=jax.ShapeDtypeStruct((depth, rows, n), F32),
        grid=(depth, n // tn),
        in_specs=[pl.BlockSpec((rows, d), lambda l, j: (0, 0)),
                  pl.BlockSpec((1, d, tn), lambda l, j: (l, 0, j)),
                  pl.BlockSpec((1, 1, tn), lambda l, j: (l, 0, j))],
        out_specs=pl.BlockSpec((1, rows, tn), lambda l, j: (l, 0, j)),
        compiler_params=pltpu.CompilerParams(
            dimension_semantics=("arbitrary", "arbitrary"), vmem_limit_bytes=VMEM_LIMIT),
        name="ada_modulation",
    )(c_pad, w_ada, b_ada.reshape(depth, 1, n))
    return out[:, :c.shape[0]]


def _modulated_norm(x, nw, shift, scale):
    return _rms(x) * (nw * (1.0 + scale)) + shift


def _in_proj_kernel(x_ref, mod_ref, nw_ref, w_ref, o_ref, h_ref):
    @pl.when(pl.program_id(1) == 0)
    def _():
        m = mod_ref[0]
        h = _modulated_norm(x_ref[...], nw_ref[...], m[0:1], m[1:2])
        h_ref[...] = h.astype(BF16)

    half = h_ref.shape[0] // 2
    for r0 in (0, half):
        o_ref[r0:r0 + half, :] = _dot_nt(h_ref[r0:r0 + half, :], w_ref[...]).astype(BF16)


def _in_proj(x2, mod_l, nw, w_perm, layer, seq, tm=1024):
    m_rows, d = x2.shape
    n = w_perm.shape[1]
    per_b = seq // tm
    return pl.pallas_call(
        _in_proj_kernel,
        out_shape=jax.ShapeDtypeStruct((m_rows, n), BF16),
        grid=(m_rows // tm, n // PROJ_TN),
        in_specs=[pl.BlockSpec((tm, d), lambda i, j: (i, 0)),
                  pl.BlockSpec((1, 6, d), lambda i, j: (i // per_b, 0, 0)),
                  pl.BlockSpec((1, d), lambda i, j: (0, 0)),
                  pl.BlockSpec((None, PROJ_TN, d), lambda i, j: (layer, j, 0))],
        out_specs=pl.BlockSpec((tm, PROJ_TN), lambda i, j: (i, j)),
        scratch_shapes=[pltpu.VMEM((tm, d), BF16)],
        compiler_params=pltpu.CompilerParams(
            dimension_semantics=("arbitrary", "arbitrary"), vmem_limit_bytes=VMEM_LIMIT_IN_PROJ),
        name="in_proj",
    )(x2, mod_l, nw.reshape(1, d), w_perm)


TILE = 2 * CHUNK
GLA_BLOCK = CHUNK
HG_BLOCK = CHUNK // 2
CHUNK_SHIFT = CHUNK.bit_length() - 1
SUB_SHIFT = SUB.bit_length() - 1
SUBLANES = 8
SLAB = 64
DIAG_RUN = 4
assert DIAG_RUN == 4


def _ones_where(mask):
    return jnp.where(mask, 1.0, 0.0).astype(BF16)


def _levels(block):
    return [hs for hs in (CHUNK // 2, CHUNK // 4) if hs >= block]


def _gate_constants(block):
    i = lax.broadcasted_iota(jnp.int32, (TILE, TILE), 0)
    m = lax.broadcasted_iota(jnp.int32, (TILE, TILE), 1)
    same_chunk = (i >> CHUNK_SHIFT) == (m >> CHUNK_SHIFT)
    sub_end = i | (SUB - 1)
    chunk_start = i & ~(CHUNK - 1)
    after = same_chunk & (m > i)
    mats = [same_chunk & (m <= i),
            after & (m <= sub_end),
            after]
    for j in range(N_SUB - 1):
        mats.append(same_chunk & (m > chunk_start + (SUB * j + SUB - 1)) & (m <= i))
    cum = jnp.concatenate([_ones_where(x) for x in mats], axis=0)
    place = _ones_where(m == ((TILE - (i >> 3)) & (TILE - 1)))
    diag_ok = ((i >> SUB_SHIFT) == (m >> SUB_SHIFT)) & (m <= i)
    off_ok = same_chunk & ((m >> SUB_SHIFT) < (i >> SUB_SHIFT))
    sub_of_row = (lax.broadcasted_iota(jnp.int32, (TILE, 1), 0) >> SUB_SHIFT) & (N_SUB - 1)
    score_masks = []
    for hs in _levels(block):
        same = (i >> (2 * hs).bit_length() - 1) == (m >> (2 * hs).bit_length() - 1)
        score_masks.append(same & ((i & (2 * hs - 1)) >= hs) & ((m & (2 * hs - 1)) < hs))
    score_masks.append(((i >> block.bit_length() - 1) == (m >> block.bit_length() - 1)) & (m <= i))
    return cum, place, diag_ok, off_ok, sub_of_row, score_masks


def _transpose_tiles(x):
    return jnp.concatenate([x[:, t:t + LANES].T for t in range(0, x.shape[1], LANES)], axis=0)


def _diag_partial_sums(q_t, k_t, g_t):
    dk = q_t.shape[0]
    slab = min(dk, SLAB)
    accs = [None] * SUB
    for s0 in range(0, dk, slab):
        qs = q_t[s0:s0 + slab]
        ks = k_t[s0:s0 + slab]
        g1 = g_t[s0:s0 + slab]
        g2 = g1 * pltpu.roll(g1, 1, axis=1)
        g4 = g2 * pltpu.roll(g2, 2, axis=1)
        g4_back = [g4] + [pltpu.roll(g4, DIAG_RUN * n, axis=1) for n in range(1, SUB // DIAG_RUN - 1)]
        for d0 in range(0, SUB, DIAG_RUN):
            w = ks
            if d0 > 0:
                w = pltpu.roll(ks, d0, axis=1)
                for n in range(d0 // DIAG_RUN):
                    w = w * g4_back[n]
            for d in range(d0, d0 + DIAG_RUN):
                if d > d0:
                    w = pltpu.roll(w, 1, axis=1) * g1
                p = jnp.sum((qs * w).reshape(slab // SUBLANES, SUBLANES, TILE), axis=0)
                accs[d] = p if accs[d] is None else accs[d] + p
    return jnp.concatenate(accs, axis=0)


def _split3_bf16(x):
    hi = x.astype(BF16)
    rest = x - hi.astype(F32)
    mid = rest.astype(BF16)
    lo = (rest - mid.astype(F32)).astype(BF16)
    return hi, mid, lo


def _cum_dot(mat, pieces):
    hi, mid, lo = pieces
    return _dot(mat, hi) + (_dot(mat, mid) + _dot(mat, lo))


def _head_lanes(h, dk):
    tile = (h * dk) // LANES
    if dk >= LANES:
        return tile, None
    lane = lax.broadcasted_iota(jnp.int32, (1, LANES), 1)
    start = (h * dk) % LANES
    return tile, (lane >= start) & (lane < start + dk)


def _gated_tile_robust(q, k, lg, lg_pieces, v, st_ref, consts, n_heads, dk, dv):
    cum, place, diag_ok, off_ok, sub_of_row, _ = consts
    sums = _cum_dot(cum, lg_pieces)
    cb = sums[0:TILE]
    q_state = (q * jnp.exp(cb)).astype(BF16)
    k_end = k * jnp.exp(sums[TILE:2 * TILE])
    k_last = (k * jnp.exp(sums[2 * TILE:3 * TILE])).astype(BF16)
    q_off = [(q * jnp.exp(sums[(3 + j) * TILE:(4 + j) * TILE])).astype(BF16) for j in range(N_SUB - 1)]
    k_off = [jnp.where(sub_of_row == j, k_end, 0.0).astype(BF16) for j in range(N_SUB - 1)]
    q_t = _transpose_tiles(q)
    k_t = _transpose_tiles(k)
    g_t = _transpose_tiles(jnp.exp(lg))
    outs = []
    for h in range(n_heads):
        ks = slice(h * dk, (h + 1) * dk)
        tile = (h * dk) // LANES
        ls = slice((h * dk) % LANES, (h * dk) % LANES + dk)
        vb = v[:, h * dv:(h + 1) * dv].astype(BF16)
        a_off = _dot_nt(q_off[0][:, ks], k_off[0][:, ks])
        for j in range(1, N_SUB - 1):
            a_off += _dot_nt(q_off[j][:, ks], k_off[j][:, ks])
        part = _diag_partial_sums(q_t[ks], k_t[ks], g_t[ks])
        p_hi, p_lo = _split_bf16(part)
        skew = _dot_tn(p_hi, place) + _dot_tn(p_lo, place)
        a_diag = pltpu.roll(skew, 0, axis=1, stride=1, stride_axis=0)
        a = jnp.where(diag_ok, a_diag, jnp.where(off_ok, a_off, 0.0)).astype(BF16)
        o = _dot(a, vb)
        st = st_ref[tile][:, ls]
        inter = []
        for c in range(TILE // CHUNK):
            rs = slice(c * CHUNK, (c + 1) * CHUNK)
            inter.append(_dot_nt(q_state[rs, ks], st.astype(BF16)))
            last = cb[c * CHUNK + CHUNK - 1:(c + 1) * CHUNK, ks]
            st = st * jnp.exp(last) + _dot_tn(vb[rs], k_last[rs, ks])
        st_ref[tile, :, ls] = st
        outs.append(o + jnp.concatenate(inter, axis=0))
    return outs


def _rows_from(cb, row, spans):
    out = None
    for lo, src in spans:
        val = jnp.zeros_like(cb[0:1]) if src is None else cb[src:src + 1]
        out = val if out is None else jnp.where(row >= lo, val, out)
    return out


def _block_prefix(cb, block):
    if block == CHUNK:
        return cb
    row = lax.broadcasted_iota(jnp.int32, (TILE, 1), 0)
    spans = [(b0, None if b0 % CHUNK == 0 else b0 - 1) for b0 in range(0, TILE, block)]
    return cb - _rows_from(cb, row, spans)


def _gated_tile_bounded(q, k, cb, pb, v, st_ref, consts, n_heads, dk, dv, block):
    score_masks = consts[-1]
    row = lax.broadcasted_iota(jnp.int32, (TILE, 1), 0)
    ends = [cb[c * CHUNK + CHUNK - 1:(c + 1) * CHUNK, :] for c in range(TILE // CHUNK)]
    end_of_row = _rows_from(cb, row, [(c * CHUNK, c * CHUNK + CHUNK - 1) for c in range(TILE // CHUNK)])
    q_state = q * jnp.exp(cb)
    k_last = k * jnp.exp(end_of_row - cb)
    q_parts, k_parts = [], []
    for hs in _levels(block):
        ref = _rows_from(cb, row, [(b0, b0 + hs - 1) for b0 in range(0, TILE, 2 * hs)])
        lower = (row & (2 * hs - 1)) < hs
        x = jnp.exp(jnp.where(lower, ref - cb, cb - ref))
        q_parts.append(jnp.where(lower, 0.0, q * x))
        k_parts.append(jnp.where(lower, k * x, 0.0).astype(BF16))
    q_parts.append(q_state if block == CHUNK else q * jnp.exp(pb))
    k_parts.append((k * jnp.exp(-pb)).astype(BF16))
    outs = [None] * n_heads
    heads_per_tile = max(LANES // dk, 1)
    for tile in range(n_heads * dk // LANES):
        ls = slice(tile * LANES, (tile + 1) * LANES)
        heads = range(tile * heads_per_tile, (tile + 1) * heads_per_tile)
        qh, kl, vb, intra = {}, {}, {}, {}
        for h in heads:
            _, mask = _head_lanes(h, dk)
            own = (lambda a: a) if mask is None else (lambda a: jnp.where(mask, a, 0.0))
            qh[h] = own(q_state[:, ls]).astype(BF16)
            kl[h] = own(k_last[:, ls]).astype(BF16)
            vb[h] = v[:, h * dv:(h + 1) * dv].astype(BF16)
            a = 0.0
            for qp, kp, ok in zip(q_parts, k_parts, score_masks):
                a = jnp.where(ok, _dot_nt(own(qp[:, ls]).astype(BF16), kp[:, ls]), a)
            intra[h] = _dot(a.astype(BF16), vb[h])
        st = st_ref[tile]
        inter = {h: [] for h in heads}
        for c in range(TILE // CHUNK):
            rs = slice(c * CHUNK, (c + 1) * CHUNK)
            stb = st.astype(BF16)
            upd = None
            for h in heads:
                inter[h].append(_dot_nt(qh[h][rs], stb))
                u = _dot_tn(vb[h][rs], kl[h][rs])
                upd = u if upd is None else upd + u
            st = st * jnp.exp(ends[c][:, ls]) + upd
        st_ref[tile] = st
        for h in heads:
            outs[h] = intra[h] + jnp.concatenate(inter[h], axis=0)
    return outs


SAFE_DECAY = 64.0


def _gated_tile(q, k, lg, v, st_ref, consts, n_heads, dk, dv, block, finish):
    pieces = _split3_bf16(lg)
    cb = _cum_dot(consts[0][0:TILE], pieces)
    pb = _block_prefix(cb, block)
    bounded = jnp.min(pb) >= -SAFE_DECAY

    @pl.when(bounded)
    def _():
        finish(_gated_tile_bounded(q, k, cb, pb, v, st_ref, consts, n_heads, dk, dv, block))

    @pl.when(jnp.logical_not(bounded))
    def _():
        finish(_gated_tile_robust(q, k, lg, pieces, v, st_ref, consts, n_heads, dk, dv))


def _gla_kernel(q_ref, k_ref, v_ref, g_ref, r_ref, w2_ref, b2_ref, nw_ref, o_ref, st_ref, *, n_tiles):
    @pl.when(pl.program_id(1) == 0)
    def _():
        st_ref[...] = jnp.zeros_like(st_ref)

    consts = _gate_constants(GLA_BLOCK)
    w2 = w2_ref[...]
    b2 = b2_ref[...]
    nw = nw_ref[...]

    def body(c, carry):
        rows = pl.ds(pl.multiple_of(c * TILE, TILE), TILE)
        pre = _dot3(_f32(r_ref[rows, :]), w2) + b2
        lg = _log_sigmoid(pre) * (1.0 / GLA_GATE_NORMALIZER)
        q = _f32(q_ref[rows, :]) * (GLA_DK ** -0.5)

        def finish(outs):
            g = _f32(g_ref[rows, :])
            for h, o in enumerate(outs):
                vs = slice(h * GLA_DV, (h + 1) * GLA_DV)
                o_ref[rows, vs] = ((_rms(o) * nw) * _silu(g[:, vs])).astype(BF16)

        _gated_tile(q, _f32(k_ref[rows, :]), lg, _f32(v_ref[rows, :]), st_ref, consts,
                    GLA_HEADS, GLA_DK, GLA_DV, GLA_BLOCK, finish)
        return carry

    lax.fori_loop(0, n_tiles, body, 0)


def _hgrn_kernel(q_ref, f_ref, i_ref, g_ref, lb_ref, nw_ref, o_ref, st_ref, *, n_tiles, layer):
    @pl.when(pl.program_id(1) == 0)
    def _():
        st_ref[...] = jnp.zeros_like(st_ref)

    consts = _gate_constants(HG_BLOCK)
    lbv = lb_ref[...]
    e = jnp.exp(lbv - jnp.max(lbv, axis=0, keepdims=True))
    soft = e / jnp.sum(e, axis=0, keepdims=True)
    cum = soft[0:1]
    for i in range(1, layer + 1):
        cum = cum + soft[i:i + 1]
    lb = cum - soft[0:1]
    nw = nw_ref[...]

    def body(c, carry):
        rows = pl.ds(pl.multiple_of(c * TILE, TILE), TILE)
        f = lb + (1.0 - lb) * _sigmoid(_f32(f_ref[rows, :]))
        q = _silu(_f32(q_ref[rows, :])) * (HG_DK ** -0.5)

        def finish(outs):
            g = _f32(g_ref[rows, :])
            for h, o in enumerate(outs):
                hs = slice(h * HG_DK, (h + 1) * HG_DK)
                o_ref[rows, hs] = ((_rms(o) * nw) * _silu(g[:, hs])).astype(BF16)

        _gated_tile(q, 1.0 - f, jnp.log(f), _f32(i_ref[rows, :]), st_ref, consts,
                    HG_HEADS, HG_DK, HG_DK, HG_BLOCK, finish)
        return carry

    lax.fori_loop(0, n_tiles, body, 0)


def _col_spec(t, width, name, nt):
    blk = _DST[name] // width
    return pl.BlockSpec((t, width), lambda b, i: (b * nt + i, blk))


def _mixer_out_spec(t, nt):
    return pl.BlockSpec((t, GROUP_WIDTH), lambda b, i: (b * nt + i, 0))


def _mixer_params():
    return pltpu.CompilerParams(dimension_semantics=("arbitrary", "arbitrary"),
                                vmem_limit_bytes=VMEM_LIMIT)


def _gla(proj, w2_pad, b2, nw, batch, seq, t=512):
    nt = seq // t
    full = lambda shape: pl.BlockSpec(shape, lambda b, i: (0, 0))
    return pl.pallas_call(
        functools.partial(_gla_kernel, n_tiles=t // TILE),
        out_shape=jax.ShapeDtypeStruct((batch * seq, GROUP_WIDTH), BF16),
        grid=(batch, nt),
        in_specs=[_col_spec(t, 256, "a_q", nt), _col_spec(t, 256, "a_k", nt),
                  _col_spec(t, 512, "a_v", nt), _col_spec(t, 512, "a_g", nt),
                  _col_spec(t, LANES, "a_r", nt),
                  full((LANES, GLA_HEADS * GLA_DK)), full((1, GLA_HEADS * GLA_DK)), full((1, GLA_DV))],
        out_specs=_mixer_out_spec(t, nt),
        scratch_shapes=[pltpu.VMEM((GLA_HEADS * GLA_DK // LANES, GLA_DV, LANES), F32)],
        compiler_params=_mixer_params(),
        name="gla_mixer",
    )(proj, proj, proj, proj, proj, w2_pad, b2.reshape(1, -1), nw.reshape(1, -1))


def _hgrn(proj, hgrn_lb, nw, layer, batch, seq, t=512):
    nt = seq // t
    full = lambda shape: pl.BlockSpec(shape, lambda b, i: (0, 0))
    return pl.pallas_call(
        functools.partial(_hgrn_kernel, n_tiles=t // TILE, layer=layer),
        out_shape=jax.ShapeDtypeStruct((batch * seq, GROUP_WIDTH), BF16),
        grid=(batch, nt),
        in_specs=[_col_spec(t, 512, "d_q", nt), _col_spec(t, 512, "d_f", nt),
                  _col_spec(t, 512, "d_i", nt), _col_spec(t, 512, "d_g", nt),
                  full(hgrn_lb.shape), full((1, HG_DK))],
        out_specs=_mixer_out_spec(t, nt),
        scratch_shapes=[pltpu.VMEM((HG_HEADS * HG_DK // LANES, HG_DK, LANES), F32)],
        compiler_params=_mixer_params(),
        name="hgrn_mixer",
    )(proj, proj, proj, proj, hgrn_lb, nw.reshape(1, -1))


def _ret_kernel(q_ref, k_ref, v_ref, g_ref, pos_ref, inv_ref, o_ref, st_ref, *, t):
    @pl.when(pl.program_id(1) == 0)
    def _():
        st_ref[...] = jnp.zeros_like(st_ref)

    ang = pos_ref[0] * inv_ref[...]
    lane = lax.broadcasted_iota(jnp.int32, (1, RET_DK), 1)
    cos = jnp.cos(ang)
    sin = jnp.where(lane < RET_DK // 2, -jnp.sin(ang), jnp.sin(ang))
    q = _f32(q_ref[...])
    k = _f32(k_ref[...])
    v = v_ref[...]
    g = _f32(g_ref[...])
    rowm = lax.broadcasted_iota(jnp.int32, (t, t), 0)
    colm = lax.broadcasted_iota(jnp.int32, (t, t), 1)
    rel = (rowm - colm).astype(F32)
    pos = lax.broadcasted_iota(jnp.int32, (t, RET_DK), 0).astype(F32)
    half = RET_DK // 2
    for h in range(RET_HEADS):
        lg = math.log1p(-(2.0 ** (-5.0 - h)))
        hs = slice(h * RET_DK, (h + 1) * RET_DK)
        qh = q[:, hs]
        kh = k[:, hs]
        qr = qh * cos + pltpu.roll(qh, half, axis=1) * sin
        kr = (kh * cos + pltpu.roll(kh, half, axis=1) * sin) * (RET_DK ** -0.5)
        vb = v[:, hs].astype(BF16)
        dmask = jnp.where(rel >= 0, jnp.exp(jnp.minimum(lg * rel, 0.0)), 0.0)
        scores = _dot_nt(qr.astype(BF16), kr.astype(BF16)) * dmask
        st = st_ref[h]
        xi = jnp.exp(lg * (pos + 1.0))
        o = _dot(scores.astype(BF16), vb) + _dot((qr * xi).astype(BF16), st.astype(BF16))
        zeta = jnp.exp(lg * (t - 1.0 - pos))
        st_ref[h] = st * math.exp(lg * t) + _dot_tn((kr * zeta).astype(BF16), vb)
        o_ref[:, hs] = (_rms(o) * _silu(g[:, hs])).astype(BF16)


def _ret(proj, pos, inv, batch, seq, t=256):
    nt = seq // t
    pos_spec = pl.BlockSpec((1, t, 1), lambda b, i: (b, i, 0))
    inv_spec = pl.BlockSpec((1, RET_DK), lambda b, i: (0, 0))
    return pl.pallas_call(
        functools.partial(_ret_kernel, t=t),
        out_shape=jax.ShapeDtypeStruct((batch * seq, GROUP_WIDTH), BF16),
        grid=(batch, nt),
        in_specs=[_col_spec(t, 512, "b_q", nt), _col_spec(t, 512, "b_k", nt),
                  _col_spec(t, 512, "b_v", nt), _col_spec(t, 512, "b_g", nt), pos_spec, inv_spec],
        out_specs=_mixer_out_spec(t, nt),
        scratch_shapes=[pltpu.VMEM((RET_HEADS, RET_DK, RET_DK), F32)],
        compiler_params=_mixer_params(),
        name="ret_mixer",
    )(proj, proj, proj, proj, pos, inv)


def _rope_partial(x, c, s_lo, s_hi):
    n = x.shape[-1]
    half = ROPE_DIM // 2
    return x * c + pltpu.roll(x, n - half, axis=1) * s_lo + pltpu.roll(x, half, axis=1) * s_hi


def _swa_window(q_of, kc_raw, vc, pos, inv, has_prev, sinks, kprev_ref, vprev_ref, store):
    w = WINDOW
    lane = lax.broadcasted_iota(jnp.int32, (1, LANES), 1)
    lo_half = lane < SWA_HD
    ang = pos * inv
    in_head = lane & (SWA_HD - 1)
    c = jnp.cos(ang)
    s_lo = jnp.where(in_head < ROPE_DIM // 2, -jnp.sin(ang), 0.0)
    s_hi = jnp.where((in_head >= ROPE_DIM // 2) & (in_head < ROPE_DIM), jnp.sin(ang), 0.0)
    kc = _rope_partial(kc_raw, c, s_lo, s_hi)
    group = SWA_Q_HEADS // SWA_KV_HEADS
    rowm = lax.broadcasted_iota(jnp.int32, (group * w, w), 0) & (w - 1)
    colm = lax.broadcasted_iota(jnp.int32, (group * w, w), 1)
    cur_ok = colm <= rowm
    prev_ok = (colm > rowm) & has_prev

    def both_halves(a, kv):
        swapped = pltpu.roll(a, SWA_HD, axis=1)
        return jnp.where(lo_half, a, swapped) if kv == 0 else jnp.where(lo_half, swapped, a)

    for kv in range(SWA_KV_HEADS):
        k2c = both_halves(kc, kv).astype(BF16)
        k2p = kprev_ref[kv]
        v2c = both_halves(vc, kv)
        v_half = [(jnp.where(lo_half, v2c, jnp.where(lane == SWA_HD, 1.0, 0.0)).astype(BF16), vprev_ref[kv, 0]),
                  (jnp.where(lo_half, jnp.where(lane == 0, 1.0, 0.0), v2c).astype(BF16), vprev_ref[kv, 1])]
        ones_lane = (SWA_HD, 0)
        kprev_ref[kv] = k2c
        vprev_ref[kv, 0] = v_half[0][0]
        vprev_ref[kv, 1] = v_half[1][0]
        q_rows, sink_rows = [], []
        for t in range(kv * group // 2, (kv + 1) * group // 2):
            qt = _rope_partial(q_of(t), c, s_lo, s_hi) * (SWA_HD ** -0.5)
            q_rows += [jnp.where(lo_half, qt, 0.0), jnp.where(lo_half, 0.0, qt)]
            sink_rows += [jnp.broadcast_to(sinks[:, 2 * t:2 * t + 1], (w, 1)),
                          jnp.broadcast_to(sinks[:, 2 * t + 1:2 * t + 2], (w, 1))]
        q4 = jnp.concatenate(q_rows, axis=0).astype(BF16)
        sink = jnp.concatenate(sink_rows, axis=0)
        s_c = jnp.where(cur_ok, _dot_nt(q4, k2c), -jnp.inf)
        s_p = jnp.where(prev_ok, _dot_nt(q4, k2p), -jnp.inf)
        m = jnp.maximum(jnp.max(jnp.maximum(s_c, s_p), axis=-1, keepdims=True), sink)
        e_c = jnp.exp(s_c - m).astype(BF16)
        e_p = jnp.exp(s_p - m).astype(BF16)
        e_sink = jnp.exp(sink - m)
        for n, t in enumerate(range(kv * group // 2, (kv + 1) * group // 2)):
            halves = []
            for half in range(2):
                rs = slice((2 * n + half) * w, (2 * n + half + 1) * w)
                vh_c, vh_p = v_half[half]
                o = _dot(e_c[rs], vh_c) + _dot(e_p[rs], vh_p)
                denom = o[:, ones_lane[half]:ones_lane[half] + 1] + e_sink[rs]
                halves.append(o * (1.0 / denom))
            store(t, jnp.where(lo_half, halves[0], halves[1]).astype(BF16))


def _swa_kernel(q_ref, kc_ref, vc_ref, pos_ref, inv_ref, sink_ref, o_ref, kprev_ref, vprev_ref, *, t):
    first = pl.program_id(1) == 0

    @pl.when(first)
    def _():
        kprev_ref[...] = jnp.zeros_like(kprev_ref)
        vprev_ref[...] = jnp.zeros_like(vprev_ref)

    sinks = sink_ref[...]
    inv = inv_ref[...]
    for i in range(t // WINDOW):
        rows = slice(i * WINDOW, (i + 1) * WINDOW)
        has_prev = jnp.logical_not(first) if i == 0 else True

        def store(tile, value, rows=rows):
            o_ref[rows, tile * LANES:(tile + 1) * LANES] = value

        _swa_window(lambda tile, rows=rows: _f32(q_ref[rows, tile * LANES:(tile + 1) * LANES]),
                    _f32(kc_ref[rows, :]), _f32(vc_ref[rows, :]), pos_ref[0, rows], inv, has_prev, sinks,
                    kprev_ref, vprev_ref, store)


def _swa(proj, pos, inv, sinks, batch, seq, t=512):
    nt = seq // t
    kv_w = SWA_KV_HEADS * SWA_HD
    cur = lambda name: pl.BlockSpec((t, kv_w), lambda b, i: (b * nt + i, _DST[name] // kv_w))
    pos_spec = pl.BlockSpec((1, t, 1), lambda b, i: (b, i, 0))
    inv_spec = pl.BlockSpec((1, LANES), lambda b, i: (0, 0))
    return pl.pallas_call(
        functools.partial(_swa_kernel, t=t),
        out_shape=jax.ShapeDtypeStruct((batch * seq, GROUP_WIDTH), BF16),
        grid=(batch, nt),
        in_specs=[_col_spec(t, 512, "c_q", nt), cur("c_k"), cur("c_v"), pos_spec, inv_spec,
                  pl.BlockSpec((1, SWA_Q_HEADS), lambda b, i: (0, 0))],
        out_specs=_mixer_out_spec(t, nt),
        scratch_shapes=[pltpu.VMEM((SWA_KV_HEADS, WINDOW, LANES), BF16),
                        pltpu.VMEM((SWA_KV_HEADS, 2, WINDOW, LANES), BF16)],
        compiler_params=_mixer_params(),
        name="swa_mixer",
    )(proj, proj, proj, pos, inv, sinks.reshape(1, -1))


def _out_proj_kernel(x_ref, mod_ref, a_ref, b_ref, c_ref, d_ref, w_ref, o_ref):
    gw = GROUP_WIDTH
    acc = _dot(a_ref[...], w_ref[0:gw, :])
    acc += _dot(b_ref[...], w_ref[gw:2 * gw, :])
    acc += _dot(c_ref[...], w_ref[2 * gw:3 * gw, :])
    acc += _dot(d_ref[...], w_ref[3 * gw:4 * gw, :])
    gate = mod_ref[0][2:3]
    o_ref[...] = x_ref[...] + gate * acc


def _out_proj(x2, mod_l, mixed, w_out, layer, seq, tm=512):
    m_rows, d = x2.shape
    per_b = seq // tm
    grp = pl.BlockSpec((tm, GROUP_WIDTH), lambda i: (i, 0))
    return pl.pallas_call(
        _out_proj_kernel,
        out_shape=jax.ShapeDtypeStruct((m_rows, d), F32),
        grid=(m_rows // tm,),
        in_specs=[pl.BlockSpec((tm, d), lambda i: (i, 0)),
                  pl.BlockSpec((1, 6, d), lambda i: (i // per_b, 0, 0)),
                  grp, grp, grp, grp,
                  pl.BlockSpec((None, d, d), lambda i: (layer, 0, 0))],
        out_specs=pl.BlockSpec((tm, d), lambda i: (i, 0)),
        compiler_params=pltpu.CompilerParams(
            dimension_semantics=("arbitrary",), vmem_limit_bytes=VMEM_LIMIT),
        name="out_proj",
    )(x2, mod_l, *mixed, w_out)


def _ffn_kernel(x_ref, mod_ref, nw_ref, wg_ref, wu_ref, wd_ref, fw_ref, o_ref, h_ref, acc_ref, *, final):
    f = pl.program_id(1)

    @pl.when(f == 0)
    def _():
        m = mod_ref[0]
        h = _modulated_norm(x_ref[...], nw_ref[...], m[3:4], m[4:5])
        h_ref[...] = h.astype(BF16)
        acc_ref[...] = jnp.zeros_like(acc_ref)

    h = h_ref[...]
    g = _dot(h, wg_ref[...])
    u = _dot(h, wu_ref[...])
    act = (_silu(g) * u).astype(BF16)
    acc_ref[...] += _dot(act, wd_ref[...])

    @pl.when(f == pl.num_programs(1) - 1)
    def _():
        y = x_ref[...] + mod_ref[0][5:6] * acc_ref[...]
        o_ref[...] = _rms(y) * fw_ref[...] if final else y


def _ffn(x2, mod_l, nw, w_in, w_down, layer, final_w, final, seq, tm=512, tf=512):
    m_rows, d = x2.shape
    nf = D_FF // tf
    per_b = seq // tm
    return pl.pallas_call(
        functools.partial(_ffn_kernel, final=final),
        out_shape=jax.ShapeDtypeStruct((m_rows, d), F32),
        grid=(m_rows // tm, nf),
        in_specs=[pl.BlockSpec((tm, d), lambda i, f: (i, 0)),
                  pl.BlockSpec((1, 6, d), lambda i, f: (i // per_b, 0, 0)),
                  pl.BlockSpec((1, d), lambda i, f: (0, 0)),
                  pl.BlockSpec((None, d, tf), lambda i, f: (layer, 0, f)),
                  pl.BlockSpec((None, d, tf), lambda i, f: (layer, 0, nf + f)),
                  pl.BlockSpec((None, tf, d), lambda i, f: (layer, f, 0)),
                  pl.BlockSpec((1, d), lambda i, f: (0, 0))],
        out_specs=pl.BlockSpec((tm, d), lambda i, f: (i, 0)),
        scratch_shapes=[pltpu.VMEM((tm, d), BF16), pltpu.VMEM((tm, d), F32)],
        compiler_params=pltpu.CompilerParams(
            dimension_semantics=("arbitrary", "arbitrary"), vmem_limit_bytes=VMEM_LIMIT),
        name="ffn",
    )(x2, mod_l, nw.reshape(1, d), w_in, w_in, w_down, final_w.reshape(1, d))


CAST_ROWS = 256
CAST_COLS = 2816


def _cast_kernel(x_ref, o_ref):
    o_ref[...] = x_ref[...].astype(BF16)


def _to_bf16(w):
    depth, k, n = w.shape
    tn = n if n <= CAST_COLS else CAST_COLS
    blk = pl.BlockSpec((1, CAST_ROWS, tn), lambda l, i, j: (l, i, j))
    return pl.pallas_call(
        _cast_kernel,
        out_shape=jax.ShapeDtypeStruct(w.shape, BF16),
        grid=(depth, k // CAST_ROWS, n // tn),
        in_specs=[blk], out_specs=blk,
        compiler_params=pltpu.CompilerParams(
            dimension_semantics=("arbitrary",) * 3, vmem_limit_bytes=VMEM_LIMIT),
        name="cast_weights",
    )(w)


def _permute_kernel(x_ref, o_ref):
    off = 0
    for name, width in _DST_ORDER:
        o_ref[0, off:off + width, :] = x_ref[0, _SRC[name]:_SRC[name] + width, :].astype(BF16)
        off += width
    o_ref[0, off:, :] = jnp.zeros((PROJ_WIDTH - off, o_ref.shape[2]), BF16)


def _permute_w_in(w):
    w_t = jnp.swapaxes(w, 1, 2)
    depth, n, k = w_t.shape
    return pl.pallas_call(
        _permute_kernel,
        out_shape=jax.ShapeDtypeStruct((depth, PROJ_WIDTH, k), BF16),
        grid=(depth, k // CAST_ROWS),
        in_specs=[pl.BlockSpec((1, n, CAST_ROWS), lambda l, i: (l, 0, i))],
        out_specs=pl.BlockSpec((1, PROJ_WIDTH, CAST_ROWS), lambda l, i: (l, 0, i)),
        compiler_params=pltpu.CompilerParams(
            dimension_semantics=("arbitrary", "arbitrary"), vmem_limit_bytes=VMEM_LIMIT),
        name="permute_w_in",
    )(w_t)


def _ret_inv_freq():
    inv = 1.0 / jnp.power(RET_ROT_BASE, jnp.linspace(0.0, 1.0, RET_DK // 2, dtype=F32))
    return jnp.concatenate([inv, inv])[None, :]


def _swa_inv_freq():
    half = ROPE_DIM // 2
    inv = 1.0 / jnp.power(ROPE_THETA, jnp.arange(half, dtype=F32) / half)
    per_head = jnp.concatenate([inv, inv, jnp.zeros((SWA_HD - ROPE_DIM,), F32)])
    return jnp.concatenate([per_head] * (LANES // SWA_HD))[None, :]


def kernel(x, c, positions, w_ada, b_ada, norm1_w, w_in, gla_gate_w2, gla_gate_b2, gla_norm_w, swa_sinks, hgrn_lb, hgrn_norm_w, w_out, norm2_w, w_ffn_in, w_ffn_down, final_norm_w):
    batch, seq, d = x.shape
    depth = w_ada.shape[0]
    mod = _ada_modulation(c, w_ada, b_ada).reshape(depth, batch, 6, d)
    pos = positions.astype(F32)[:, :, None]
    ret_inv, swa_inv = _ret_inv_freq(), _swa_inv_freq()
    x2 = x.reshape(batch * seq, d)
    w_in_b, w_out_b = _permute_w_in(w_in), _to_bf16(w_out)
    w_ffn_in_b, w_ffn_down_b = _to_bf16(w_ffn_in), _to_bf16(w_ffn_down)
    for l in range(depth):
        w2_pad = jnp.zeros((LANES, GLA_HEADS * GLA_DK), F32).at[:GLA_GATE_RANK].set(gla_gate_w2[l])
        proj = _in_proj(x2, mod[l], norm1_w[l], w_in_b, l, seq)
        mixed = (
            _gla(proj, w2_pad, gla_gate_b2[l], gla_norm_w[l], batch, seq),
            _ret(proj, pos, ret_inv, batch, seq),
            _swa(proj, pos, swa_inv, swa_sinks[l], batch, seq),
            _hgrn(proj, hgrn_lb, hgrn_norm_w[l], l, batch, seq),
        )
        x2 = _out_proj(x2, mod[l], mixed, w_out_b, l, seq)
        x2 = _ffn(x2, mod[l], norm2_w[l], w_ffn_in_b, w_ffn_down_b, l, final_norm_w, l == depth - 1, seq)
    return x2.reshape(batch, seq, d)
```

```python
import functools
import math

import jax
import jax.numpy as jnp
from jax import lax
from jax.experimental import pallas as pl
from jax.experimental.pallas import tpu as pltpu

F32 = jnp.float32
BF16 = jnp.bfloat16

D_MODEL = 2048
DEPTH = 2
EPS = 1e-6
GROUP_WIDTH = D_MODEL // 4
CHUNK = 64
SUB = 16
N_SUB = CHUNK // SUB
GLA_HEADS = 4
GLA_DV = GROUP_WIDTH // GLA_HEADS
GLA_DK = GLA_DV // 2
GLA_GATE_RANK = 16
GLA_GATE_NORMALIZER = 16.0
RET_HEADS = 4
RET_DK = GROUP_WIDTH // RET_HEADS
RET_ROT_BASE = 10000.0
SWA_Q_HEADS = 8
SWA_KV_HEADS = 2
SWA_HD = GROUP_WIDTH // SWA_Q_HEADS
WINDOW = 128
ROPE_THETA = 500000.0
ROPE_DIM = SWA_HD // 4
HG_HEADS = 4
HG_DK = GROUP_WIDTH // HG_HEADS
D_FF = ((8 * D_MODEL + 3 * 256 - 1) // (3 * 256)) * 256

LANES = 128
VMEM_LIMIT = 48 * 1024 * 1024
VMEM_LIMIT_IN_PROJ = 56 * 1024 * 1024

_SRC = dict(a_q=0, a_k=256, a_v=512, a_g=1024, a_r=1536,
            b_q=1552, b_k=2064, b_v=2576, b_g=3088,
            c_q=3600, c_k=4112, c_v=4240,
            d_q=4368, d_f=4880, d_i=5392, d_g=5904)
_DST_ORDER = (("a_q", 256), ("a_k", 256), ("a_v", 512), ("a_g", 512), ("c_q", 512),
              ("b_q", 512), ("b_k", 512), ("b_v", 512), ("b_g", 512),
              ("d_q", 512), ("d_f", 512), ("d_i", 512), ("d_g", 512),
              ("c_k", 128), ("c_v", 128), ("a_r", GLA_GATE_RANK))
PROJ_WIDTH = 6528
PROJ_TN = PROJ_WIDTH // 3


def _dst_offsets():
    offs, o = {}, 0
    for name, w in _DST_ORDER:
        offs[name] = o
        o += w
    return offs


_DST = _dst_offsets()


def _dot(a, b):
    return jnp.dot(a, b, preferred_element_type=F32)


def _dot_nt(a, b):
    return lax.dot_general(a, b, (((1,), (1,)), ((), ())), preferred_element_type=F32)


def _dot_tn(a, b):
    return lax.dot_general(a, b, (((0,), (0,)), ((), ())), preferred_element_type=F32)


def _split_bf16(x):
    hi = x.astype(BF16)
    lo = (x - hi.astype(F32)).astype(BF16)
    return hi, lo


def _dot3(a, b):
    a_hi, a_lo = _split_bf16(a)
    b_hi, b_lo = _split_bf16(b)
    return _dot(a_hi, b_hi) + (_dot(a_hi, b_lo) + _dot(a_lo, b_hi))


def _sigmoid(x):
    return 1.0 / (1.0 + jnp.exp(-x))


def _silu(x):
    return (0.5 * x) * (1.0 + jnp.tanh(0.5 * x))


def _log_sigmoid(x):
    return jnp.minimum(x, 0.0) - jnp.log1p(jnp.exp(-jnp.abs(x)))


def _f32(x):
    return x.astype(F32)


def _rms(x):
    return x * lax.rsqrt(jnp.mean(x * x, axis=-1, keepdims=True) + EPS)


def _ada_kernel(c_ref, w_ref, b_ref, o_ref):
    cond = _silu(c_ref[...])
    o_ref[0] = _dot3(cond, w_ref[0]) + b_ref[0]


def _ada_modulation(c, w_ada, b_ada):
    depth, d, n = w_ada.shape
    rows = 8
    c_pad = jnp.zeros((rows, d), F32).at[:c.shape[0]].set(c)
    tn = 2048
    out = pl.pallas_call(
        _ada_kernel,
        out_shape=jax.ShapeDtypeStruct((depth, rows, n), F32),
        grid=(depth, n // tn),
        in_specs=[pl.BlockSpec((rows, d), lambda l, j: (0, 0)),
                  pl.BlockSpec((1, d, tn), lambda l, j: (l, 0, j)),
                  pl.BlockSpec((1, 1, tn), lambda l, j: (l, 0, j))],
        out_specs=pl.BlockSpec((1, rows, tn), lambda l, j: (l, 0, j)),
        compiler_params=pltpu.CompilerParams(
            dimension_semantics=("arbitrary", "arbitrary"), vmem_limit_bytes=VMEM_LIMIT),
        name="ada_modulation",
    )(c_pad, w_ada, b_ada.reshape(depth, 1, n))
    return out[:, :c.shape[0]]


def _modulated_norm(x, nw, shift, scale):
    return _rms(x) * (nw * (1.0 + scale)) + shift


def _in_proj_kernel(x_ref, mod_ref, nw_ref, w_ref, o_ref, h_ref):
    @pl.when(pl.program_id(1) == 0)
    def _():
        m = mod_ref[0]
        h = _modulated_norm(x_ref[...], nw_ref[...], m[0:1], m[1:2])
        h_ref[...] = h.astype(BF16)

    half = h_ref.shape[0] // 2
    for r0 in (0, half):
        o_ref[r0:r0 + half, :] = _dot_nt(h_ref[r0:r0 + half, :], w_ref[...]).astype(BF16)


def _in_proj(x2, mod_l, nw, w_perm, layer, seq, tm=1024):
    m_rows, d = x2.shape
    n = w_perm.shape[1]
    per_b = seq // tm
    return pl.pallas_call(
        _in_proj_kernel,
        out_shape=jax.ShapeDtypeStruct((m_rows, n), BF16),
        grid=(m_rows // tm, n // PROJ_TN),
        in_specs=[pl.BlockSpec((tm, d), lambda i, j: (i, 0)),
                  pl.BlockSpec((1, 6, d), lambda i, j: (i // per_b, 0, 0)),
                  pl.BlockSpec((1, d), lambda i, j: (0, 0)),
                  pl.BlockSpec((None, PROJ_TN, d), lambda i, j: (layer, j, 0))],
        out_specs=pl.BlockSpec((tm, PROJ_TN), lambda i, j: (i, j)),
        scratch_shapes=[pltpu.VMEM((tm, d), BF16)],
        compiler_params=pltpu.CompilerParams(
            dimension_semantics=("arbitrary", "arbitrary"), vmem_limit_bytes=VMEM_LIMIT_IN_PROJ),
        name="in_proj",
    )(x2, mod_l, nw.reshape(1, d), w_perm)


TILE = 2 * CHUNK
GLA_BLOCK = CHUNK
HG_BLOCK = CHUNK // 2
CHUNK_SHIFT = CHUNK.bit_length() - 1
SUB_SHIFT = SUB.bit_length() - 1
SUBLANES = 8
SLAB = 64
DIAG_RUN = 4
assert DIAG_RUN == 4


def _ones_where(mask):
    return jnp.where(mask, 1.0, 0.0).astype(BF16)


def _levels(block):
    return [hs for hs in (CHUNK // 2, CHUNK // 4) if hs >= block]


def _gate_constants(block):
    i = lax.broadcasted_iota(jnp.int32, (TILE, TILE), 0)
    m = lax.broadcasted_iota(jnp.int32, (TILE, TILE), 1)
    same_chunk = (i >> CHUNK_SHIFT) == (m >> CHUNK_SHIFT)
    sub_end = i | (SUB - 1)
    chunk_start = i & ~(CHUNK - 1)
    after = same_chunk & (m > i)
    mats = [same_chunk & (m <= i),
            after & (m <= sub_end),
            after]
    for j in range(N_SUB - 1):
        mats.append(same_chunk & (m > chunk_start + (SUB * j + SUB - 1)) & (m <= i))
    cum = jnp.concatenate([_ones_where(x) for x in mats], axis=0)
    place = _ones_where(m == ((TILE - (i >> 3)) & (TILE - 1)))
    diag_ok = ((i >> SUB_SHIFT) == (m >> SUB_SHIFT)) & (m <= i)
    off_ok = same_chunk & ((m >> SUB_SHIFT) < (i >> SUB_SHIFT))
    sub_of_row = (lax.broadcasted_iota(jnp.int32, (TILE, 1), 0) >> SUB_SHIFT) & (N_SUB - 1)
    score_masks = []
    for hs in _levels(block):
        same = (i >> (2 * hs).bit_length() - 1) == (m >> (2 * hs).bit_length() - 1)
        score_masks.append(same & ((i & (2 * hs - 1)) >= hs) & ((m & (2 * hs - 1)) < hs))
    score_masks.append(((i >> block.bit_length() - 1) == (m >> block.bit_length() - 1)) & (m <= i))
    return cum, place, diag_ok, off_ok, sub_of_row, score_masks


def _transpose_tiles(x):
    return jnp.concatenate([x[:, t:t + LANES].T for t in range(0, x.shape[1], LANES)], axis=0)


def _diag_partial_sums(q_t, k_t, g_t):
    dk = q_t.shape[0]
    slab = min(dk, SLAB)
    accs = [None] * SUB
    for s0 in range(0, dk, slab):
        qs = q_t[s0:s0 + slab]
        ks = k_t[s0:s0 + slab]
        g1 = g_t[s0:s0 + slab]
        g2 = g1 * pltpu.roll(g1, 1, axis=1)
        g4 = g2 * pltpu.roll(g2, 2, axis=1)
        g4_back = [g4] + [pltpu.roll(g4, DIAG_RUN * n, axis=1) for n in range(1, SUB // DIAG_RUN - 1)]
        for d0 in range(0, SUB, DIAG_RUN):
            w = ks
            if d0 > 0:
                w = pltpu.roll(ks, d0, axis=1)
                for n in range(d0 // DIAG_RUN):
                    w = w * g4_back[n]
            for d in range(d0, d0 + DIAG_RUN):
                if d > d0:
                    w = pltpu.roll(w, 1, axis=1) * g1
                p = jnp.sum((qs * w).reshape(slab // SUBLANES, SUBLANES, TILE), axis=0)
                accs[d] = p if accs[d] is None else accs[d] + p
    return jnp.concatenate(accs, axis=0)


def _split3_bf16(x):
    hi = x.astype(BF16)
    rest = x - hi.astype(F32)
    mid = rest.astype(BF16)
    lo = (rest - mid.astype(F32)).astype(BF16)
    return hi, mid, lo


def _cum_dot(mat, pieces):
    hi, mid, lo = pieces
    return _dot(mat, hi) + (_dot(mat, mid) + _dot(mat, lo))


def _head_lanes(h, dk):
    tile = (h * dk) // LANES
    if dk >= LANES:
        return tile, None
    lane = lax.broadcasted_iota(jnp.int32, (1, LANES), 1)
    start = (h * dk) % LANES
    return tile, (lane >= start) & (lane < start + dk)


def _gated_tile_robust(q, k, lg, lg_pieces, v, st_ref, consts, n_heads, dk, dv):
    cum, place, diag_ok, off_ok, sub_of_row, _ = consts
    sums = _cum_dot(cum, lg_pieces)
    cb = sums[0:TILE]
    q_state = (q * jnp.exp(cb)).astype(BF16)
    k_end = k * jnp.exp(sums[TILE:2 * TILE])
    k_last = (k * jnp.exp(sums[2 * TILE:3 * TILE])).astype(BF16)
    q_off = [(q * jnp.exp(sums[(3 + j) * TILE:(4 + j) * TILE])).astype(BF16) for j in range(N_SUB - 1)]
    k_off = [jnp.where(sub_of_row == j, k_end, 0.0).astype(BF16) for j in range(N_SUB - 1)]
    q_t = _transpose_tiles(q)
    k_t = _transpose_tiles(k)
    g_t = _transpose_tiles(jnp.exp(lg))
    outs = []
    for h in range(n_heads):
        ks = slice(h * dk, (h + 1) * dk)
        tile = (h * dk) // LANES
        ls = slice((h * dk) % LANES, (h * dk) % LANES + dk)
        vb = v[:, h * dv:(h + 1) * dv].astype(BF16)
        a_off = _dot_nt(q_off[0][:, ks], k_off[0][:, ks])
        for j in range(1, N_SUB - 1):
            a_off += _dot_nt(q_off[j][:, ks], k_off[j][:, ks])
        part = _diag_partial_sums(q_t[ks], k_t[ks], g_t[ks])
        p_hi, p_lo = _split_bf16(part)
        skew = _dot_tn(p_hi, place) + _dot_tn(p_lo, place)
        a_diag = pltpu.roll(skew, 0, axis=1, stride=1, stride_axis=0)
        a = jnp.where(diag_ok, a_diag, jnp.where(off_ok, a_off, 0.0)).astype(BF16)
        o = _dot(a, vb)
        st = st_ref[tile][:, ls]
        inter = []
        for c in range(TILE // CHUNK):
            rs = slice(c * CHUNK, (c + 1) * CHUNK)
            inter.append(_dot_nt(q_state[rs, ks], st.astype(BF16)))
            last = cb[c * CHUNK + CHUNK - 1:(c + 1) * CHUNK, ks]
            st = st * jnp.exp(last) + _dot_tn(vb[rs], k_last[rs, ks])
        st_ref[tile, :, ls] = st
        outs.append(o + jnp.concatenate(inter, axis=0))
    return outs


def _rows_from(cb, row, spans):
    out = None
    for lo, src in spans:
        val = jnp.zeros_like(cb[0:1]) if src is None else cb[src:src + 1]
        out = val if out is None else jnp.where(row >= lo, val, out)
    return out


def _block_prefix(cb, block):
    if block == CHUNK:
        return cb
    row = lax.broadcasted_iota(jnp.int32, (TILE, 1), 0)
    spans = [(b0, None if b0 % CHUNK == 0 else b0 - 1) for b0 in range(0, TILE, block)]
    return cb - _rows_from(cb, row, spans)


def _gated_tile_bounded(q, k, cb, pb, v, st_ref, consts, n_heads, dk, dv, block):
    score_masks = consts[-1]
    row = lax.broadcasted_iota(jnp.int32, (TILE, 1), 0)
    ends = [cb[c * CHUNK + CHUNK - 1:(c + 1) * CHUNK, :] for c in range(TILE // CHUNK)]
    end_of_row = _rows_from(cb, row, [(c * CHUNK, c * CHUNK + CHUNK - 1) for c in range(TILE // CHUNK)])
    q_state = q * jnp.exp(cb)
    k_last = k * jnp.exp(end_of_row - cb)
    q_parts, k_parts = [], []
    for hs in _levels(block):
        ref = _rows_from(cb, row, [(b0, b0 + hs - 1) for b0 in range(0, TILE, 2 * hs)])
        lower = (row & (2 * hs - 1)) < hs
        x = jnp.exp(jnp.where(lower, ref - cb, cb - ref))
        q_parts.append(jnp.where(lower, 0.0, q * x))
        k_parts.append(jnp.where(lower, k * x, 0.0).astype(BF16))
    q_parts.append(q_state if block == CHUNK else q * jnp.exp(pb))
    k_parts.append((k * jnp.exp(-pb)).astype(BF16))
    outs = [None] * n_heads
    heads_per_tile = max(LANES // dk, 1)
    for tile in range(n_heads * dk // LANES):
        ls = slice(tile * LANES, (tile + 1) * LANES)
        heads = range(tile * heads_per_tile, (tile + 1) * heads_per_tile)
        qh, kl, vb, intra = {}, {}, {}, {}
        for h in heads:
            _, mask = _head_lanes(h, dk)
            own = (lambda a: a) if mask is None else (lambda a: jnp.where(mask, a, 0.0))
            qh[h] = own(q_state[:, ls]).astype(BF16)
            kl[h] = own(k_last[:, ls]).astype(BF16)
            vb[h] = v[:, h * dv:(h + 1) * dv].astype(BF16)
            a = 0.0
            for qp, kp, ok in zip(q_parts, k_parts, score_masks):
                a = jnp.where(ok, _dot_nt(own(qp[:, ls]).astype(BF16), kp[:, ls]), a)
            intra[h] = _dot(a.astype(BF16), vb[h])
        st = st_ref[tile]
        inter = {h: [] for h in heads}
        for c in range(TILE // CHUNK):
            rs = slice(c * CHUNK, (c + 1) * CHUNK)
            stb = st.astype(BF16)
            upd = None
            for h in heads:
                inter[h].append(_dot_nt(qh[h][rs], stb))
                u = _dot_tn(vb[h][rs], kl[h][rs])
                upd = u if upd is None else upd + u
            st = st * jnp.exp(ends[c][:, ls]) + upd
        st_ref[tile] = st
        for h in heads:
            outs[h] = intra[h] + jnp.concatenate(inter[h], axis=0)
    return outs


SAFE_DECAY = 64.0


def _gated_tile(q, k, lg, v, st_ref, consts, n_heads, dk, dv, block, finish):
    pieces = _split3_bf16(lg)
    cb = _cum_dot(consts[0][0:TILE], pieces)
    pb = _block_prefix(cb, block)
    bounded = jnp.min(pb) >= -SAFE_DECAY

    @pl.when(bounded)
    def _():
        finish(_gated_tile_bounded(q, k, cb, pb, v, st_ref, consts, n_heads, dk, dv, block))

    @pl.when(jnp.logical_not(bounded))
    def _():
        finish(_gated_tile_robust(q, k, lg, pieces, v, st_ref, consts, n_heads, dk, dv))


def _gated_step(lg_of, qkv_of, st_ref, cb_ref, consts, n_heads, dk, dv, block, n_tiles, finish):
    worst = None
    for c in range(n_tiles):
        rows = slice(c * TILE, (c + 1) * TILE)
        cb = _cum_dot(consts[0][0:TILE], _split3_bf16(lg_of(rows)))
        cb_ref[rows, :] = cb
        pb = _block_prefix(cb, block)
        worst = pb if worst is None else jnp.minimum(worst, pb)
    all_bounded = jnp.min(worst) >= -SAFE_DECAY

    def tile_rows(c):
        return pl.ds(pl.multiple_of(c * TILE, TILE), TILE)

    @pl.when(all_bounded)
    def _():
        def body(c, carry):
            rows = tile_rows(c)
            q, k, v = qkv_of(rows)
            cb = cb_ref[rows, :]
            pb = _block_prefix(cb, block)
            finish(rows, _gated_tile_bounded(q, k, cb, pb, v, st_ref, consts, n_heads, dk, dv, block))
            return carry

        lax.fori_loop(0, n_tiles, body, 0)

    @pl.when(jnp.logical_not(all_bounded))
    def _():
        def body(c, carry):
            rows = tile_rows(c)
            q, k, v = qkv_of(rows)
            _gated_tile(q, k, lg_of(rows), v, st_ref, consts, n_heads, dk, dv, block,
                        functools.partial(finish, rows))
            return carry

        lax.fori_loop(0, n_tiles, body, 0)


def _gla_kernel(q_ref, k_ref, v_ref, g_ref, r_ref, w2_ref, b2_ref, nw_ref, o_ref, st_ref, cb_ref, *, n_tiles):
    @pl.when(pl.program_id(1) == 0)
    def _():
        st_ref[...] = jnp.zeros_like(st_ref)

    consts = _gate_constants(GLA_BLOCK)
    w2 = w2_ref[...]
    b2 = b2_ref[...]
    nw = nw_ref[...]

    def lg_of(rows):
        pre = _dot3(_f32(r_ref[rows, :]), w2) + b2
        return _log_sigmoid(pre) * (1.0 / GLA_GATE_NORMALIZER)

    def qkv_of(rows):
        return _f32(q_ref[rows, :]) * (GLA_DK ** -0.5), _f32(k_ref[rows, :]), _f32(v_ref[rows, :])

    def finish(rows, outs):
        g = _f32(g_ref[rows, :])
        for h, o in enumerate(outs):
            vs = slice(h * GLA_DV, (h + 1) * GLA_DV)
            o_ref[rows, vs] = ((_rms(o) * nw) * _silu(g[:, vs])).astype(BF16)

    _gated_step(lg_of, qkv_of, st_ref, cb_ref, consts, GLA_HEADS, GLA_DK, GLA_DV, GLA_BLOCK, n_tiles, finish)


def _hgrn_kernel(q_ref, f_ref, i_ref, g_ref, lb_ref, nw_ref, o_ref, st_ref, cb_ref, *, n_tiles, layer):
    @pl.when(pl.program_id(1) == 0)
    def _():
        st_ref[...] = jnp.zeros_like(st_ref)

    consts = _gate_constants(HG_BLOCK)
    lbv = lb_ref[...]
    e = jnp.exp(lbv - jnp.max(lbv, axis=0, keepdims=True))
    soft = e / jnp.sum(e, axis=0, keepdims=True)
    cum = soft[0:1]
    for i in range(1, layer + 1):
        cum = cum + soft[i:i + 1]
    lb = cum - soft[0:1]
    nw = nw_ref[...]

    def forget(rows):
        return lb + (1.0 - lb) * _sigmoid(_f32(f_ref[rows, :]))

    def lg_of(rows):
        return jnp.log(forget(rows))

    def qkv_of(rows):
        q = _silu(_f32(q_ref[rows, :])) * (HG_DK ** -0.5)
        return q, 1.0 - forget(rows), _f32(i_ref[rows, :])

    def finish(rows, outs):
        g = _f32(g_ref[rows, :])
        for h, o in enumerate(outs):
            hs = slice(h * HG_DK, (h + 1) * HG_DK)
            o_ref[rows, hs] = ((_rms(o) * nw) * _silu(g[:, hs])).astype(BF16)

    _gated_step(lg_of, qkv_of, st_ref, cb_ref, consts, HG_HEADS, HG_DK, HG_DK, HG_BLOCK, n_tiles, finish)


def _col_spec(t, width, name, nt):
    blk = _DST[name] // width
    return pl.BlockSpec((t, width), lambda b, i: (b * nt + i, blk))


def _mixer_out_spec(t, nt):
    return pl.BlockSpec((t, GROUP_WIDTH), lambda b, i: (b * nt + i, 0))


def _mixer_params():
    return pltpu.CompilerParams(dimension_semantics=("arbitrary", "arbitrary"),
                                vmem_limit_bytes=VMEM_LIMIT)


def _gla(proj, w2_pad, b2, nw, batch, seq, t=512):
    nt = seq // t
    full = lambda shape: pl.BlockSpec(shape, lambda b, i: (0, 0))
    return pl.pallas_call(
        functools.partial(_gla_kernel, n_tiles=t // TILE),
        out_shape=jax.ShapeDtypeStruct((batch * seq, GROUP_WIDTH), BF16),
        grid=(batch, nt),
        in_specs=[_col_spec(t, 256, "a_q", nt), _col_spec(t, 256, "a_k", nt),
                  _col_spec(t, 512, "a_v", nt), _col_spec(t, 512, "a_g", nt),
                  _col_spec(t, LANES, "a_r", nt),
                  full((LANES, GLA_HEADS * GLA_DK)), full((1, GLA_HEADS * GLA_DK)), full((1, GLA_DV))],
        out_specs=_mixer_out_spec(t, nt),
        scratch_shapes=[pltpu.VMEM((GLA_HEADS * GLA_DK // LANES, GLA_DV, LANES), F32),
                        pltpu.VMEM((t, GLA_HEADS * GLA_DK), F32)],
        compiler_params=_mixer_params(),
        name="gla_mixer",
    )(proj, proj, proj, proj, proj, w2_pad, b2.reshape(1, -1), nw.reshape(1, -1))


def _hgrn(proj, hgrn_lb, nw, layer, batch, seq, t=512):
    nt = seq // t
    full = lambda shape: pl.BlockSpec(shape, lambda b, i: (0, 0))
    return pl.pallas_call(
        functools.partial(_hgrn_kernel, n_tiles=t // TILE, layer=layer),
        out_shape=jax.ShapeDtypeStruct((batch * seq, GROUP_WIDTH), BF16),
        grid=(batch, nt),
        in_specs=[_col_spec(t, 512, "d_q", nt), _col_spec(t, 512, "d_f", nt),
                  _col_spec(t, 512, "d_i", nt), _col_spec(t, 512, "d_g", nt),
                  full(hgrn_lb.shape), full((1, HG_DK))],
        out_specs=_mixer_out_spec(t, nt),
        scratch_shapes=[pltpu.VMEM((HG_HEADS * HG_DK // LANES, HG_DK, LANES), F32),
                        pltpu.VMEM((t, HG_HEADS * HG_DK), F32)],
        compiler_params=_mixer_params(),
        name="hgrn_mixer",
    )(proj, proj, proj, proj, hgrn_lb, nw.reshape(1, -1))


def _ret_kernel(q_ref, k_ref, v_ref, g_ref, pos_ref, inv_ref, o_ref, st_ref, *, t):
    @pl.when(pl.program_id(1) == 0)
    def _():
        st_ref[...] = jnp.zeros_like(st_ref)

    ang = pos_ref[0] * inv_ref[...]
    lane = lax.broadcasted_iota(jnp.int32, (1, RET_DK), 1)
    cos = jnp.cos(ang)
    sin = jnp.where(lane < RET_DK // 2, -jnp.sin(ang), jnp.sin(ang))
    q = _f32(q_ref[...])
    k = _f32(k_ref[...])
    v = v_ref[...]
    g = _f32(g_ref[...])
    rowm = lax.broadcasted_iota(jnp.int32, (t, t), 0)
    colm = lax.broadcasted_iota(jnp.int32, (t, t), 1)
    rel = (rowm - colm).astype(F32)
    pos = lax.broadcasted_iota(jnp.int32, (t, RET_DK), 0).astype(F32)
    half = RET_DK // 2
    for h in range(RET_HEADS):
        lg = math.log1p(-(2.0 ** (-5.0 - h)))
        hs = slice(h * RET_DK, (h + 1) * RET_DK)
        qh = q[:, hs]
        kh = k[:, hs]
        qr = qh * cos + pltpu.roll(qh, half, axis=1) * sin
        kr = (kh * cos + pltpu.roll(kh, half, axis=1) * sin) * (RET_DK ** -0.5)
        vb = v[:, hs].astype(BF16)
        dmask = jnp.where(rel >= 0, jnp.exp(jnp.minimum(lg * rel, 0.0)), 0.0)
        scores = _dot_nt(qr.astype(BF16), kr.astype(BF16)) * dmask
        st = st_ref[h]
        xi = jnp.exp(lg * (pos + 1.0))
        o = _dot(scores.astype(BF16), vb) + _dot((qr * xi).astype(BF16), st.astype(BF16))
        zeta = jnp.exp(lg * (t - 1.0 - pos))
        st_ref[h] = st * math.exp(lg * t) + _dot_tn((kr * zeta).astype(BF16), vb)
        o_ref[:, hs] = (_rms(o) * _silu(g[:, hs])).astype(BF16)


def _ret(proj, pos, inv, batch, seq, t=256):
    nt = seq // t
    pos_spec = pl.BlockSpec((1, t, 1), lambda b, i: (b, i, 0))
    inv_spec = pl.BlockSpec((1, RET_DK), lambda b, i: (0, 0))
    return pl.pallas_call(
        functools.partial(_ret_kernel, t=t),
        out_shape=jax.ShapeDtypeStruct((batch * seq, GROUP_WIDTH), BF16),
        grid=(batch, nt),
        in_specs=[_col_spec(t, 512, "b_q", nt), _col_spec(t, 512, "b_k", nt),
                  _col_spec(t, 512, "b_v", nt), _col_spec(t, 512, "b_g", nt), pos_spec, inv_spec],
        out_specs=_mixer_out_spec(t, nt),
        scratch_shapes=[pltpu.VMEM((RET_HEADS, RET_DK, RET_DK), F32)],
        compiler_params=_mixer_params(),
        name="ret_mixer",
    )(proj, proj, proj, proj, pos, inv)


def _rope_partial(x, c, s_lo, s_hi):
    n = x.shape[-1]
    half = ROPE_DIM // 2
    return x * c + pltpu.roll(x, n - half, axis=1) * s_lo + pltpu.roll(x, half, axis=1) * s_hi


def _swa_window(q_of, kc_raw, vc, pos, inv, has_prev, sinks, kprev_ref, vprev_ref, store):
    w = WINDOW
    lane = lax.broadcasted_iota(jnp.int32, (1, LANES), 1)
    lo_half = lane < SWA_HD
    ang = pos * inv
    in_head = lane & (SWA_HD - 1)
    c = jnp.cos(ang)
    s_lo = jnp.where(in_head < ROPE_DIM // 2, -jnp.sin(ang), 0.0)
    s_hi = jnp.where((in_head >= ROPE_DIM // 2) & (in_head < ROPE_DIM), jnp.sin(ang), 0.0)
    kc = _rope_partial(kc_raw, c, s_lo, s_hi)
    group = SWA_Q_HEADS // SWA_KV_HEADS
    rowm = lax.broadcasted_iota(jnp.int32, (group * w, w), 0) & (w - 1)
    colm = lax.broadcasted_iota(jnp.int32, (group * w, w), 1)
    cur_ok = colm <= rowm
    prev_ok = (colm > rowm) & has_prev

    def both_halves(a, kv):
        swapped = pltpu.roll(a, SWA_HD, axis=1)
        return jnp.where(lo_half, a, swapped) if kv == 0 else jnp.where(lo_half, swapped, a)

    for kv in range(SWA_KV_HEADS):
        k2c = both_halves(kc, kv).astype(BF16)
        k2p = kprev_ref[kv]
        v2c = both_halves(vc, kv)
        v_half = [(jnp.where(lo_half, v2c, jnp.where(lane == SWA_HD, 1.0, 0.0)).astype(BF16), vprev_ref[kv, 0]),
                  (jnp.where(lo_half, jnp.where(lane == 0, 1.0, 0.0), v2c).astype(BF16), vprev_ref[kv, 1])]
        ones_lane = (SWA_HD, 0)
        kprev_ref[kv] = k2c
        vprev_ref[kv, 0] = v_half[0][0]
        vprev_ref[kv, 1] = v_half[1][0]
        q_rows, sink_rows = [], []
        for t in range(kv * group // 2, (kv + 1) * group // 2):
            qt = _rope_partial(q_of(t), c, s_lo, s_hi) * (SWA_HD ** -0.5)
            q_rows += [jnp.where(lo_half, qt, 0.0), jnp.where(lo_half, 0.0, qt)]
            sink_rows += [jnp.broadcast_to(sinks[:, 2 * t:2 * t + 1], (w, 1)),
                          jnp.broadcast_to(sinks[:, 2 * t + 1:2 * t + 2], (w, 1))]
        q4 = jnp.concatenate(q_rows, axis=0).astype(BF16)
        sink = jnp.concatenate(sink_rows, axis=0)
        s_c = jnp.where(cur_ok, _dot_nt(q4, k2c), -jnp.inf)
        s_p = jnp.where(prev_ok, _dot_nt(q4, k2p), -jnp.inf)
        m = jnp.maximum(jnp.max(jnp.maximum(s_c, s_p), axis=-1, keepdims=True), sink)
        e_c = jnp.exp(s_c - m).astype(BF16)
        e_p = jnp.exp(s_p - m).astype(BF16)
        e_sink = jnp.exp(sink - m)
        for n, t in enumerate(range(kv * group // 2, (kv + 1) * group // 2)):
            halves = []
            for half in range(2):
                rs = slice((2 * n + half) * w, (2 * n + half + 1) * w)
                vh_c, vh_p = v_half[half]
                o = _dot(e_c[rs], vh_c) + _dot(e_p[rs], vh_p)
                denom = o[:, ones_lane[half]:ones_lane[half] + 1] + e_sink[rs]
                halves.append(o * (1.0 / denom))
            store(t, jnp.where(lo_half, halves[0], halves[1]).astype(BF16))


def _swa_kernel(q_ref, kc_ref, vc_ref, pos_ref, inv_ref, sink_ref, o_ref, kprev_ref, vprev_ref, *, t):
    first = pl.program_id(1) == 0

    @pl.when(first)
    def _():
        kprev_ref[...] = jnp.zeros_like(kprev_ref)
        vprev_ref[...] = jnp.zeros_like(vprev_ref)

    sinks = sink_ref[...]
    inv = inv_ref[...]
    for i in range(t // WINDOW):
        rows = slice(i * WINDOW, (i + 1) * WINDOW)
        has_prev = jnp.logical_not(first) if i == 0 else True

        def store(tile, value, rows=rows):
            o_ref[rows, tile * LANES:(tile + 1) * LANES] = value

        _swa_window(lambda tile, rows=rows: _f32(q_ref[rows, tile * LANES:(tile + 1) * LANES]),
                    _f32(kc_ref[rows, :]), _f32(vc_ref[rows, :]), pos_ref[0, rows], inv, has_prev, sinks,
                    kprev_ref, vprev_ref, store)


def _swa(proj, pos, inv, sinks, batch, seq, t=512):
    nt = seq // t
    kv_w = SWA_KV_HEADS * SWA_HD
    cur = lambda name: pl.BlockSpec((t, kv_w), lambda b, i: (b * nt + i, _DST[name] // kv_w))
    pos_spec = pl.BlockSpec((1, t, 1), lambda b, i: (b, i, 0))
    inv_spec = pl.BlockSpec((1, LANES), lambda b, i: (0, 0))
    return pl.pallas_call(
        functools.partial(_swa_kernel, t=t),
        out_shape=jax.ShapeDtypeStruct((batch * seq, GROUP_WIDTH), BF16),
        grid=(batch, nt),
        in_specs=[_col_spec(t, 512, "c_q", nt), cur("c_k"), cur("c_v"), pos_spec, inv_spec,
                  pl.BlockSpec((1, SWA_Q_HEADS), lambda b, i: (0, 0))],
        out_specs=_mixer_out_spec(t, nt),
        scratch_shapes=[pltpu.VMEM((SWA_KV_HEADS, WINDOW, LANES), BF16),
                        pltpu.VMEM((SWA_KV_HEADS, 2, WINDOW, LANES), BF16)],
        compiler_params=_mixer_params(),
        name="swa_mixer",
    )(proj, proj, proj, pos, inv, sinks.reshape(1, -1))


def _out_proj_kernel(x_ref, mod_ref, a_ref, b_ref, c_ref, d_ref, w_ref, o_ref):
    gw = GROUP_WIDTH
    acc = _dot(a_ref[...], w_ref[0:gw, :])
    acc += _dot(b_ref[...], w_ref[gw:2 * gw, :])
    acc += _dot(c_ref[...], w_ref[2 * gw:3 * gw, :])
    acc += _dot(d_ref[...], w_ref[3 * gw:4 * gw, :])
    gate = mod_ref[0][2:3]
    o_ref[...] = x_ref[...] + gate * acc


def _out_proj(x2, mod_l, mixed, w_out, layer, seq, tm=512):
    m_rows, d = x2.shape
    per_b = seq // tm
    grp = pl.BlockSpec((tm, GROUP_WIDTH), lambda i: (i, 0))
    return pl.pallas_call(
        _out_proj_kernel,
        out_shape=jax.ShapeDtypeStruct((m_rows, d), F32),
        grid=(m_rows // tm,),
        in_specs=[pl.BlockSpec((tm, d), lambda i: (i, 0)),
                  pl.BlockSpec((1, 6, d), lambda i: (i // per_b, 0, 0)),
                  grp, grp, grp, grp,
                  pl.BlockSpec((None, d, d), lambda i: (layer, 0, 0))],
        out_specs=pl.BlockSpec((tm, d), lambda i: (i, 0)),
        compiler_params=pltpu.CompilerParams(
            dimension_semantics=("arbitrary",), vmem_limit_bytes=VMEM_LIMIT),
        name="out_proj",
    )(x2, mod_l, *mixed, w_out)


def _ffn_kernel(x_ref, mod_ref, nw_ref, wg_ref, wu_ref, wd_ref, fw_ref, o_ref, h_ref, acc_ref, *, final):
    f = pl.program_id(1)

    @pl.when(f == 0)
    def _():
        m = mod_ref[0]
        h = _modulated_norm(x_ref[...], nw_ref[...], m[3:4], m[4:5])
        h_ref[...] = h.astype(BF16)
        acc_ref[...] = jnp.zeros_like(acc_ref)

    h = h_ref[...]
    g = _dot(h, wg_ref[...])
    u = _dot(h, wu_ref[...])
    act = (_silu(g) * u).astype(BF16)
    acc_ref[...] += _dot(act, wd_ref[...])

    @pl.when(f == pl.num_programs(1) - 1)
    def _():
        y = x_ref[...] + mod_ref[0][5:6] * acc_ref[...]
        o_ref[...] = _rms(y) * fw_ref[...] if final else y


def _ffn(x2, mod_l, nw, w_in, w_down, layer, final_w, final, seq, tm=512, tf=512):
    m_rows, d = x2.shape
    nf = D_FF // tf
    per_b = seq // tm
    return pl.pallas_call(
        functools.partial(_ffn_kernel, final=final),
        out_shape=jax.ShapeDtypeStruct((m_rows, d), F32),
        grid=(m_rows // tm, nf),
        in_specs=[pl.BlockSpec((tm, d), lambda i, f: (i, 0)),
                  pl.BlockSpec((1, 6, d), lambda i, f: (i // per_b, 0, 0)),
                  pl.BlockSpec((1, d), lambda i, f: (0, 0)),
                  pl.BlockSpec((None, d, tf), lambda i, f: (layer, 0, f)),
                  pl.BlockSpec((None, d, tf), lambda i, f: (layer, 0, nf + f)),
                  pl.BlockSpec((None, tf, d), lambda i, f: (layer, f, 0)),
                  pl.BlockSpec((1, d), lambda i, f: (0, 0))],
        out_specs=pl.BlockSpec((tm, d), lambda i, f: (i, 0)),
        scratch_shapes=[pltpu.VMEM((tm, d), BF16), pltpu.VMEM((tm, d), F32)],
        compiler_params=pltpu.CompilerParams(
            dimension_semantics=("arbitrary", "arbitrary"), vmem_limit_bytes=VMEM_LIMIT),
        name="ffn",
    )(x2, mod_l, nw.reshape(1, d), w_in, w_in, w_down, final_w.reshape(1, d))


CAST_ROWS = 256
CAST_COLS = 2816


def _cast_kernel(x_ref, o_ref):
    o_ref[...] = x_ref[...].astype(BF16)


def _to_bf16(w):
    depth, k, n = w.shape
    tn = n if n <= CAST_COLS else CAST_COLS
    blk = pl.BlockSpec((1, CAST_ROWS, tn), lambda l, i, j: (l, i, j))
    return pl.pallas_call(
        _cast_kernel,
        out_shape=jax.ShapeDtypeStruct(w.shape, BF16),
        grid=(depth, k // CAST_ROWS, n // tn),
        in_specs=[blk], out_specs=blk,
        compiler_params=pltpu.CompilerParams(
            dimension_semantics=("arbitrary",) * 3, vmem_limit_bytes=VMEM_LIMIT),
        name="cast_weights",
    )(w)


def _permute_kernel(x_ref, o_ref):
    off = 0
    for name, width in _DST_ORDER:
        o_ref[0, off:off + width, :] = x_ref[0, _SRC[name]:_SRC[name] + width, :].astype(BF16)
        off += width
    o_ref[0, off:, :] = jnp.zeros((PROJ_WIDTH - off, o_ref.shape[2]), BF16)


def _permute_w_in(w):
    w_t = jnp.swapaxes(w, 1, 2)
    depth, n, k = w_t.shape
    return pl.pallas_call(
        _permute_kernel,
        out_shape=jax.ShapeDtypeStruct((depth, PROJ_WIDTH, k), BF16),
        grid=(depth, k // CAST_ROWS),
        in_specs=[pl.BlockSpec((1, n, CAST_ROWS), lambda l, i: (l, 0, i))],
        out_specs=pl.BlockSpec((1, PROJ_WIDTH, CAST_ROWS), lambda l, i: (l, 0, i)),
        compiler_params=pltpu.CompilerParams(
            dimension_semantics=("arbitrary", "arbitrary"), vmem_limit_bytes=VMEM_LIMIT),
        name="permute_w_in",
    )(w_t)


def _ret_inv_freq():
    inv = 1.0 / jnp.power(RET_ROT_BASE, jnp.linspace(0.0, 1.0, RET_DK // 2, dtype=F32))
    return jnp.concatenate([inv, inv])[None, :]


def _swa_inv_freq():
    half = ROPE_DIM // 2
    inv = 1.0 / jnp.power(ROPE_THETA, jnp.arange(half, dtype=F32) / half)
    per_head = jnp.concatenate([inv, inv, jnp.zeros((SWA_HD - ROPE_DIM,), F32)])
    return jnp.concatenate([per_head] * (LANES // SWA_HD))[None, :]


def kernel(x, c, positions, w_ada, b_ada, norm1_w, w_in, gla_gate_w2, gla_gate_b2, gla_norm_w, swa_sinks, hgrn_lb, hgrn_norm_w, w_out, norm2_w, w_ffn_in, w_ffn_down, final_norm_w):
    batch, seq, d = x.shape
    depth = w_ada.shape[0]
    mod = _ada_modulation(c, w_ada, b_ada).reshape(depth, batch, 6, d)
    pos = positions.astype(F32)[:, :, None]
    ret_inv, swa_inv = _ret_inv_freq(), _swa_inv_freq()
    x2 = x.reshape(batch * seq, d)
    w_in_b, w_out_b = _permute_w_in(w_in), _to_bf16(w_out)
    w_ffn_in_b, w_ffn_down_b = _to_bf16(w_ffn_in), _to_bf16(w_ffn_down)
    for l in range(depth):
        w2_pad = jnp.zeros((LANES, GLA_HEADS * GLA_DK), F32).at[:GLA_GATE_RANK].set(gla_gate_w2[l])
        proj = _in_proj(x2, mod[l], norm1_w[l], w_in_b, l, seq)
        mixed = (
            _gla(proj, w2_pad, gla_gate_b2[l], gla_norm_w[l], batch, seq),
            _ret(proj, pos, ret_inv, batch, seq),
            _swa(proj, pos, swa_inv, swa_sinks[l], batch, seq),
            _hgrn(proj, hgrn_lb, hgrn_norm_w[l], l, batch, seq),
        )
        x2 = _out_proj(x2, mod[l], mixed, w_out_b, l, seq)
        x2 = _ffn(x2, mod[l], norm2_w[l], w_ffn_in_b, w_ffn_down_b, l, final_norm_w, l == depth - 1, seq)
    return x2.reshape(batch, seq, d)
```

```python
import functools
import math

import jax
import jax.numpy as jnp
from jax import lax
from jax.experimental import pallas as pl
from jax.experimental.pallas import tpu as pltpu

F32 = jnp.float32
BF16 = jnp.bfloat16

D_MODEL = 2048
DEPTH = 2
EPS = 1e-6
GROUP_WIDTH = D_MODEL // 4
CHUNK = 64
SUB = 16
N_SUB = CHUNK // SUB
GLA_HEADS = 4
GLA_DV = GROUP_WIDTH // GLA_HEADS
GLA_DK = GLA_DV // 2
GLA_GATE_RANK = 16
GLA_GATE_NORMALIZER = 16.0
RET_HEADS = 4
RET_DK = GROUP_WIDTH // RET_HEADS
RET_ROT_BASE = 10000.0
SWA_Q_HEADS = 8
SWA_KV_HEADS = 2
SWA_HD = GROUP_WIDTH // SWA_Q_HEADS
WINDOW = 128
ROPE_THETA = 500000.0
ROPE_DIM = SWA_HD // 4
HG_HEADS = 4
HG_DK = GROUP_WIDTH // HG_HEADS
D_FF = ((8 * D_MODEL + 3 * 256 - 1) // (3 * 256)) * 256

LANES = 128
VMEM_LIMIT = 48 * 1024 * 1024
VMEM_LIMIT_IN_PROJ = 56 * 1024 * 1024

_SRC = dict(a_q=0, a_k=256, a_v=512, a_g=1024, a_r=1536,
            b_q=1552, b_k=2064, b_v=2576, b_g=3088,
            c_q=3600, c_k=4112, c_v=4240,
            d_q=4368, d_f=4880, d_i=5392, d_g=5904)
_DST_ORDER = (("a_q", 256), ("a_k", 256), ("a_v", 512), ("a_g", 512), ("c_q", 512),
              ("b_q", 512), ("b_k", 512), ("b_v", 512), ("b_g", 512),
              ("d_q", 512), ("d_f", 512), ("d_i", 512), ("d_g", 512),
              ("c_k", 128), ("c_v", 128), ("a_r", GLA_GATE_RANK))
PROJ_WIDTH = 6528
PROJ_TN = PROJ_WIDTH // 3


def _dst_offsets():
    offs, o = {}, 0
    for name, w in _DST_ORDER:
        offs[name] = o
        o += w
    return offs


_DST = _dst_offsets()


def _dot(a, b):
    return jnp.dot(a, b, preferred_element_type=F32)


def _dot_nt(a, b):
    return lax.dot_general(a, b, (((1,), (1,)), ((), ())), preferred_element_type=F32)


def _dot_tn(a, b):
    return lax.dot_general(a, b, (((0,), (0,)), ((), ())), preferred_element_type=F32)


def _split_bf16(x):
    hi = x.astype(BF16)
    lo = (x - hi.astype(F32)).astype(BF16)
    return hi, lo


def _dot3(a, b):
    a_hi, a_lo = _split_bf16(a)
    b_hi, b_lo = _split_bf16(b)
    return _dot(a_hi, b_hi) + (_dot(a_hi, b_lo) + _dot(a_lo, b_hi))


def _sigmoid(x):
    return 1.0 / (1.0 + jnp.exp(-x))


def _silu(x):
    return (0.5 * x) * (1.0 + jnp.tanh(0.5 * x))


def _log_sigmoid(x):
    return jnp.minimum(x, 0.0) - jnp.log1p(jnp.exp(-jnp.abs(x)))


def _f32(x):
    return x.astype(F32)


def _rms(x):
    return x * lax.rsqrt(jnp.mean(x * x, axis=-1, keepdims=True) + EPS)


def _ada_kernel(c_ref, w_ref, b_ref, o_ref):
    cond = _silu(c_ref[...])
    o_ref[0] = _dot3(cond, w_ref[0]) + b_ref[0]


def _ada_modulation(c, w_ada, b_ada):
    depth, d, n = w_ada.shape
    rows = 8
    c_pad = jnp.zeros((rows, d), F32).at[:c.shape[0]].set(c)
    tn = 2048
    out = pl.pallas_call(
        _ada_kernel,
        out_shape=jax.ShapeDtypeStruct((depth, rows, n), F32),
        grid=(depth, n // tn),
        in_specs=[pl.BlockSpec((rows, d), lambda l, j: (0, 0)),
                  pl.BlockSpec((1, d, tn), lambda l, j: (l, 0, j)),
                  pl.BlockSpec((1, 1, tn), lambda l, j: (l, 0, j))],
        out_specs=pl.BlockSpec((1, rows, tn), lambda l, j: (l, 0, j)),
        compiler_params=pltpu.CompilerParams(
            dimension_semantics=("arbitrary", "arbitrary"), vmem_limit_bytes=VMEM_LIMIT),
        name="ada_modulation",
    )(c_pad, w_ada, b_ada.reshape(depth, 1, n))
    return out[:, :c.shape[0]]


def _modulated_norm(x, nw, shift, scale):
    return _rms(x) * (nw * (1.0 + scale)) + shift


def _in_proj_kernel(x_ref, mod_ref, nw_ref, w_ref, o_ref, h_ref):
    @pl.when(pl.program_id(1) == 0)
    def _():
        m = mod_ref[0]
        h = _modulated_norm(x_ref[...], nw_ref[...], m[0:1], m[1:2])
        h_ref[...] = h.astype(BF16)

    half = h_ref.shape[0] // 2
    for r0 in (0, half):
        o_ref[r0:r0 + half, :] = _dot_nt(h_ref[r0:r0 + half, :], w_ref[...]).astype(BF16)


def _in_proj(x2, mod_l, nw, w_perm, layer, seq, tm=1024):
    m_rows, d = x2.shape
    n = w_perm.shape[1]
    per_b = seq // tm
    return pl.pallas_call(
        _in_proj_kernel,
        out_shape=jax.ShapeDtypeStruct((m_rows, n), BF16),
        grid=(m_rows // tm, n // PROJ_TN),
        in_specs=[pl.BlockSpec((tm, d), lambda i, j: (i, 0)),
                  pl.BlockSpec((1, 6, d), lambda i, j: (i // per_b, 0, 0)),
                  pl.BlockSpec((1, d), lambda i, j: (0, 0)),
                  pl.BlockSpec((None, PROJ_TN, d), lambda i, j: (layer, j, 0))],
        out_specs=pl.BlockSpec((tm, PROJ_TN), lambda i, j: (i, j)),
        scratch_shapes=[pltpu.VMEM((tm, d), BF16)],
        compiler_params=pltpu.CompilerParams(
            dimension_semantics=("arbitrary", "arbitrary"), vmem_limit_bytes=VMEM_LIMIT_IN_PROJ),
        name="in_proj",
    )(x2, mod_l, nw.reshape(1, d), w_perm)


TILE = 2 * CHUNK
GLA_BLOCK = CHUNK
HG_BLOCK = CHUNK // 2
CHUNK_SHIFT = CHUNK.bit_length() - 1
SUB_SHIFT = SUB.bit_length() - 1
SUBLANES = 8
SLAB = 64
DIAG_RUN = 4
assert DIAG_RUN == 4


def _ones_where(mask):
    return jnp.where(mask, 1.0, 0.0).astype(BF16)


def _levels(block):
    return [hs for hs in (CHUNK // 2, CHUNK // 4) if hs >= block]


def _gate_constants(block):
    i = lax.broadcasted_iota(jnp.int32, (TILE, TILE), 0)
    m = lax.broadcasted_iota(jnp.int32, (TILE, TILE), 1)
    same_chunk = (i >> CHUNK_SHIFT) == (m >> CHUNK_SHIFT)
    sub_end = i | (SUB - 1)
    chunk_start = i & ~(CHUNK - 1)
    after = same_chunk & (m > i)
    mats = [same_chunk & (m <= i),
            after & (m <= sub_end),
            after]
    for j in range(N_SUB - 1):
        mats.append(same_chunk & (m > chunk_start + (SUB * j + SUB - 1)) & (m <= i))
    cum = jnp.concatenate([_ones_where(x) for x in mats], axis=0)
    place = _ones_where(m == ((TILE - (i >> 3)) & (TILE - 1)))
    diag_ok = ((i >> SUB_SHIFT) == (m >> SUB_SHIFT)) & (m <= i)
    off_ok = same_chunk & ((m >> SUB_SHIFT) < (i >> SUB_SHIFT))
    sub_of_row = (lax.broadcasted_iota(jnp.int32, (TILE, 1), 0) >> SUB_SHIFT) & (N_SUB - 1)
    score_masks = []
    for hs in _levels(block):
        same = (i >> (2 * hs).bit_length() - 1) == (m >> (2 * hs).bit_length() - 1)
        score_masks.append(same & ((i & (2 * hs - 1)) >= hs) & ((m & (2 * hs - 1)) < hs))
    score_masks.append(((i >> block.bit_length() - 1) == (m >> block.bit_length() - 1)) & (m <= i))
    return cum, place, diag_ok, off_ok, sub_of_row, score_masks


def _transpose_tiles(x):
    return jnp.concatenate([x[:, t:t + LANES].T for t in range(0, x.shape[1], LANES)], axis=0)


def _diag_partial_sums(q_t, k_t, g_t):
    dk = q_t.shape[0]
    slab = min(dk, SLAB)
    accs = [None] * SUB
    for s0 in range(0, dk, slab):
        qs = q_t[s0:s0 + slab]
        ks = k_t[s0:s0 + slab]
        g1 = g_t[s0:s0 + slab]
        g2 = g1 * pltpu.roll(g1, 1, axis=1)
        g4 = g2 * pltpu.roll(g2, 2, axis=1)
        g4_back = [g4] + [pltpu.roll(g4, DIAG_RUN * n, axis=1) for n in range(1, SUB // DIAG_RUN - 1)]
        for d0 in range(0, SUB, DIAG_RUN):
            w = ks
            if d0 > 0:
                w = pltpu.roll(ks, d0, axis=1)
                for n in range(d0 // DIAG_RUN):
                    w = w * g4_back[n]
            for d in range(d0, d0 + DIAG_RUN):
                if d > d0:
                    w = pltpu.roll(w, 1, axis=1) * g1
                p = jnp.sum((qs * w).reshape(slab // SUBLANES, SUBLANES, TILE), axis=0)
                accs[d] = p if accs[d] is None else accs[d] + p
    return jnp.concatenate(accs, axis=0)


def _split3_bf16(x):
    hi = x.astype(BF16)
    rest = x - hi.astype(F32)
    mid = rest.astype(BF16)
    lo = (rest - mid.astype(F32)).astype(BF16)
    return hi, mid, lo


def _cum_dot(mat, pieces):
    hi, mid, lo = pieces
    return _dot(mat, hi) + (_dot(mat, mid) + _dot(mat, lo))


def _head_lanes(h, dk):
    tile = (h * dk) // LANES
    if dk >= LANES:
        return tile, None
    lane = lax.broadcasted_iota(jnp.int32, (1, LANES), 1)
    start = (h * dk) % LANES
    return tile, (lane >= start) & (lane < start + dk)


def _gated_tile_robust(q, k, lg, lg_pieces, v, st_ref, consts, n_heads, dk, dv):
    cum, place, diag_ok, off_ok, sub_of_row, _ = consts
    sums = _cum_dot(cum, lg_pieces)
    cb = sums[0:TILE]
    q_state = (q * jnp.exp(cb)).astype(BF16)
    k_end = k * jnp.exp(sums[TILE:2 * TILE])
    k_last = (k * jnp.exp(sums[2 * TILE:3 * TILE])).astype(BF16)
    q_off = [(q * jnp.exp(sums[(3 + j) * TILE:(4 + j) * TILE])).astype(BF16) for j in range(N_SUB - 1)]
    k_off = [jnp.where(sub_of_row == j, k_end, 0.0).astype(BF16) for j in range(N_SUB - 1)]
    q_t = _transpose_tiles(q)
    k_t = _transpose_tiles(k)
    g_t = _transpose_tiles(jnp.exp(lg))
    outs = []
    for h in range(n_heads):
        ks = slice(h * dk, (h + 1) * dk)
        tile = (h * dk) // LANES
        ls = slice((h * dk) % LANES, (h * dk) % LANES + dk)
        vb = v[:, h * dv:(h + 1) * dv].astype(BF16)
        a_off = _dot_nt(q_off[0][:, ks], k_off[0][:, ks])
        for j in range(1, N_SUB - 1):
            a_off += _dot_nt(q_off[j][:, ks], k_off[j][:, ks])
        part = _diag_partial_sums(q_t[ks], k_t[ks], g_t[ks])
        p_hi, p_lo = _split_bf16(part)
        skew = _dot_tn(p_hi, place) + _dot_tn(p_lo, place)
        a_diag = pltpu.roll(skew, 0, axis=1, stride=1, stride_axis=0)
        a = jnp.where(diag_ok, a_diag, jnp.where(off_ok, a_off, 0.0)).astype(BF16)
        o = _dot(a, vb)
        st = st_ref[tile][:, ls]
        inter = []
        for c in range(TILE // CHUNK):
            rs = slice(c * CHUNK, (c + 1) * CHUNK)
            inter.append(_dot_nt(q_state[rs, ks], st.astype(BF16)))
            last = cb[c * CHUNK + CHUNK - 1:(c + 1) * CHUNK, ks]
            st = st * jnp.exp(last) + _dot_tn(vb[rs], k_last[rs, ks])
        st_ref[tile, :, ls] = st
        outs.append(o + jnp.concatenate(inter, axis=0))
    return outs


def _rows_from(cb, row, spans):
    out = None
    for lo, src in spans:
        val = jnp.zeros_like(cb[0:1]) if src is None else cb[src:src + 1]
        out = val if out is None else jnp.where(row >= lo, val, out)
    return out


def _block_prefix(cb, block):
    if block == CHUNK:
        return cb
    row = lax.broadcasted_iota(jnp.int32, (TILE, 1), 0)
    spans = [(b0, None if b0 % CHUNK == 0 else b0 - 1) for b0 in range(0, TILE, block)]
    return cb - _rows_from(cb, row, spans)


def _gated_tile_bounded(q, k, cb, pb, v, st_ref, consts, n_heads, dk, dv, block):
    score_masks = consts[-1]
    row = lax.broadcasted_iota(jnp.int32, (TILE, 1), 0)
    ends = [cb[c * CHUNK + CHUNK - 1:(c + 1) * CHUNK, :] for c in range(TILE // CHUNK)]
    end_of_row = _rows_from(cb, row, [(c * CHUNK, c * CHUNK + CHUNK - 1) for c in range(TILE // CHUNK)])
    q_state = q * jnp.exp(cb)
    k_last = k * jnp.exp(end_of_row - cb)
    q_parts, k_parts = [], []
    for hs in _levels(block):
        ref = _rows_from(cb, row, [(b0, b0 + hs - 1) for b0 in range(0, TILE, 2 * hs)])
        lower = (row & (2 * hs - 1)) < hs
        x = jnp.exp(jnp.where(lower, ref - cb, cb - ref))
        q_parts.append(jnp.where(lower, 0.0, q * x))
        k_parts.append(jnp.where(lower, k * x, 0.0).astype(BF16))
    q_parts.append(q_state if block == CHUNK else q * jnp.exp(pb))
    k_parts.append((k * jnp.exp(-pb)).astype(BF16))
    outs = [None] * n_heads
    heads_per_tile = max(LANES // dk, 1)
    for tile in range(n_heads * dk // LANES):
        ls = slice(tile * LANES, (tile + 1) * LANES)
        heads = range(tile * heads_per_tile, (tile + 1) * heads_per_tile)
        qh, kl, vb, intra = {}, {}, {}, {}
        for h in heads:
            _, mask = _head_lanes(h, dk)
            own = (lambda a: a) if mask is None else (lambda a: jnp.where(mask, a, 0.0))
            qh[h] = own(q_state[:, ls]).astype(BF16)
            kl[h] = own(k_last[:, ls]).astype(BF16)
            vb[h] = v[:, h * dv:(h + 1) * dv].astype(BF16)
            a = 0.0
            for qp, kp, ok in zip(q_parts, k_parts, score_masks):
                a = jnp.where(ok, _dot_nt(own(qp[:, ls]).astype(BF16), kp[:, ls]), a)
            intra[h] = _dot(a.astype(BF16), vb[h])
        st = st_ref[tile]
        inter = {h: [] for h in heads}
        for c in range(TILE // CHUNK):
            rs = slice(c * CHUNK, (c + 1) * CHUNK)
            stb = st.astype(BF16)
            upd = None
            for h in heads:
                inter[h].append(_dot_nt(qh[h][rs], stb))
                u = _dot_tn(vb[h][rs], kl[h][rs])
                upd = u if upd is None else upd + u
            st = st * jnp.exp(ends[c][:, ls]) + upd
        st_ref[tile] = st
        for h in heads:
            outs[h] = intra[h] + jnp.concatenate(inter[h], axis=0)
    return outs


SAFE_DECAY = 64.0


def _gated_tile(q, k, lg, v, st_ref, consts, n_heads, dk, dv, block, finish):
    pieces = _split3_bf16(lg)
    cb = _cum_dot(consts[0][0:TILE], pieces)
    pb = _block_prefix(cb, block)
    bounded = jnp.min(pb) >= -SAFE_DECAY

    @pl.when(bounded)
    def _():
        finish(_gated_tile_bounded(q, k, cb, pb, v, st_ref, consts, n_heads, dk, dv, block))

    @pl.when(jnp.logical_not(bounded))
    def _():
        finish(_gated_tile_robust(q, k, lg, pieces, v, st_ref, consts, n_heads, dk, dv))


def _gated_step(lg_of, qkv_of, st_ref, cb_ref, consts, n_heads, dk, dv, block, n_tiles, finish):
    worst = None
    for c in range(n_tiles):
        rows = slice(c * TILE, (c + 1) * TILE)
        cb = _cum_dot(consts[0][0:TILE], _split3_bf16(lg_of(rows)))
        cb_ref[rows, :] = cb
        pb = _block_prefix(cb, block)
        worst = pb if worst is None else jnp.minimum(worst, pb)
    all_bounded = jnp.min(worst) >= -SAFE_DECAY

    def tile_rows(c):
        return pl.ds(pl.multiple_of(c * TILE, TILE), TILE)

    @pl.when(all_bounded)
    def _():
        def body(c, carry):
            rows = tile_rows(c)
            q, k, v = qkv_of(rows)
            cb = cb_ref[rows, :]
            pb = _block_prefix(cb, block)
            finish(rows, _gated_tile_bounded(q, k, cb, pb, v, st_ref, consts, n_heads, dk, dv, block))
            return carry

        lax.fori_loop(0, n_tiles, body, 0)

    @pl.when(jnp.logical_not(all_bounded))
    def _():
        def body(c, carry):
            rows = tile_rows(c)
            q, k, v = qkv_of(rows)
            _gated_tile(q, k, lg_of(rows), v, st_ref, consts, n_heads, dk, dv, block,
                        functools.partial(finish, rows))
            return carry

        lax.fori_loop(0, n_tiles, body, 0)


def _gla_kernel(q_ref, k_ref, v_ref, g_ref, r_ref, w2_ref, b2_ref, nw_ref, o_ref, st_ref, cb_ref, *, n_tiles):
    @pl.when(pl.program_id(1) == 0)
    def _():
        st_ref[...] = jnp.zeros_like(st_ref)

    consts = _gate_constants(GLA_BLOCK)
    w2 = w2_ref[...]
    b2 = b2_ref[...]
    nw = nw_ref[...]

    def lg_of(rows):
        pre = _dot3(_f32(r_ref[rows, :]), w2) + b2
        return _log_sigmoid(pre) * (1.0 / GLA_GATE_NORMALIZER)

    def qkv_of(rows):
        return _f32(q_ref[rows, :]) * (GLA_DK ** -0.5), _f32(k_ref[rows, :]), _f32(v_ref[rows, :])

    def finish(rows, outs):
        g = _f32(g_ref[rows, :])
        for h, o in enumerate(outs):
            vs = slice(h * GLA_DV, (h + 1) * GLA_DV)
            o_ref[rows, vs] = ((_rms(o) * nw) * _silu(g[:, vs])).astype(BF16)

    _gated_step(lg_of, qkv_of, st_ref, cb_ref, consts, GLA_HEADS, GLA_DK, GLA_DV, GLA_BLOCK, n_tiles, finish)


def _hgrn_kernel(q_ref, f_ref, i_ref, g_ref, lb_ref, nw_ref, o_ref, st_ref, cb_ref, key_ref, *, n_tiles, layer):
    @pl.when(pl.program_id(1) == 0)
    def _():
        st_ref[...] = jnp.zeros_like(st_ref)

    consts = _gate_constants(HG_BLOCK)
    lbv = lb_ref[...]
    e = jnp.exp(lbv - jnp.max(lbv, axis=0, keepdims=True))
    soft = e / jnp.sum(e, axis=0, keepdims=True)
    cum = soft[0:1]
    for i in range(1, layer + 1):
        cum = cum + soft[i:i + 1]
    lb = cum - soft[0:1]
    nw = nw_ref[...]

    def lg_of(rows):
        f = lb + (1.0 - lb) * _sigmoid(_f32(f_ref[rows, :]))
        key_ref[rows, :] = 1.0 - f
        return jnp.log(f)

    def qkv_of(rows):
        q = _silu(_f32(q_ref[rows, :])) * (HG_DK ** -0.5)
        return q, key_ref[rows, :], _f32(i_ref[rows, :])

    def finish(rows, outs):
        g = _f32(g_ref[rows, :])
        for h, o in enumerate(outs):
            hs = slice(h * HG_DK, (h + 1) * HG_DK)
            o_ref[rows, hs] = ((_rms(o) * nw) * _silu(g[:, hs])).astype(BF16)

    _gated_step(lg_of, qkv_of, st_ref, cb_ref, consts, HG_HEADS, HG_DK, HG_DK, HG_BLOCK, n_tiles, finish)


def _col_spec(t, width, name, nt):
    blk = _DST[name] // width
    return pl.BlockSpec((t, width), lambda b, i: (b * nt + i, blk))


def _mixer_out_spec(t, nt):
    return pl.BlockSpec((t, GROUP_WIDTH), lambda b, i: (b * nt + i, 0))


def _mixer_params():
    return pltpu.CompilerParams(dimension_semantics=("arbitrary", "arbitrary"),
                                vmem_limit_bytes=VMEM_LIMIT)


def _gla(proj, w2_pad, b2, nw, batch, seq, t=1024):
    nt = seq // t
    full = lambda shape: pl.BlockSpec(shape, lambda b, i: (0, 0))
    return pl.pallas_call(
        functools.partial(_gla_kernel, n_tiles=t // TILE),
        out_shape=jax.ShapeDtypeStruct((batch * seq, GROUP_WIDTH), BF16),
        grid=(batch, nt),
        in_specs=[_col_spec(t, 256, "a_q", nt), _col_spec(t, 256, "a_k", nt),
                  _col_spec(t, 512, "a_v", nt), _col_spec(t, 512, "a_g", nt),
                  _col_spec(t, LANES, "a_r", nt),
                  full((LANES, GLA_HEADS * GLA_DK)), full((1, GLA_HEADS * GLA_DK)), full((1, GLA_DV))],
        out_specs=_mixer_out_spec(t, nt),
        scratch_shapes=[pltpu.VMEM((GLA_HEADS * GLA_DK // LANES, GLA_DV, LANES), F32),
                        pltpu.VMEM((t, GLA_HEADS * GLA_DK), F32)],
        compiler_params=_mixer_params(),
        name="gla_mixer",
    )(proj, proj, proj, proj, proj, w2_pad, b2.reshape(1, -1), nw.reshape(1, -1))


def _hgrn(proj, hgrn_lb, nw, layer, batch, seq, t=1024):
    nt = seq // t
    full = lambda shape: pl.BlockSpec(shape, lambda b, i: (0, 0))
    return pl.pallas_call(
        functools.partial(_hgrn_kernel, n_tiles=t // TILE, layer=layer),
        out_shape=jax.ShapeDtypeStruct((batch * seq, GROUP_WIDTH), BF16),
        grid=(batch, nt),
        in_specs=[_col_spec(t, 512, "d_q", nt), _col_spec(t, 512, "d_f", nt),
                  _col_spec(t, 512, "d_i", nt), _col_spec(t, 512, "d_g", nt),
                  full(hgrn_lb.shape), full((1, HG_DK))],
        out_specs=_mixer_out_spec(t, nt),
        scratch_shapes=[pltpu.VMEM((HG_HEADS * HG_DK // LANES, HG_DK, LANES), F32),
                        pltpu.VMEM((t, HG_HEADS * HG_DK), F32), pltpu.VMEM((t, HG_HEADS * HG_DK), F32)],
        compiler_params=_mixer_params(),
        name="hgrn_mixer",
    )(proj, proj, proj, proj, hgrn_lb, nw.reshape(1, -1))


def _ret_kernel(q_ref, k_ref, v_ref, g_ref, pos_ref, inv_ref, o_ref, st_ref, *, t):
    @pl.when(pl.program_id(1) == 0)
    def _():
        st_ref[...] = jnp.zeros_like(st_ref)

    ang = pos_ref[0] * inv_ref[...]
    lane = lax.broadcasted_iota(jnp.int32, (1, RET_DK), 1)
    cos = jnp.cos(ang)
    sin = jnp.where(lane < RET_DK // 2, -jnp.sin(ang), jnp.sin(ang))
    q = _f32(q_ref[...])
    k = _f32(k_ref[...])
    v = v_ref[...]
    g = _f32(g_ref[...])
    rowm = lax.broadcasted_iota(jnp.int32, (t, t), 0)
    colm = lax.broadcasted_iota(jnp.int32, (t, t), 1)
    rel = (rowm - colm).astype(F32)
    pos = lax.broadcasted_iota(jnp.int32, (t, RET_DK), 0).astype(F32)
    half = RET_DK // 2
    for h in range(RET_HEADS):
        lg = math.log1p(-(2.0 ** (-5.0 - h)))
        hs = slice(h * RET_DK, (h + 1) * RET_DK)
        qh = q[:, hs]
        kh = k[:, hs]
        qr = qh * cos + pltpu.roll(qh, half, axis=1) * sin
        kr = (kh * cos + pltpu.roll(kh, half, axis=1) * sin) * (RET_DK ** -0.5)
        vb = v[:, hs].astype(BF16)
        dmask = jnp.where(rel >= 0, jnp.exp(jnp.minimum(lg * rel, 0.0)), 0.0)
        scores = _dot_nt(qr.astype(BF16), kr.astype(BF16)) * dmask
        st = st_ref[h]
        xi = jnp.exp(lg * (pos + 1.0))
        o = _dot(scores.astype(BF16), vb) + _dot((qr * xi).astype(BF16), st.astype(BF16))
        zeta = jnp.exp(lg * (t - 1.0 - pos))
        st_ref[h] = st * math.exp(lg * t) + _dot_tn((kr * zeta).astype(BF16), vb)
        o_ref[:, hs] = (_rms(o) * _silu(g[:, hs])).astype(BF16)


def _ret(proj, pos, inv, batch, seq, t=256):
    nt = seq // t
    pos_spec = pl.BlockSpec((1, t, 1), lambda b, i: (b, i, 0))
    inv_spec = pl.BlockSpec((1, RET_DK), lambda b, i: (0, 0))
    return pl.pallas_call(
        functools.partial(_ret_kernel, t=t),
        out_shape=jax.ShapeDtypeStruct((batch * seq, GROUP_WIDTH), BF16),
        grid=(batch, nt),
        in_specs=[_col_spec(t, 512, "b_q", nt), _col_spec(t, 512, "b_k", nt),
                  _col_spec(t, 512, "b_v", nt), _col_spec(t, 512, "b_g", nt), pos_spec, inv_spec],
        out_specs=_mixer_out_spec(t, nt),
        scratch_shapes=[pltpu.VMEM((RET_HEADS, RET_DK, RET_DK), F32)],
        compiler_params=_mixer_params(),
        name="ret_mixer",
    )(proj, proj, proj, proj, pos, inv)


def _rope_partial(x, c, s_lo, s_hi):
    n = x.shape[-1]
    half = ROPE_DIM // 2
    return x * c + pltpu.roll(x, n - half, axis=1) * s_lo + pltpu.roll(x, half, axis=1) * s_hi


def _swa_window(q_of, kc_raw, vc, pos, inv, has_prev, sinks, kprev_ref, vprev_ref, store):
    w = WINDOW
    lane = lax.broadcasted_iota(jnp.int32, (1, LANES), 1)
    lo_half = lane < SWA_HD
    ang = pos * inv
    in_head = lane & (SWA_HD - 1)
    c = jnp.cos(ang)
    s_lo = jnp.where(in_head < ROPE_DIM // 2, -jnp.sin(ang), 0.0)
    s_hi = jnp.where((in_head >= ROPE_DIM // 2) & (in_head < ROPE_DIM), jnp.sin(ang), 0.0)
    kc = _rope_partial(kc_raw, c, s_lo, s_hi)
    group = SWA_Q_HEADS // SWA_KV_HEADS
    rowm = lax.broadcasted_iota(jnp.int32, (group * w, w), 0) & (w - 1)
    colm = lax.broadcasted_iota(jnp.int32, (group * w, w), 1)
    cur_ok = colm <= rowm
    prev_ok = (colm > rowm) & has_prev

    def both_halves(a, kv):
        swapped = pltpu.roll(a, SWA_HD, axis=1)
        return jnp.where(lo_half, a, swapped) if kv == 0 else jnp.where(lo_half, swapped, a)

    for kv in range(SWA_KV_HEADS):
        k2c = both_halves(kc, kv).astype(BF16)
        k2p = kprev_ref[kv]
        v2c = both_halves(vc, kv)
        v_half = [(jnp.where(lo_half, v2c, jnp.where(lane == SWA_HD, 1.0, 0.0)).astype(BF16), vprev_ref[kv, 0]),
                  (jnp.where(lo_half, jnp.where(lane == 0, 1.0, 0.0), v2c).astype(BF16), vprev_ref[kv, 1])]
        ones_lane = (SWA_HD, 0)
        kprev_ref[kv] = k2c
        vprev_ref[kv, 0] = v_half[0][0]
        vprev_ref[kv, 1] = v_half[1][0]
        q_rows, sink_rows = [], []
        for t in range(kv * group // 2, (kv + 1) * group // 2):
            qt = _rope_partial(q_of(t), c, s_lo, s_hi) * (SWA_HD ** -0.5)
            q_rows += [jnp.where(lo_half, qt, 0.0), jnp.where(lo_half, 0.0, qt)]
            sink_rows += [jnp.broadcast_to(sinks[:, 2 * t:2 * t + 1], (w, 1)),
                          jnp.broadcast_to(sinks[:, 2 * t + 1:2 * t + 2], (w, 1))]
        q4 = jnp.concatenate(q_rows, axis=0).astype(BF16)
        sink = jnp.concatenate(sink_rows, axis=0)
        s_c = jnp.where(cur_ok, _dot_nt(q4, k2c), -jnp.inf)
        s_p = jnp.where(prev_ok, _dot_nt(q4, k2p), -jnp.inf)
        m = jnp.maximum(jnp.max(jnp.maximum(s_c, s_p), axis=-1, keepdims=True), sink)
        e_c = jnp.exp(s_c - m).astype(BF16)
        e_p = jnp.exp(s_p - m).astype(BF16)
        e_sink = jnp.exp(sink - m)
        for n, t in enumerate(range(kv * group // 2, (kv + 1) * group // 2)):
            halves = []
            for half in range(2):
                rs = slice((2 * n + half) * w, (2 * n + half + 1) * w)
                vh_c, vh_p = v_half[half]
                o = _dot(e_c[rs], vh_c) + _dot(e_p[rs], vh_p)
                denom = o[:, ones_lane[half]:ones_lane[half] + 1] + e_sink[rs]
                halves.append(o * (1.0 / denom))
            store(t, jnp.where(lo_half, halves[0], halves[1]).astype(BF16))


def _swa_kernel(q_ref, kc_ref, vc_ref, pos_ref, inv_ref, sink_ref, o_ref, kprev_ref, vprev_ref, *, t):
    first = pl.program_id(1) == 0

    @pl.when(first)
    def _():
        kprev_ref[...] = jnp.zeros_like(kprev_ref)
        vprev_ref[...] = jnp.zeros_like(vprev_ref)

    sinks = sink_ref[...]
    inv = inv_ref[...]
    for i in range(t // WINDOW):
        rows = slice(i * WINDOW, (i + 1) * WINDOW)
        has_prev = jnp.logical_not(first) if i == 0 else True

        def store(tile, value, rows=rows):
            o_ref[rows, tile * LANES:(tile + 1) * LANES] = value

        _swa_window(lambda tile, rows=rows: _f32(q_ref[rows, tile * LANES:(tile + 1) * LANES]),
                    _f32(kc_ref[rows, :]), _f32(vc_ref[rows, :]), pos_ref[0, rows], inv, has_prev, sinks,
                    kprev_ref, vprev_ref, store)


def _swa(proj, pos, inv, sinks, batch, seq, t=512):
    nt = seq // t
    kv_w = SWA_KV_HEADS * SWA_HD
    cur = lambda name: pl.BlockSpec((t, kv_w), lambda b, i: (b * nt + i, _DST[name] // kv_w))
    pos_spec = pl.BlockSpec((1, t, 1), lambda b, i: (b, i, 0))
    inv_spec = pl.BlockSpec((1, LANES), lambda b, i: (0, 0))
    return pl.pallas_call(
        functools.partial(_swa_kernel, t=t),
        out_shape=jax.ShapeDtypeStruct((batch * seq, GROUP_WIDTH), BF16),
        grid=(batch, nt),
        in_specs=[_col_spec(t, 512, "c_q", nt), cur("c_k"), cur("c_v"), pos_spec, inv_spec,
                  pl.BlockSpec((1, SWA_Q_HEADS), lambda b, i: (0, 0))],
        out_specs=_mixer_out_spec(t, nt),
        scratch_shapes=[pltpu.VMEM((SWA_KV_HEADS, WINDOW, LANES), BF16),
                        pltpu.VMEM((SWA_KV_HEADS, 2, WINDOW, LANES), BF16)],
        compiler_params=_mixer_params(),
        name="swa_mixer",
    )(proj, proj, proj, pos, inv, sinks.reshape(1, -1))


def _out_proj_kernel(x_ref, mod_ref, a_ref, b_ref, c_ref, d_ref, w_ref, o_ref):
    gw = GROUP_WIDTH
    acc = _dot(a_ref[...], w_ref[0:gw, :])
    acc += _dot(b_ref[...], w_ref[gw:2 * gw, :])
    acc += _dot(c_ref[...], w_ref[2 * gw:3 * gw, :])
    acc += _dot(d_ref[...], w_ref[3 * gw:4 * gw, :])
    gate = mod_ref[0][2:3]
    o_ref[...] = x_ref[...] + gate * acc


def _out_proj(x2, mod_l, mixed, w_out, layer, seq, tm=512):
    m_rows, d = x2.shape
    per_b = seq // tm
    grp = pl.BlockSpec((tm, GROUP_WIDTH), lambda i: (i, 0))
    return pl.pallas_call(
        _out_proj_kernel,
        out_shape=jax.ShapeDtypeStruct((m_rows, d), F32),
        grid=(m_rows // tm,),
        in_specs=[pl.BlockSpec((tm, d), lambda i: (i, 0)),
                  pl.BlockSpec((1, 6, d), lambda i: (i // per_b, 0, 0)),
                  grp, grp, grp, grp,
                  pl.BlockSpec((None, d, d), lambda i: (layer, 0, 0))],
        out_specs=pl.BlockSpec((tm, d), lambda i: (i, 0)),
        compiler_params=pltpu.CompilerParams(
            dimension_semantics=("arbitrary",), vmem_limit_bytes=VMEM_LIMIT),
        name="out_proj",
    )(x2, mod_l, *mixed, w_out)


def _ffn_kernel(x_ref, mod_ref, nw_ref, wg_ref, wu_ref, wd_ref, fw_ref, o_ref, h_ref, acc_ref, *, final):
    f = pl.program_id(1)

    @pl.when(f == 0)
    def _():
        m = mod_ref[0]
        h = _modulated_norm(x_ref[...], nw_ref[...], m[3:4], m[4:5])
        h_ref[...] = h.astype(BF16)
        acc_ref[...] = jnp.zeros_like(acc_ref)

    h = h_ref[...]
    g = _dot(h, wg_ref[...])
    u = _dot(h, wu_ref[...])
    act = (_silu(g) * u).astype(BF16)
    acc_ref[...] += _dot(act, wd_ref[...])

    @pl.when(f == pl.num_programs(1) - 1)
    def _():
        y = x_ref[...] + mod_ref[0][5:6] * acc_ref[...]
        o_ref[...] = _rms(y) * fw_ref[...] if final else y


def _ffn(x2, mod_l, nw, w_in, w_down, layer, final_w, final, seq, tm=512, tf=512):
    m_rows, d = x2.shape
    nf = D_FF // tf
    per_b = seq // tm
    return pl.pallas_call(
        functools.partial(_ffn_kernel, final=final),
        out_shape=jax.ShapeDtypeStruct((m_rows, d), F32),
        grid=(m_rows // tm, nf),
        in_specs=[pl.BlockSpec((tm, d), lambda i, f: (i, 0)),
                  pl.BlockSpec((1, 6, d), lambda i, f: (i // per_b, 0, 0)),
                  pl.BlockSpec((1, d), lambda i, f: (0, 0)),
                  pl.BlockSpec((None, d, tf), lambda i, f: (layer, 0, f)),
                  pl.BlockSpec((None, d, tf), lambda i, f: (layer, 0, nf + f)),
                  pl.BlockSpec((None, tf, d), lambda i, f: (layer, f, 0)),
                  pl.BlockSpec((1, d), lambda i, f: (0, 0))],
        out_specs=pl.BlockSpec((tm, d), lambda i, f: (i, 0)),
        scratch_shapes=[pltpu.VMEM((tm, d), BF16), pltpu.VMEM((tm, d), F32)],
        compiler_params=pltpu.CompilerParams(
            dimension_semantics=("arbitrary", "arbitrary"), vmem_limit_bytes=VMEM_LIMIT),
        name="ffn",
    )(x2, mod_l, nw.reshape(1, d), w_in, w_in, w_down, final_w.reshape(1, d))


CAST_ROWS = 256
CAST_COLS = 2816


def _cast_kernel(x_ref, o_ref):
    o_ref[...] = x_ref[...].astype(BF16)


def _to_bf16(w):
    depth, k, n = w.shape
    tn = n if n <= CAST_COLS else CAST_COLS
    blk = pl.BlockSpec((1, CAST_ROWS, tn), lambda l, i, j: (l, i, j))
    return pl.pallas_call(
        _cast_kernel,
        out_shape=jax.ShapeDtypeStruct(w.shape, BF16),
        grid=(depth, k // CAST_ROWS, n // tn),
        in_specs=[blk], out_specs=blk,
        compiler_params=pltpu.CompilerParams(
            dimension_semantics=("arbitrary",) * 3, vmem_limit_bytes=VMEM_LIMIT),
        name="cast_weights",
    )(w)


def _permute_kernel(x_ref, o_ref):
    off = 0
    for name, width in _DST_ORDER:
        o_ref[0, off:off + width, :] = x_ref[0, _SRC[name]:_SRC[name] + width, :].astype(BF16)
        off += width
    o_ref[0, off:, :] = jnp.zeros((PROJ_WIDTH - off, o_ref.shape[2]), BF16)


def _permute_w_in(w):
    w_t = jnp.swapaxes(w, 1, 2)
    depth, n, k = w_t.shape
    return pl.pallas_call(
        _permute_kernel,
        out_shape=jax.ShapeDtypeStruct((depth, PROJ_WIDTH, k), BF16),
        grid=(depth, k // CAST_ROWS),
        in_specs=[pl.BlockSpec((1, n, CAST_ROWS), lambda l, i: (l, 0, i))],
        out_specs=pl.BlockSpec((1, PROJ_WIDTH, CAST_ROWS), lambda l, i: (l, 0, i)),
        compiler_params=pltpu.CompilerParams(
            dimension_semantics=("arbitrary", "arbitrary"), vmem_limit_bytes=VMEM_LIMIT),
        name="permute_w_in",
    )(w_t)


def _ret_inv_freq():
    inv = 1.0 / jnp.power(RET_ROT_BASE, jnp.linspace(0.0, 1.0, RET_DK // 2, dtype=F32))
    return jnp.concatenate([inv, inv])[None, :]


def _swa_inv_freq():
    half = ROPE_DIM // 2
    inv = 1.0 / jnp.power(ROPE_THETA, jnp.arange(half, dtype=F32) / half)
    per_head = jnp.concatenate([inv, inv, jnp.zeros((SWA_HD - ROPE_DIM,), F32)])
    return jnp.concatenate([per_head] * (LANES // SWA_HD))[None, :]


def kernel(x, c, positions, w_ada, b_ada, norm1_w, w_in, gla_gate_w2, gla_gate_b2, gla_norm_w, swa_sinks, hgrn_lb, hgrn_norm_w, w_out, norm2_w, w_ffn_in, w_ffn_down, final_norm_w):
    batch, seq, d = x.shape
    depth = w_ada.shape[0]
    mod = _ada_modulation(c, w_ada, b_ada).reshape(depth, batch, 6, d)
    pos = positions.astype(F32)[:, :, None]
    ret_inv, swa_inv = _ret_inv_freq(), _swa_inv_freq()
    x2 = x.reshape(batch * seq, d)
    w_in_b, w_out_b = _permute_w_in(w_in), _to_bf16(w_out)
    w_ffn_in_b, w_ffn_down_b = _to_bf16(w_ffn_in), _to_bf16(w_ffn_down)
    for l in range(depth):
        w2_pad = jnp.zeros((LANES, GLA_HEADS * GLA_DK), F32).at[:GLA_GATE_RANK].set(gla_gate_w2[l])
        proj = _in_proj(x2, mod[l], norm1_w[l], w_in_b, l, seq)
        mixed = (
            _gla(proj, w2_pad, gla_gate_b2[l], gla_norm_w[l], batch, seq),
            _ret(proj, pos, ret_inv, batch, seq),
            _swa(proj, pos, swa_inv, swa_sinks[l], batch, seq),
            _hgrn(proj, hgrn_lb, hgrn_norm_w[l], l, batch, seq),
        )
        x2 = _out_proj(x2, mod[l], mixed, w_out_b, l, seq)
        x2 = _ffn(x2, mod[l], norm2_w[l], w_ffn_in_b, w_ffn_down_b, l, final_norm_w, l == depth - 1, seq)
    return x2.reshape(batch, seq, d)
```

```python
import functools
import math

import jax
import jax.numpy as jnp
from jax import lax
from jax.experimental import pallas as pl
from jax.experimental.pallas import tpu as pltpu

F32 = jnp.float32
BF16 = jnp.bfloat16

D_MODEL = 2048
DEPTH = 2
EPS = 1e-6
GROUP_WIDTH = D_MODEL // 4
CHUNK = 64
SUB = 16
N_SUB = CHUNK // SUB
GLA_HEADS = 4
GLA_DV = GROUP_WIDTH // GLA_HEADS
GLA_DK = GLA_DV // 2
GLA_GATE_RANK = 16
GLA_GATE_NORMALIZER = 16.0
RET_HEADS = 4
RET_DK = GROUP_WIDTH // RET_HEADS
RET_ROT_BASE = 10000.0
SWA_Q_HEADS = 8
SWA_KV_HEADS = 2
SWA_HD = GROUP_WIDTH // SWA_Q_HEADS
WINDOW = 128
ROPE_THETA = 500000.0
ROPE_DIM = SWA_HD // 4
HG_HEADS = 4
HG_DK = GROUP_WIDTH // HG_HEADS
D_FF = ((8 * D_MODEL + 3 * 256 - 1) // (3 * 256)) * 256

LANES = 128
VMEM_LIMIT = 48 * 1024 * 1024
VMEM_LIMIT_IN_PROJ = 56 * 1024 * 1024

_SRC = dict(a_q=0, a_k=256, a_v=512, a_g=1024, a_r=1536,
            b_q=1552, b_k=2064, b_v=2576, b_g=3088,
            c_q=3600, c_k=4112, c_v=4240,
            d_q=4368, d_f=4880, d_i=5392, d_g=5904)
_DST_ORDER = (("a_q", 256), ("a_k", 256), ("a_v", 512), ("a_g", 512), ("c_q", 512),
              ("b_q", 512), ("b_k", 512), ("b_v", 512), ("b_g", 512),
              ("d_q", 512), ("d_f", 512), ("d_i", 512), ("d_g", 512),
              ("c_k", 128), ("c_v", 128), ("a_r", GLA_GATE_RANK))
PROJ_WIDTH = 6528
PROJ_TN = PROJ_WIDTH // 3


def _dst_offsets():
    offs, o = {}, 0
    for name, w in _DST_ORDER:
        offs[name] = o
        o += w
    return offs


_DST = _dst_offsets()


def _dot(a, b):
    return jnp.dot(a, b, preferred_element_type=F32)


def _dot_nt(a, b):
    return lax.dot_general(a, b, (((1,), (1,)), ((), ())), preferred_element_type=F32)


def _dot_tn(a, b):
    return lax.dot_general(a, b, (((0,), (0,)), ((), ())), preferred_element_type=F32)


def _split_bf16(x):
    hi = x.astype(BF16)
    lo = (x - hi.astype(F32)).astype(BF16)
    return hi, lo


def _dot3(a, b):
    a_hi, a_lo = _split_bf16(a)
    b_hi, b_lo = _split_bf16(b)
    return _dot(a_hi, b_hi) + (_dot(a_hi, b_lo) + _dot(a_lo, b_hi))


def _sigmoid(x):
    return 1.0 / (1.0 + jnp.exp(-x))


def _silu(x):
    return (0.5 * x) * (1.0 + jnp.tanh(0.5 * x))


def _log_sigmoid(x):
    return jnp.minimum(x, 0.0) - jnp.log1p(jnp.exp(-jnp.abs(x)))


def _f32(x):
    return x.astype(F32)


def _rms(x):
    return x * lax.rsqrt(jnp.mean(x * x, axis=-1, keepdims=True) + EPS)


def _ada_kernel(c_ref, w_ref, b_ref, o_ref):
    cond = _silu(c_ref[...])
    o_ref[0] = _dot3(cond, w_ref[0]) + b_ref[0]


def _ada_modulation(c, w_ada, b_ada):
    depth, d, n = w_ada.shape
    rows = 8
    c_pad = jnp.zeros((rows, d), F32).at[:c.shape[0]].set(c)
    tn = 2048
    out = pl.pallas_call(
        _ada_kernel,
        out_shape=jax.ShapeDtypeStruct((depth, rows, n), F32),
        grid=(depth, n // tn),
        in_specs=[pl.BlockSpec((rows, d), lambda l, j: (0, 0)),
                  pl.BlockSpec((1, d, tn), lambda l, j: (l, 0, j)),
                  pl.BlockSpec((1, 1, tn), lambda l, j: (l, 0, j))],
        out_specs=pl.BlockSpec((1, rows, tn), lambda l, j: (l, 0, j)),
        compiler_params=pltpu.CompilerParams(
            dimension_semantics=("arbitrary", "arbitrary"), vmem_limit_bytes=VMEM_LIMIT),
        name="ada_modulation",
    )(c_pad, w_ada, b_ada.reshape(depth, 1, n))
    return out[:, :c.shape[0]]


def _modulated_norm(x, nw, shift, scale):
    return _rms(x) * (nw * (1.0 + scale)) + shift


def _in_proj_kernel(x_ref, mod_ref, nw_ref, w_ref, o_ref, h_ref):
    @pl.when(pl.program_id(1) == 0)
    def _():
        m = mod_ref[0]
        h = _modulated_norm(x_ref[...], nw_ref[...], m[0:1], m[1:2])
        h_ref[...] = h.astype(BF16)

    half = h_ref.shape[0] // 2
    for r0 in (0, half):
        o_ref[r0:r0 + half, :] = _dot_nt(h_ref[r0:r0 + half, :], w_ref[...]).astype(BF16)


def _in_proj(x2, mod_l, nw, w_perm, layer, seq, tm=1024):
    m_rows, d = x2.shape
    n = w_perm.shape[1]
    per_b = seq // tm
    return pl.pallas_call(
        _in_proj_kernel,
        out_shape=jax.ShapeDtypeStruct((m_rows, n), BF16),
        grid=(m_rows // tm, n // PROJ_TN),
        in_specs=[pl.BlockSpec((tm, d), lambda i, j: (i, 0)),
                  pl.BlockSpec((1, 6, d), lambda i, j: (i // per_b, 0, 0)),
                  pl.BlockSpec((1, d), lambda i, j: (0, 0)),
                  pl.BlockSpec((None, PROJ_TN, d), lambda i, j: (layer, j, 0))],
        out_specs=pl.BlockSpec((tm, PROJ_TN), lambda i, j: (i, j)),
        scratch_shapes=[pltpu.VMEM((tm, d), BF16)],
        compiler_params=pltpu.CompilerParams(
            dimension_semantics=("arbitrary", "arbitrary"), vmem_limit_bytes=VMEM_LIMIT_IN_PROJ),
        name="in_proj",
    )(x2, mod_l, nw.reshape(1, d), w_perm)


TILE = 2 * CHUNK
GLA_BLOCK = CHUNK
HG_BLOCK = CHUNK // 2
CHUNK_SHIFT = CHUNK.bit_length() - 1
SUB_SHIFT = SUB.bit_length() - 1
SUBLANES = 8
SLAB = 64
DIAG_RUN = 4
assert DIAG_RUN == 4


def _ones_where(mask):
    return jnp.where(mask, 1.0, 0.0).astype(BF16)


def _levels(block):
    return [hs for hs in (CHUNK // 2, CHUNK // 4) if hs >= block]


def _gate_constants(block):
    i = lax.broadcasted_iota(jnp.int32, (TILE, TILE), 0)
    m = lax.broadcasted_iota(jnp.int32, (TILE, TILE), 1)
    same_chunk = (i >> CHUNK_SHIFT) == (m >> CHUNK_SHIFT)
    sub_end = i | (SUB - 1)
    chunk_start = i & ~(CHUNK - 1)
    after = same_chunk & (m > i)
    mats = [same_chunk & (m <= i),
            after & (m <= sub_end),
            after]
    for j in range(N_SUB - 1):
        mats.append(same_chunk & (m > chunk_start + (SUB * j + SUB - 1)) & (m <= i))
    cum = jnp.concatenate([_ones_where(x) for x in mats], axis=0)
    place = _ones_where(m == ((TILE - (i >> 3)) & (TILE - 1)))
    diag_ok = ((i >> SUB_SHIFT) == (m >> SUB_SHIFT)) & (m <= i)
    off_ok = same_chunk & ((m >> SUB_SHIFT) < (i >> SUB_SHIFT))
    sub_of_row = (lax.broadcasted_iota(jnp.int32, (TILE, 1), 0) >> SUB_SHIFT) & (N_SUB - 1)
    score_masks = []
    for hs in _levels(block):
        same = (i >> (2 * hs).bit_length() - 1) == (m >> (2 * hs).bit_length() - 1)
        score_masks.append(same & ((i & (2 * hs - 1)) >= hs) & ((m & (2 * hs - 1)) < hs))
    score_masks.append(((i >> block.bit_length() - 1) == (m >> block.bit_length() - 1)) & (m <= i))
    return cum, place, diag_ok, off_ok, sub_of_row, score_masks


def _transpose_tiles(x):
    return jnp.concatenate([x[:, t:t + LANES].T for t in range(0, x.shape[1], LANES)], axis=0)


def _diag_partial_sums(q_t, k_t, g_t):
    dk = q_t.shape[0]
    slab = min(dk, SLAB)
    accs = [None] * SUB
    for s0 in range(0, dk, slab):
        qs = q_t[s0:s0 + slab]
        ks = k_t[s0:s0 + slab]
        g1 = g_t[s0:s0 + slab]
        g2 = g1 * pltpu.roll(g1, 1, axis=1)
        g4 = g2 * pltpu.roll(g2, 2, axis=1)
        g4_back = [g4] + [pltpu.roll(g4, DIAG_RUN * n, axis=1) for n in range(1, SUB // DIAG_RUN - 1)]
        for d0 in range(0, SUB, DIAG_RUN):
            w = ks
            if d0 > 0:
                w = pltpu.roll(ks, d0, axis=1)
                for n in range(d0 // DIAG_RUN):
                    w = w * g4_back[n]
            for d in range(d0, d0 + DIAG_RUN):
                if d > d0:
                    w = pltpu.roll(w, 1, axis=1) * g1
                p = jnp.sum((qs * w).reshape(slab // SUBLANES, SUBLANES, TILE), axis=0)
                accs[d] = p if accs[d] is None else accs[d] + p
    return jnp.concatenate(accs, axis=0)


def _split3_bf16(x):
    hi = x.astype(BF16)
    rest = x - hi.astype(F32)
    mid = rest.astype(BF16)
    lo = (rest - mid.astype(F32)).astype(BF16)
    return hi, mid, lo


def _cum_dot(mat, pieces):
    hi, mid, lo = pieces
    return _dot(mat, hi) + (_dot(mat, mid) + _dot(mat, lo))


def _head_lanes(h, dk):
    tile = (h * dk) // LANES
    if dk >= LANES:
        return tile, None
    lane = lax.broadcasted_iota(jnp.int32, (1, LANES), 1)
    start = (h * dk) % LANES
    return tile, (lane >= start) & (lane < start + dk)


def _gated_tile_robust(q, k, lg, lg_pieces, v, st_ref, consts, n_heads, dk, dv):
    cum, place, diag_ok, off_ok, sub_of_row, _ = consts
    sums = _cum_dot(cum, lg_pieces)
    cb = sums[0:TILE]
    q_state = (q * jnp.exp(cb)).astype(BF16)
    k_end = k * jnp.exp(sums[TILE:2 * TILE])
    k_last = (k * jnp.exp(sums[2 * TILE:3 * TILE])).astype(BF16)
    q_off = [(q * jnp.exp(sums[(3 + j) * TILE:(4 + j) * TILE])).astype(BF16) for j in range(N_SUB - 1)]
    k_off = [jnp.where(sub_of_row == j, k_end, 0.0).astype(BF16) for j in range(N_SUB - 1)]
    q_t = _transpose_tiles(q)
    k_t = _transpose_tiles(k)
    g_t = _transpose_tiles(jnp.exp(lg))
    outs = []
    for h in range(n_heads):
        ks = slice(h * dk, (h + 1) * dk)
        tile = (h * dk) // LANES
        ls = slice((h * dk) % LANES, (h * dk) % LANES + dk)
        vb = v[:, h * dv:(h + 1) * dv].astype(BF16)
        a_off = _dot_nt(q_off[0][:, ks], k_off[0][:, ks])
        for j in range(1, N_SUB - 1):
            a_off += _dot_nt(q_off[j][:, ks], k_off[j][:, ks])
        part = _diag_partial_sums(q_t[ks], k_t[ks], g_t[ks])
        p_hi, p_lo = _split_bf16(part)
        skew = _dot_tn(p_hi, place) + _dot_tn(p_lo, place)
        a_diag = pltpu.roll(skew, 0, axis=1, stride=1, stride_axis=0)
        a = jnp.where(diag_ok, a_diag, jnp.where(off_ok, a_off, 0.0)).astype(BF16)
        o = _dot(a, vb)
        st = st_ref[tile][:, ls]
        inter = []
        for c in range(TILE // CHUNK):
            rs = slice(c * CHUNK, (c + 1) * CHUNK)
            inter.append(_dot_nt(q_state[rs, ks], st.astype(BF16)))
            last = cb[c * CHUNK + CHUNK - 1:(c + 1) * CHUNK, ks]
            st = st * jnp.exp(last) + _dot_tn(vb[rs], k_last[rs, ks])
        st_ref[tile, :, ls] = st
        outs.append(o + jnp.concatenate(inter, axis=0))
    return outs


def _rows_from(cb, row, spans):
    out = None
    for lo, src in spans:
        val = jnp.zeros_like(cb[0:1]) if src is None else cb[src:src + 1]
        out = val if out is None else jnp.where(row >= lo, val, out)
    return out


def _block_prefix(cb, block):
    if block == CHUNK:
        return cb
    row = lax.broadcasted_iota(jnp.int32, (TILE, 1), 0)
    spans = [(b0, None if b0 % CHUNK == 0 else b0 - 1) for b0 in range(0, TILE, block)]
    return cb - _rows_from(cb, row, spans)


def _gated_tile_bounded(q, k, cb, pb, v, st_ref, consts, n_heads, dk, dv, block):
    score_masks = consts[-1]
    row = lax.broadcasted_iota(jnp.int32, (TILE, 1), 0)
    ends = [cb[c * CHUNK + CHUNK - 1:(c + 1) * CHUNK, :] for c in range(TILE // CHUNK)]
    end_of_row = _rows_from(cb, row, [(c * CHUNK, c * CHUNK + CHUNK - 1) for c in range(TILE // CHUNK)])
    q_state = q * jnp.exp(cb)
    k_last = k * jnp.exp(end_of_row - cb)
    q_parts, k_parts = [], []
    for hs in _levels(block):
        ref = _rows_from(cb, row, [(b0, b0 + hs - 1) for b0 in range(0, TILE, 2 * hs)])
        lower = (row & (2 * hs - 1)) < hs
        x = jnp.exp(jnp.where(lower, ref - cb, cb - ref))
        q_parts.append(jnp.where(lower, 0.0, q * x))
        k_parts.append(jnp.where(lower, k * x, 0.0).astype(BF16))
    q_parts.append(q_state if block == CHUNK else q * jnp.exp(pb))
    k_parts.append((k * jnp.exp(-pb)).astype(BF16))
    outs = [None] * n_heads
    heads_per_tile = max(LANES // dk, 1)
    for tile in range(n_heads * dk // LANES):
        ls = slice(tile * LANES, (tile + 1) * LANES)
        heads = range(tile * heads_per_tile, (tile + 1) * heads_per_tile)
        qh, kl, vb, intra = {}, {}, {}, {}
        for h in heads:
            _, mask = _head_lanes(h, dk)
            own = (lambda a: a) if mask is None else (lambda a: jnp.where(mask, a, 0.0))
            qh[h] = own(q_state[:, ls]).astype(BF16)
            kl[h] = own(k_last[:, ls]).astype(BF16)
            vb[h] = v[:, h * dv:(h + 1) * dv].astype(BF16)
            a = 0.0
            for qp, kp, ok in zip(q_parts, k_parts, score_masks):
                a = jnp.where(ok, _dot_nt(own(qp[:, ls]).astype(BF16), kp[:, ls]), a)
            intra[h] = _dot(a.astype(BF16), vb[h])
        st = st_ref[tile]
        inter = {h: [] for h in heads}
        for c in range(TILE // CHUNK):
            rs = slice(c * CHUNK, (c + 1) * CHUNK)
            stb = st.astype(BF16)
            upd = None
            for h in heads:
                inter[h].append(_dot_nt(qh[h][rs], stb))
                u = _dot_tn(vb[h][rs], kl[h][rs])
                upd = u if upd is None else upd + u
            st = st * jnp.exp(ends[c][:, ls]) + upd
        st_ref[tile] = st
        for h in heads:
            outs[h] = intra[h] + jnp.concatenate(inter[h], axis=0)
    return outs


SAFE_DECAY = 64.0


def _gated_tile(q, k, lg, v, st_ref, consts, n_heads, dk, dv, block, finish):
    pieces = _split3_bf16(lg)
    cb = _cum_dot(consts[0][0:TILE], pieces)
    pb = _block_prefix(cb, block)
    bounded = jnp.min(pb) >= -SAFE_DECAY

    @pl.when(bounded)
    def _():
        finish(_gated_tile_bounded(q, k, cb, pb, v, st_ref, consts, n_heads, dk, dv, block))

    @pl.when(jnp.logical_not(bounded))
    def _():
        finish(_gated_tile_robust(q, k, lg, pieces, v, st_ref, consts, n_heads, dk, dv))


def _gated_step(lg_of, qkv_of, st_ref, cb_ref, consts, n_heads, dk, dv, block, n_tiles, finish):
    worst = None
    for c in range(n_tiles):
        rows = slice(c * TILE, (c + 1) * TILE)
        cb = _cum_dot(consts[0][0:TILE], _split3_bf16(lg_of(rows)))
        cb_ref[rows, :] = cb
        pb = _block_prefix(cb, block)
        worst = pb if worst is None else jnp.minimum(worst, pb)
    all_bounded = jnp.min(worst) >= -SAFE_DECAY

    def tile_rows(c):
        return pl.ds(pl.multiple_of(c * TILE, TILE), TILE)

    @pl.when(all_bounded)
    def _():
        def body(c, carry):
            rows = tile_rows(c)
            q, k, v = qkv_of(rows)
            cb = cb_ref[rows, :]
            pb = _block_prefix(cb, block)
            finish(rows, _gated_tile_bounded(q, k, cb, pb, v, st_ref, consts, n_heads, dk, dv, block))
            return carry

        lax.fori_loop(0, n_tiles, body, 0)

    @pl.when(jnp.logical_not(all_bounded))
    def _():
        def body(c, carry):
            rows = tile_rows(c)
            q, k, v = qkv_of(rows)
            _gated_tile(q, k, lg_of(rows), v, st_ref, consts, n_heads, dk, dv, block,
                        functools.partial(finish, rows))
            return carry

        lax.fori_loop(0, n_tiles, body, 0)


def _gla_kernel(q_ref, k_ref, v_ref, g_ref, r_ref, w2_ref, b2_ref, nw_ref, o_ref, st_ref, cb_ref, *, n_tiles):
    @pl.when(pl.program_id(1) == 0)
    def _():
        st_ref[...] = jnp.zeros_like(st_ref)

    consts = _gate_constants(GLA_BLOCK)
    w2 = w2_ref[...]
    b2 = b2_ref[...]
    nw = nw_ref[...]

    def lg_of(rows):
        pre = _dot3(_f32(r_ref[rows, :]), w2) + b2
        return _log_sigmoid(pre) * (1.0 / GLA_GATE_NORMALIZER)

    def qkv_of(rows):
        return _f32(q_ref[rows, :]) * (GLA_DK ** -0.5), _f32(k_ref[rows, :]), _f32(v_ref[rows, :])

    def finish(rows, outs):
        g = _f32(g_ref[rows, :])
        for h, o in enumerate(outs):
            vs = slice(h * GLA_DV, (h + 1) * GLA_DV)
            o_ref[rows, vs] = ((_rms(o) * nw) * _silu(g[:, vs])).astype(BF16)

    _gated_step(lg_of, qkv_of, st_ref, cb_ref, consts, GLA_HEADS, GLA_DK, GLA_DV, GLA_BLOCK, n_tiles, finish)


def _hgrn_kernel(q_ref, f_ref, i_ref, g_ref, lb_ref, nw_ref, o_ref, st_ref, cb_ref, key_ref, *, n_tiles, layer):
    @pl.when(pl.program_id(1) == 0)
    def _():
        st_ref[...] = jnp.zeros_like(st_ref)

    consts = _gate_constants(HG_BLOCK)
    lbv = lb_ref[...]
    e = jnp.exp(lbv - jnp.max(lbv, axis=0, keepdims=True))
    soft = e / jnp.sum(e, axis=0, keepdims=True)
    cum = soft[0:1]
    for i in range(1, layer + 1):
        cum = cum + soft[i:i + 1]
    lb = cum - soft[0:1]
    nw = nw_ref[...]

    def lg_of(rows):
        f = lb + (1.0 - lb) * _sigmoid(_f32(f_ref[rows, :]))
        key_ref[rows, :] = 1.0 - f
        return jnp.log(f)

    def qkv_of(rows):
        q = _silu(_f32(q_ref[rows, :])) * (HG_DK ** -0.5)
        return q, key_ref[rows, :], _f32(i_ref[rows, :])

    def finish(rows, outs):
        g = _f32(g_ref[rows, :])
        for h, o in enumerate(outs):
            hs = slice(h * HG_DK, (h + 1) * HG_DK)
            o_ref[rows, hs] = ((_rms(o) * nw) * _silu(g[:, hs])).astype(BF16)

    _gated_step(lg_of, qkv_of, st_ref, cb_ref, consts, HG_HEADS, HG_DK, HG_DK, HG_BLOCK, n_tiles, finish)


def _col_spec(t, width, name, nt):
    blk = _DST[name] // width
    return pl.BlockSpec((t, width), lambda b, i: (b * nt + i, blk))


def _mixer_out_spec(t, nt):
    return pl.BlockSpec((t, GROUP_WIDTH), lambda b, i: (b * nt + i, 0))


def _mixer_params():
    return pltpu.CompilerParams(dimension_semantics=("arbitrary", "arbitrary"),
                                vmem_limit_bytes=VMEM_LIMIT)


def _gla(proj, w2_pad, b2, nw, batch, seq, t=1024):
    nt = seq // t
    full = lambda shape: pl.BlockSpec(shape, lambda b, i: (0, 0))
    return pl.pallas_call(
        functools.partial(_gla_kernel, n_tiles=t // TILE),
        out_shape=jax.ShapeDtypeStruct((batch * seq, GROUP_WIDTH), BF16),
        grid=(batch, nt),
        in_specs=[_col_spec(t, 256, "a_q", nt), _col_spec(t, 256, "a_k", nt),
                  _col_spec(t, 512, "a_v", nt), _col_spec(t, 512, "a_g", nt),
                  _col_spec(t, LANES, "a_r", nt),
                  full((LANES, GLA_HEADS * GLA_DK)), full((1, GLA_HEADS * GLA_DK)), full((1, GLA_DV))],
        out_specs=_mixer_out_spec(t, nt),
        scratch_shapes=[pltpu.VMEM((GLA_HEADS * GLA_DK // LANES, GLA_DV, LANES), F32),
                        pltpu.VMEM((t, GLA_HEADS * GLA_DK), F32)],
        compiler_params=_mixer_params(),
        name="gla_mixer",
    )(proj, proj, proj, proj, proj, w2_pad, b2.reshape(1, -1), nw.reshape(1, -1))


def _hgrn(proj, hgrn_lb, nw, layer, batch, seq, t=1024):
    nt = seq // t
    full = lambda shape: pl.BlockSpec(shape, lambda b, i: (0, 0))
    return pl.pallas_call(
        functools.partial(_hgrn_kernel, n_tiles=t // TILE, layer=layer),
        out_shape=jax.ShapeDtypeStruct((batch * seq, GROUP_WIDTH), BF16),
        grid=(batch, nt),
        in_specs=[_col_spec(t, 512, "d_q", nt), _col_spec(t, 512, "d_f", nt),
                  _col_spec(t, 512, "d_i", nt), _col_spec(t, 512, "d_g", nt),
                  full(hgrn_lb.shape), full((1, HG_DK))],
        out_specs=_mixer_out_spec(t, nt),
        scratch_shapes=[pltpu.VMEM((HG_HEADS * HG_DK // LANES, HG_DK, LANES), F32),
                        pltpu.VMEM((t, HG_HEADS * HG_DK), F32), pltpu.VMEM((t, HG_HEADS * HG_DK), F32)],
        compiler_params=_mixer_params(),
        name="hgrn_mixer",
    )(proj, proj, proj, proj, hgrn_lb, nw.reshape(1, -1))


def _ret_kernel(q_ref, k_ref, v_ref, g_ref, cos_ref, sin_ref, o_ref, st_ref, *, t):
    @pl.when(pl.program_id(1) == 0)
    def _():
        st_ref[...] = jnp.zeros_like(st_ref)

    cos = cos_ref[0]
    sin = sin_ref[0]
    q = _f32(q_ref[...])
    k = _f32(k_ref[...])
    v = v_ref[...]
    g = _f32(g_ref[...])
    rowm = lax.broadcasted_iota(jnp.int32, (t, t), 0)
    colm = lax.broadcasted_iota(jnp.int32, (t, t), 1)
    rel = (rowm - colm).astype(F32)
    pos = lax.broadcasted_iota(jnp.int32, (t, RET_DK), 0).astype(F32)
    half = RET_DK // 2
    for h in range(RET_HEADS):
        lg = math.log1p(-(2.0 ** (-5.0 - h)))
        hs = slice(h * RET_DK, (h + 1) * RET_DK)
        qh = q[:, hs]
        kh = k[:, hs]
        qr = qh * cos + pltpu.roll(qh, half, axis=1) * sin
        kr = (kh * cos + pltpu.roll(kh, half, axis=1) * sin) * (RET_DK ** -0.5)
        vb = v[:, hs].astype(BF16)
        dmask = jnp.where(rel >= 0, jnp.exp(jnp.minimum(lg * rel, 0.0)), 0.0)
        scores = _dot_nt(qr.astype(BF16), kr.astype(BF16)) * dmask
        st = st_ref[h]
        xi = jnp.exp(lg * (pos + 1.0))
        o = _dot(scores.astype(BF16), vb) + _dot((qr * xi).astype(BF16), st.astype(BF16))
        zeta = jnp.exp(lg * (t - 1.0 - pos))
        st_ref[h] = st * math.exp(lg * t) + _dot_tn((kr * zeta).astype(BF16), vb)
        o_ref[:, hs] = (_rms(o) * _silu(g[:, hs])).astype(BF16)


def _ret(proj, cos, sin, batch, seq, t=256):
    nt = seq // t
    tab = pl.BlockSpec((1, t, RET_DK), lambda b, i: (b, i, 0))
    return pl.pallas_call(
        functools.partial(_ret_kernel, t=t),
        out_shape=jax.ShapeDtypeStruct((batch * seq, GROUP_WIDTH), BF16),
        grid=(batch, nt),
        in_specs=[_col_spec(t, 512, "b_q", nt), _col_spec(t, 512, "b_k", nt),
                  _col_spec(t, 512, "b_v", nt), _col_spec(t, 512, "b_g", nt), tab, tab],
        out_specs=_mixer_out_spec(t, nt),
        scratch_shapes=[pltpu.VMEM((RET_HEADS, RET_DK, RET_DK), F32)],
        compiler_params=_mixer_params(),
        name="ret_mixer",
    )(proj, proj, proj, proj, cos, sin)


def _rope_partial(x, c, s_lo, s_hi):
    n = x.shape[-1]
    half = ROPE_DIM // 2
    return x * c + pltpu.roll(x, n - half, axis=1) * s_lo + pltpu.roll(x, half, axis=1) * s_hi


def _swa_window(q_of, kc_raw, vc, tabs, has_prev, sinks, kprev_ref, vprev_ref, store):
    w = WINDOW
    lane = lax.broadcasted_iota(jnp.int32, (1, LANES), 1)
    lo_half = lane < SWA_HD
    c, s_lo, s_hi = tabs
    kc = _rope_partial(kc_raw, c, s_lo, s_hi)
    group = SWA_Q_HEADS // SWA_KV_HEADS
    rowm = lax.broadcasted_iota(jnp.int32, (group * w, w), 0) & (w - 1)
    colm = lax.broadcasted_iota(jnp.int32, (group * w, w), 1)
    cur_ok = colm <= rowm
    prev_ok = (colm > rowm) & has_prev

    def both_halves(a, kv):
        swapped = pltpu.roll(a, SWA_HD, axis=1)
        return jnp.where(lo_half, a, swapped) if kv == 0 else jnp.where(lo_half, swapped, a)

    for kv in range(SWA_KV_HEADS):
        k2c = both_halves(kc, kv).astype(BF16)
        k2p = kprev_ref[kv]
        v2c = both_halves(vc, kv)
        v_half = [(jnp.where(lo_half, v2c, jnp.where(lane == SWA_HD, 1.0, 0.0)).astype(BF16), vprev_ref[kv, 0]),
                  (jnp.where(lo_half, jnp.where(lane == 0, 1.0, 0.0), v2c).astype(BF16), vprev_ref[kv, 1])]
        ones_lane = (SWA_HD, 0)
        kprev_ref[kv] = k2c
        vprev_ref[kv, 0] = v_half[0][0]
        vprev_ref[kv, 1] = v_half[1][0]
        q_rows, sink_rows = [], []
        for t in range(kv * group // 2, (kv + 1) * group // 2):
            qt = _rope_partial(q_of(t), c, s_lo, s_hi) * (SWA_HD ** -0.5)
            q_rows += [jnp.where(lo_half, qt, 0.0), jnp.where(lo_half, 0.0, qt)]
            sink_rows += [jnp.broadcast_to(sinks[:, 2 * t:2 * t + 1], (w, 1)),
                          jnp.broadcast_to(sinks[:, 2 * t + 1:2 * t + 2], (w, 1))]
        q4 = jnp.concatenate(q_rows, axis=0).astype(BF16)
        sink = jnp.concatenate(sink_rows, axis=0)
        s_c = jnp.where(cur_ok, _dot_nt(q4, k2c), -jnp.inf)
        s_p = jnp.where(prev_ok, _dot_nt(q4, k2p), -jnp.inf)
        m = jnp.maximum(jnp.max(jnp.maximum(s_c, s_p), axis=-1, keepdims=True), sink)
        e_c = jnp.exp(s_c - m).astype(BF16)
        e_p = jnp.exp(s_p - m).astype(BF16)
        e_sink = jnp.exp(sink - m)
        for n, t in enumerate(range(kv * group // 2, (kv + 1) * group // 2)):
            halves = []
            for half in range(2):
                rs = slice((2 * n + half) * w, (2 * n + half + 1) * w)
                vh_c, vh_p = v_half[half]
                o = _dot(e_c[rs], vh_c) + _dot(e_p[rs], vh_p)
                denom = o[:, ones_lane[half]:ones_lane[half] + 1] + e_sink[rs]
                halves.append(o * (1.0 / denom))
            store(t, jnp.where(lo_half, halves[0], halves[1]).astype(BF16))


def _swa_kernel(q_ref, kc_ref, vc_ref, c_ref, slo_ref, shi_ref, sink_ref, o_ref, kprev_ref, vprev_ref, *, t):
    first = pl.program_id(1) == 0

    @pl.when(first)
    def _():
        kprev_ref[...] = jnp.zeros_like(kprev_ref)
        vprev_ref[...] = jnp.zeros_like(vprev_ref)

    sinks = sink_ref[...]
    for i in range(t // WINDOW):
        rows = slice(i * WINDOW, (i + 1) * WINDOW)
        has_prev = jnp.logical_not(first) if i == 0 else True

        def store(tile, value, rows=rows):
            o_ref[rows, tile * LANES:(tile + 1) * LANES] = value

        _swa_window(lambda tile, rows=rows: _f32(q_ref[rows, tile * LANES:(tile + 1) * LANES]),
                    _f32(kc_ref[rows, :]), _f32(vc_ref[rows, :]),
                    (c_ref[0, rows], slo_ref[0, rows], shi_ref[0, rows]), has_prev, sinks,
                    kprev_ref, vprev_ref, store)


def _swa(proj, tabs, sinks, batch, seq, t=512):
    nt = seq // t
    kv_w = SWA_KV_HEADS * SWA_HD
    cur = lambda name: pl.BlockSpec((t, kv_w), lambda b, i: (b * nt + i, _DST[name] // kv_w))
    tab = pl.BlockSpec((1, t, LANES), lambda b, i: (b, i, 0))
    return pl.pallas_call(
        functools.partial(_swa_kernel, t=t),
        out_shape=jax.ShapeDtypeStruct((batch * seq, GROUP_WIDTH), BF16),
        grid=(batch, nt),
        in_specs=[_col_spec(t, 512, "c_q", nt), cur("c_k"), cur("c_v"), tab, tab, tab,
                  pl.BlockSpec((1, SWA_Q_HEADS), lambda b, i: (0, 0))],
        out_specs=_mixer_out_spec(t, nt),
        scratch_shapes=[pltpu.VMEM((SWA_KV_HEADS, WINDOW, LANES), BF16),
                        pltpu.VMEM((SWA_KV_HEADS, 2, WINDOW, LANES), BF16)],
        compiler_params=_mixer_params(),
        name="swa_mixer",
    )(proj, proj, proj, *tabs, sinks.reshape(1, -1))


def _out_proj_kernel(x_ref, mod_ref, a_ref, b_ref, c_ref, d_ref, w_ref, o_ref):
    gw = GROUP_WIDTH
    acc = _dot(a_ref[...], w_ref[0:gw, :])
    acc += _dot(b_ref[...], w_ref[gw:2 * gw, :])
    acc += _dot(c_ref[...], w_ref[2 * gw:3 * gw, :])
    acc += _dot(d_ref[...], w_ref[3 * gw:4 * gw, :])
    gate = mod_ref[0][2:3]
    o_ref[...] = x_ref[...] + gate * acc


def _out_proj(x2, mod_l, mixed, w_out, layer, seq, tm=512):
    m_rows, d = x2.shape
    per_b = seq // tm
    grp = pl.BlockSpec((tm, GROUP_WIDTH), lambda i: (i, 0))
    return pl.pallas_call(
        _out_proj_kernel,
        out_shape=jax.ShapeDtypeStruct((m_rows, d), F32),
        grid=(m_rows // tm,),
        in_specs=[pl.BlockSpec((tm, d), lambda i: (i, 0)),
                  pl.BlockSpec((1, 6, d), lambda i: (i // per_b, 0, 0)),
                  grp, grp, grp, grp,
                  pl.BlockSpec((None, d, d), lambda i: (layer, 0, 0))],
        out_specs=pl.BlockSpec((tm, d), lambda i: (i, 0)),
        compiler_params=pltpu.CompilerParams(
            dimension_semantics=("arbitrary",), vmem_limit_bytes=VMEM_LIMIT),
        name="out_proj",
    )(x2, mod_l, *mixed, w_out)


def _ffn_kernel(x_ref, mod_ref, nw_ref, wg_ref, wu_ref, wd_ref, fw_ref, o_ref, h_ref, acc_ref, *, final):
    f = pl.program_id(1)

    @pl.when(f == 0)
    def _():
        m = mod_ref[0]
        h = _modulated_norm(x_ref[...], nw_ref[...], m[3:4], m[4:5])
        h_ref[...] = h.astype(BF16)
        acc_ref[...] = jnp.zeros_like(acc_ref)

    h = h_ref[...]
    g = _dot(h, wg_ref[...])
    u = _dot(h, wu_ref[...])
    act = (_silu(g) * u).astype(BF16)
    acc_ref[...] += _dot(act, wd_ref[...])

    @pl.when(f == pl.num_programs(1) - 1)
    def _():
        y = x_ref[...] + mod_ref[0][5:6] * acc_ref[...]
        o_ref[...] = _rms(y) * fw_ref[...] if final else y


def _ffn(x2, mod_l, nw, w_in, w_down, layer, final_w, final, seq, tm=512, tf=512):
    m_rows, d = x2.shape
    nf = D_FF // tf
    per_b = seq // tm
    return pl.pallas_call(
        functools.partial(_ffn_kernel, final=final),
        out_shape=jax.ShapeDtypeStruct((m_rows, d), F32),
        grid=(m_rows // tm, nf),
        in_specs=[pl.BlockSpec((tm, d), lambda i, f: (i, 0)),
                  pl.BlockSpec((1, 6, d), lambda i, f: (i // per_b, 0, 0)),
                  pl.BlockSpec((1, d), lambda i, f: (0, 0)),
                  pl.BlockSpec((None, d, tf), lambda i, f: (layer, 0, f)),
                  pl.BlockSpec((None, d, tf), lambda i, f: (layer, 0, nf + f)),
                  pl.BlockSpec((None, tf, d), lambda i, f: (layer, f, 0)),
                  pl.BlockSpec((1, d), lambda i, f: (0, 0))],
        out_specs=pl.BlockSpec((tm, d), lambda i, f: (i, 0)),
        scratch_shapes=[pltpu.VMEM((tm, d), BF16), pltpu.VMEM((tm, d), F32)],
        compiler_params=pltpu.CompilerParams(
            dimension_semantics=("arbitrary", "arbitrary"), vmem_limit_bytes=VMEM_LIMIT),
        name="ffn",
    )(x2, mod_l, nw.reshape(1, d), w_in, w_in, w_down, final_w.reshape(1, d))


CAST_ROWS = 256
CAST_COLS = 2816


def _cast_kernel(x_ref, o_ref):
    o_ref[...] = x_ref[...].astype(BF16)


def _to_bf16(w):
    depth, k, n = w.shape
    tn = n if n <= CAST_COLS else CAST_COLS
    blk = pl.BlockSpec((1, CAST_ROWS, tn), lambda l, i, j: (l, i, j))
    return pl.pallas_call(
        _cast_kernel,
        out_shape=jax.ShapeDtypeStruct(w.shape, BF16),
        grid=(depth, k // CAST_ROWS, n // tn),
        in_specs=[blk], out_specs=blk,
        compiler_params=pltpu.CompilerParams(
            dimension_semantics=("arbitrary",) * 3, vmem_limit_bytes=VMEM_LIMIT),
        name="cast_weights",
    )(w)


def _permute_kernel(x_ref, o_ref):
    off = 0
    for name, width in _DST_ORDER:
        o_ref[0, off:off + width, :] = x_ref[0, _SRC[name]:_SRC[name] + width, :].astype(BF16)
        off += width
    o_ref[0, off:, :] = jnp.zeros((PROJ_WIDTH - off, o_ref.shape[2]), BF16)


def _permute_w_in(w):
    w_t = jnp.swapaxes(w, 1, 2)
    depth, n, k = w_t.shape
    return pl.pallas_call(
        _permute_kernel,
        out_shape=jax.ShapeDtypeStruct((depth, PROJ_WIDTH, k), BF16),
        grid=(depth, k // CAST_ROWS),
        in_specs=[pl.BlockSpec((1, n, CAST_ROWS), lambda l, i: (l, 0, i))],
        out_specs=pl.BlockSpec((1, PROJ_WIDTH, CAST_ROWS), lambda l, i: (l, 0, i)),
        compiler_params=pltpu.CompilerParams(
            dimension_semantics=("arbitrary", "arbitrary"), vmem_limit_bytes=VMEM_LIMIT),
        name="permute_w_in",
    )(w_t)


def _ret_inv_freq():
    inv = 1.0 / jnp.power(RET_ROT_BASE, jnp.linspace(0.0, 1.0, RET_DK // 2, dtype=F32))
    return jnp.concatenate([inv, inv])[None, :]


def _swa_inv_freq():
    half = ROPE_DIM // 2
    inv = 1.0 / jnp.power(ROPE_THETA, jnp.arange(half, dtype=F32) / half)
    per_head = jnp.concatenate([inv, inv, jnp.zeros((SWA_HD - ROPE_DIM,), F32)])
    return jnp.concatenate([per_head] * (LANES // SWA_HD))[None, :]


def _rotary_tables_kernel(pos_ref, rinv_ref, sinv_ref, rcos_ref, rsin_ref, c_ref, slo_ref, shi_ref):
    lane = lax.broadcasted_iota(jnp.int32, (1, LANES), 1)
    pos = pos_ref[0]
    ang = pos * rinv_ref[...]
    sin = jnp.sin(ang)
    rcos_ref[0] = jnp.cos(ang)
    rsin_ref[0] = jnp.where(lane < RET_DK // 2, -sin, sin)
    ang = pos * sinv_ref[...]
    sin = jnp.sin(ang)
    in_head = lane & (SWA_HD - 1)
    c_ref[0] = jnp.cos(ang)
    slo_ref[0] = jnp.where(in_head < ROPE_DIM // 2, -sin, 0.0)
    shi_ref[0] = jnp.where((in_head >= ROPE_DIM // 2) & (in_head < ROPE_DIM), sin, 0.0)


def _rotary_tables(pos, ret_inv, swa_inv, t=512):
    batch, seq, _ = pos.shape
    tab = pl.BlockSpec((1, t, LANES), lambda b, i: (b, i, 0))
    inv = pl.BlockSpec((1, LANES), lambda b, i: (0, 0))
    return pl.pallas_call(
        _rotary_tables_kernel,
        out_shape=[jax.ShapeDtypeStruct((batch, seq, LANES), F32)] * 5,
        grid=(batch, seq // t),
        in_specs=[pl.BlockSpec((1, t, 1), lambda b, i: (b, i, 0)), inv, inv],
        out_specs=[tab] * 5,
        compiler_params=_mixer_params(),
        name="rotary_tables",
    )(pos, ret_inv, swa_inv)


def kernel(x, c, positions, w_ada, b_ada, norm1_w, w_in, gla_gate_w2, gla_gate_b2, gla_norm_w, swa_sinks, hgrn_lb, hgrn_norm_w, w_out, norm2_w, w_ffn_in, w_ffn_down, final_norm_w):
    batch, seq, d = x.shape
    depth = w_ada.shape[0]
    mod = _ada_modulation(c, w_ada, b_ada).reshape(depth, batch, 6, d)
    ret_cos, ret_sin, *swa_tabs = _rotary_tables(positions.astype(F32)[:, :, None], _ret_inv_freq(), _swa_inv_freq())
    x2 = x.reshape(batch * seq, d)
    w_in_b, w_out_b = _permute_w_in(w_in), _to_bf16(w_out)
    w_ffn_in_b, w_ffn_down_b = _to_bf16(w_ffn_in), _to_bf16(w_ffn_down)
    for l in range(depth):
        w2_pad = jnp.zeros((LANES, GLA_HEADS * GLA_DK), F32).at[:GLA_GATE_RANK].set(gla_gate_w2[l])
        proj = _in_proj(x2, mod[l], norm1_w[l], w_in_b, l, seq)
        mixed = (
            _gla(proj, w2_pad, gla_gate_b2[l], gla_norm_w[l], batch, seq),
            _ret(proj, ret_cos, ret_sin, batch, seq),
            _swa(proj, swa_tabs, swa_sinks[l], batch, seq),
            _hgrn(proj, hgrn_lb, hgrn_norm_w[l], l, batch, seq),
        )
        x2 = _out_proj(x2, mod[l], mixed, w_out_b, l, seq)
        x2 = _ffn(x2, mod[l], norm2_w[l], w_ffn_in_b, w_ffn_down_b, l, final_norm_w, l == depth - 1, seq)
    return x2.reshape(batch, seq, d)
```

```python
import functools
import math

import jax
import jax.numpy as jnp
from jax import lax
from jax.experimental import pallas as pl
from jax.experimental.pallas import tpu as pltpu

F32 = jnp.float32
BF16 = jnp.bfloat16

D_MODEL = 2048
DEPTH = 2
EPS = 1e-6
GROUP_WIDTH = D_MODEL // 4
CHUNK = 64
SUB = 16
N_SUB = CHUNK // SUB
GLA_HEADS = 4
GLA_DV = GROUP_WIDTH // GLA_HEADS
GLA_DK = GLA_DV // 2
GLA_GATE_RANK = 16
GLA_GATE_NORMALIZER = 16.0
RET_HEADS = 4
RET_DK = GROUP_WIDTH // RET_HEADS
RET_ROT_BASE = 10000.0
SWA_Q_HEADS = 8
SWA_KV_HEADS = 2
SWA_HD = GROUP_WIDTH // SWA_Q_HEADS
WINDOW = 128
ROPE_THETA = 500000.0
ROPE_DIM = SWA_HD // 4
HG_HEADS = 4
HG_DK = GROUP_WIDTH // HG_HEADS
D_FF = ((8 * D_MODEL + 3 * 256 - 1) // (3 * 256)) * 256

LANES = 128
VMEM_LIMIT = 48 * 1024 * 1024
VMEM_LIMIT_IN_PROJ = 56 * 1024 * 1024

_SRC = dict(a_q=0, a_k=256, a_v=512, a_g=1024, a_r=1536,
            b_q=1552, b_k=2064, b_v=2576, b_g=3088,
            c_q=3600, c_k=4112, c_v=4240,
            d_q=4368, d_f=4880, d_i=5392, d_g=5904)
_DST_ORDER = (("a_q", 256), ("a_k", 256), ("a_v", 512), ("a_g", 512), ("c_q", 512),
              ("b_q", 512), ("b_k", 512), ("b_v", 512), ("b_g", 512),
              ("d_q", 512), ("d_f", 512), ("d_i", 512), ("d_g", 512),
              ("c_k", 128), ("c_v", 128), ("a_r", GLA_GATE_RANK))
PROJ_WIDTH = 6528
PROJ_TN = PROJ_WIDTH // 3


def _dst_offsets():
    offs, o = {}, 0
    for name, w in _DST_ORDER:
        offs[name] = o
        o += w
    return offs


_DST = _dst_offsets()


def _dot(a, b):
    return jnp.dot(a, b, preferred_element_type=F32)


def _dot_nt(a, b):
    return lax.dot_general(a, b, (((1,), (1,)), ((), ())), preferred_element_type=F32)


def _dot_tn(a, b):
    return lax.dot_general(a, b, (((0,), (0,)), ((), ())), preferred_element_type=F32)


def _split_bf16(x):
    hi = x.astype(BF16)
    lo = (x - hi.astype(F32)).astype(BF16)
    return hi, lo


def _dot3(a, b):
    a_hi, a_lo = _split_bf16(a)
    b_hi, b_lo = _split_bf16(b)
    return _dot(a_hi, b_hi) + (_dot(a_hi, b_lo) + _dot(a_lo, b_hi))


def _sigmoid(x):
    return 1.0 / (1.0 + jnp.exp(-x))


def _silu(x):
    return (0.5 * x) * (1.0 + jnp.tanh(0.5 * x))


def _log_sigmoid(x):
    return jnp.minimum(x, 0.0) - jnp.log1p(jnp.exp(-jnp.abs(x)))


def _f32(x):
    return x.astype(F32)


def _rms(x):
    return x * lax.rsqrt(jnp.mean(x * x, axis=-1, keepdims=True) + EPS)


def _ada_kernel(c_ref, w_ref, b_ref, o_ref):
    cond = _silu(c_ref[...])
    o_ref[0] = _dot3(cond, w_ref[0]) + b_ref[0]


def _ada_modulation(c, w_ada, b_ada):
    depth, d, n = w_ada.shape
    rows = 8
    c_pad = jnp.zeros((rows, d), F32).at[:c.shape[0]].set(c)
    tn = 2048
    out = pl.pallas_call(
        _ada_kernel,
        out_shape=jax.ShapeDtypeStruct((depth, rows, n), F32),
        grid=(depth, n // tn),
        in_specs=[pl.BlockSpec((rows, d), lambda l, j: (0, 0)),
                  pl.BlockSpec((1, d, tn), lambda l, j: (l, 0, j)),
                  pl.BlockSpec((1, 1, tn), lambda l, j: (l, 0, j))],
        out_specs=pl.BlockSpec((1, rows, tn), lambda l, j: (l, 0, j)),
        compiler_params=pltpu.CompilerParams(
            dimension_semantics=("arbitrary", "arbitrary"), vmem_limit_bytes=VMEM_LIMIT),
        name="ada_modulation",
    )(c_pad, w_ada, b_ada.reshape(depth, 1, n))
    return out[:, :c.shape[0]]


def _modulated_norm(x, nw, shift, scale):
    return _rms(x) * (nw * (1.0 + scale)) + shift


def _in_proj_kernel(x_ref, mod_ref, nw_ref, w_ref, o_ref, h_ref):
    @pl.when(pl.program_id(1) == 0)
    def _():
        m = mod_ref[0]
        h = _modulated_norm(x_ref[...], nw_ref[...], m[0:1], m[1:2])
        h_ref[...] = h.astype(BF16)

    half = h_ref.shape[0] // 2
    for r0 in (0, half):
        o_ref[r0:r0 + half, :] = _dot_nt(h_ref[r0:r0 + half, :], w_ref[...]).astype(BF16)


def _in_proj(x2, mod_l, nw, w_perm, layer, seq, tm=1024):
    m_rows, d = x2.shape
    n = w_perm.shape[1]
    per_b = seq // tm
    return pl.pallas_call(
        _in_proj_kernel,
        out_shape=jax.ShapeDtypeStruct((m_rows, n), BF16),
        grid=(m_rows // tm, n // PROJ_TN),
        in_specs=[pl.BlockSpec((tm, d), lambda i, j: (i, 0)),
                  pl.BlockSpec((1, 6, d), lambda i, j: (i // per_b, 0, 0)),
                  pl.BlockSpec((1, d), lambda i, j: (0, 0)),
                  pl.BlockSpec((None, PROJ_TN, d), lambda i, j: (layer, j, 0))],
        out_specs=pl.BlockSpec((tm, PROJ_TN), lambda i, j: (i, j)),
        scratch_shapes=[pltpu.VMEM((tm, d), BF16)],
        compiler_params=pltpu.CompilerParams(
            dimension_semantics=("arbitrary", "arbitrary"), vmem_limit_bytes=VMEM_LIMIT_IN_PROJ),
        name="in_proj",
    )(x2, mod_l, nw.reshape(1, d), w_perm)


TILE = 2 * CHUNK
GLA_BLOCK = CHUNK
HG_BLOCK = CHUNK // 2
CHUNK_SHIFT = CHUNK.bit_length() - 1
SUB_SHIFT = SUB.bit_length() - 1
SUBLANES = 8
SLAB = 64
DIAG_RUN = 4
assert DIAG_RUN == 4


def _ones_where(mask):
    return jnp.where(mask, 1.0, 0.0).astype(BF16)


def _levels(block):
    return [hs for hs in (CHUNK // 2, CHUNK // 4) if hs >= block]


def _gate_constants(block):
    i = lax.broadcasted_iota(jnp.int32, (TILE, TILE), 0)
    m = lax.broadcasted_iota(jnp.int32, (TILE, TILE), 1)
    same_chunk = (i >> CHUNK_SHIFT) == (m >> CHUNK_SHIFT)
    sub_end = i | (SUB - 1)
    chunk_start = i & ~(CHUNK - 1)
    after = same_chunk & (m > i)
    mats = [same_chunk & (m <= i),
            after & (m <= sub_end),
            after]
    for j in range(N_SUB - 1):
        mats.append(same_chunk & (m > chunk_start + (SUB * j + SUB - 1)) & (m <= i))
    cum = jnp.concatenate([_ones_where(x) for x in mats], axis=0)
    place = _ones_where(m == ((TILE - (i >> 3)) & (TILE - 1)))
    diag_ok = ((i >> SUB_SHIFT) == (m >> SUB_SHIFT)) & (m <= i)
    off_ok = same_chunk & ((m >> SUB_SHIFT) < (i >> SUB_SHIFT))
    sub_of_row = (lax.broadcasted_iota(jnp.int32, (TILE, 1), 0) >> SUB_SHIFT) & (N_SUB - 1)
    score_masks = []
    for hs in _levels(block):
        same = (i >> (2 * hs).bit_length() - 1) == (m >> (2 * hs).bit_length() - 1)
        score_masks.append(same & ((i & (2 * hs - 1)) >= hs) & ((m & (2 * hs - 1)) < hs))
    score_masks.append(((i >> block.bit_length() - 1) == (m >> block.bit_length() - 1)) & (m <= i))
    return cum, place, diag_ok, off_ok, sub_of_row, score_masks


def _transpose_tiles(x):
    return jnp.concatenate([x[:, t:t + LANES].T for t in range(0, x.shape[1], LANES)], axis=0)


def _diag_partial_sums(q_t, k_t, g_t):
    dk = q_t.shape[0]
    slab = min(dk, SLAB)
    accs = [None] * SUB
    for s0 in range(0, dk, slab):
        qs = q_t[s0:s0 + slab]
        ks = k_t[s0:s0 + slab]
        g1 = g_t[s0:s0 + slab]
        g2 = g1 * pltpu.roll(g1, 1, axis=1)
        g4 = g2 * pltpu.roll(g2, 2, axis=1)
        g4_back = [g4] + [pltpu.roll(g4, DIAG_RUN * n, axis=1) for n in range(1, SUB // DIAG_RUN - 1)]
        for d0 in range(0, SUB, DIAG_RUN):
            w = ks
            if d0 > 0:
                w = pltpu.roll(ks, d0, axis=1)
                for n in range(d0 // DIAG_RUN):
                    w = w * g4_back[n]
            for d in range(d0, d0 + DIAG_RUN):
                if d > d0:
                    w = pltpu.roll(w, 1, axis=1) * g1
                p = jnp.sum((qs * w).reshape(slab // SUBLANES, SUBLANES, TILE), axis=0)
                accs[d] = p if accs[d] is None else accs[d] + p
    return jnp.concatenate(accs, axis=0)


def _split3_bf16(x):
    hi = x.astype(BF16)
    rest = x - hi.astype(F32)
    mid = rest.astype(BF16)
    lo = (rest - mid.astype(F32)).astype(BF16)
    return hi, mid, lo


def _cum_dot(mat, pieces):
    hi, mid, lo = pieces
    return _dot(mat, hi) + (_dot(mat, mid) + _dot(mat, lo))


def _head_lanes(h, dk):
    tile = (h * dk) // LANES
    if dk >= LANES:
        return tile, None
    lane = lax.broadcasted_iota(jnp.int32, (1, LANES), 1)
    start = (h * dk) % LANES
    return tile, (lane >= start) & (lane < start + dk)


def _gated_tile_robust(q, k, lg, lg_pieces, v, st_ref, consts, n_heads, dk, dv):
    cum, place, diag_ok, off_ok, sub_of_row, _ = consts
    sums = _cum_dot(cum, lg_pieces)
    cb = sums[0:TILE]
    q_state = (q * jnp.exp(cb)).astype(BF16)
    k_end = k * jnp.exp(sums[TILE:2 * TILE])
    k_last = (k * jnp.exp(sums[2 * TILE:3 * TILE])).astype(BF16)
    q_off = [(q * jnp.exp(sums[(3 + j) * TILE:(4 + j) * TILE])).astype(BF16) for j in range(N_SUB - 1)]
    k_off = [jnp.where(sub_of_row == j, k_end, 0.0).astype(BF16) for j in range(N_SUB - 1)]
    q_t = _transpose_tiles(q)
    k_t = _transpose_tiles(k)
    g_t = _transpose_tiles(jnp.exp(lg))
    outs = []
    for h in range(n_heads):
        ks = slice(h * dk, (h + 1) * dk)
        tile = (h * dk) // LANES
        ls = slice((h * dk) % LANES, (h * dk) % LANES + dk)
        vb = v[:, h * dv:(h + 1) * dv].astype(BF16)
        a_off = _dot_nt(q_off[0][:, ks], k_off[0][:, ks])
        for j in range(1, N_SUB - 1):
            a_off += _dot_nt(q_off[j][:, ks], k_off[j][:, ks])
        part = _diag_partial_sums(q_t[ks], k_t[ks], g_t[ks])
        p_hi, p_lo = _split_bf16(part)
        skew = _dot_tn(p_hi, place) + _dot_tn(p_lo, place)
        a_diag = pltpu.roll(skew, 0, axis=1, stride=1, stride_axis=0)
        a = jnp.where(diag_ok, a_diag, jnp.where(off_ok, a_off, 0.0)).astype(BF16)
        o = _dot(a, vb)
        st = st_ref[tile][:, ls]
        inter = []
        for c in range(TILE // CHUNK):
            rs = slice(c * CHUNK, (c + 1) * CHUNK)
            inter.append(_dot_nt(q_state[rs, ks], st.astype(BF16)))
            last = cb[c * CHUNK + CHUNK - 1:(c + 1) * CHUNK, ks]
            st = st * jnp.exp(last) + _dot_tn(vb[rs], k_last[rs, ks])
        st_ref[tile, :, ls] = st
        outs.append(o + jnp.concatenate(inter, axis=0))
    return outs


def _rows_from(cb, row, spans):
    out = None
    for lo, src in spans:
        val = jnp.zeros_like(cb[0:1]) if src is None else cb[src:src + 1]
        out = val if out is None else jnp.where(row >= lo, val, out)
    return out


def _block_prefix(cb, block):
    if block == CHUNK:
        return cb
    row = lax.broadcasted_iota(jnp.int32, (TILE, 1), 0)
    spans = [(b0, None if b0 % CHUNK == 0 else b0 - 1) for b0 in range(0, TILE, block)]
    return cb - _rows_from(cb, row, spans)


def _gated_tile_bounded(q, k, cb, pb, v, st_ref, consts, n_heads, dk, dv, block):
    score_masks = consts[-1]
    row = lax.broadcasted_iota(jnp.int32, (TILE, 1), 0)
    ends = [cb[c * CHUNK + CHUNK - 1:(c + 1) * CHUNK, :] for c in range(TILE // CHUNK)]
    end_of_row = _rows_from(cb, row, [(c * CHUNK, c * CHUNK + CHUNK - 1) for c in range(TILE // CHUNK)])
    q_state = q * jnp.exp(cb)
    k_last = k * jnp.exp(end_of_row - cb)
    q_parts, k_parts = [], []
    for hs in _levels(block):
        ref = _rows_from(cb, row, [(b0, b0 + hs - 1) for b0 in range(0, TILE, 2 * hs)])
        lower = (row & (2 * hs - 1)) < hs
        x = jnp.exp(jnp.where(lower, ref - cb, cb - ref))
        q_parts.append(jnp.where(lower, 0.0, q * x))
        k_parts.append(jnp.where(lower, k * x, 0.0).astype(BF16))
    q_parts.append(q_state if block == CHUNK else q * jnp.exp(pb))
    k_parts.append((k * jnp.exp(-pb)).astype(BF16))
    outs = [None] * n_heads
    heads_per_tile = max(LANES // dk, 1)
    for tile in range(n_heads * dk // LANES):
        ls = slice(tile * LANES, (tile + 1) * LANES)
        heads = range(tile * heads_per_tile, (tile + 1) * heads_per_tile)
        qh, kl, vb, intra = {}, {}, {}, {}
        for h in heads:
            _, mask = _head_lanes(h, dk)
            own = (lambda a: a) if mask is None else (lambda a: jnp.where(mask, a, 0.0))
            qh[h] = own(q_state[:, ls]).astype(BF16)
            kl[h] = own(k_last[:, ls]).astype(BF16)
            vb[h] = v[:, h * dv:(h + 1) * dv].astype(BF16)
            a = 0.0
            for qp, kp, ok in zip(q_parts, k_parts, score_masks):
                a = jnp.where(ok, _dot_nt(own(qp[:, ls]).astype(BF16), kp[:, ls]), a)
            intra[h] = _dot(a.astype(BF16), vb[h])
        st = st_ref[tile]
        inter = {h: [] for h in heads}
        for c in range(TILE // CHUNK):
            rs = slice(c * CHUNK, (c + 1) * CHUNK)
            stb = st.astype(BF16)
            upd = None
            for h in heads:
                inter[h].append(_dot_nt(qh[h][rs], stb))
                u = _dot_tn(vb[h][rs], kl[h][rs])
                upd = u if upd is None else upd + u
            st = st * jnp.exp(ends[c][:, ls]) + upd
        st_ref[tile] = st
        for h in heads:
            outs[h] = intra[h] + jnp.concatenate(inter[h], axis=0)
    return outs


SAFE_DECAY = 64.0


def _gated_tile(q, k, lg, v, st_ref, consts, n_heads, dk, dv, block, finish):
    pieces = _split3_bf16(lg)
    cb = _cum_dot(consts[0][0:TILE], pieces)
    pb = _block_prefix(cb, block)
    bounded = jnp.min(pb) >= -SAFE_DECAY

    @pl.when(bounded)
    def _():
        finish(_gated_tile_bounded(q, k, cb, pb, v, st_ref, consts, n_heads, dk, dv, block))

    @pl.when(jnp.logical_not(bounded))
    def _():
        finish(_gated_tile_robust(q, k, lg, pieces, v, st_ref, consts, n_heads, dk, dv))


def _gated_step(lg_of, qkv_of, st_ref, cb_ref, consts, n_heads, dk, dv, block, n_tiles, finish):
    worst = None
    for c in range(n_tiles):
        rows = slice(c * TILE, (c + 1) * TILE)
        cb = _cum_dot(consts[0][0:TILE], _split3_bf16(lg_of(rows)))
        cb_ref[rows, :] = cb
        pb = _block_prefix(cb, block)
        worst = pb if worst is None else jnp.minimum(worst, pb)
    all_bounded = jnp.min(worst) >= -SAFE_DECAY

    def tile_rows(c):
        return pl.ds(pl.multiple_of(c * TILE, TILE), TILE)

    @pl.when(all_bounded)
    def _():
        def body(c, carry):
            rows = tile_rows(c)
            q, k, v = qkv_of(rows)
            cb = cb_ref[rows, :]
            pb = _block_prefix(cb, block)
            finish(rows, _gated_tile_bounded(q, k, cb, pb, v, st_ref, consts, n_heads, dk, dv, block))
            return carry

        lax.fori_loop(0, n_tiles, body, 0)

    @pl.when(jnp.logical_not(all_bounded))
    def _():
        def body(c, carry):
            rows = tile_rows(c)
            q, k, v = qkv_of(rows)
            _gated_tile(q, k, lg_of(rows), v, st_ref, consts, n_heads, dk, dv, block,
                        functools.partial(finish, rows))
            return carry

        lax.fori_loop(0, n_tiles, body, 0)


def _gla_kernel(q_ref, k_ref, v_ref, g_ref, r_ref, w2_ref, b2_ref, nw_ref, o_ref, st_ref, cb_ref, *, n_tiles):
    @pl.when(pl.program_id(1) == 0)
    def _():
        st_ref[...] = jnp.zeros_like(st_ref)

    consts = _gate_constants(GLA_BLOCK)
    w2 = w2_ref[...]
    b2 = b2_ref[...]
    nw = nw_ref[...]

    def lg_of(rows):
        pre = _dot3(_f32(r_ref[rows, :]), w2) + b2
        return _log_sigmoid(pre) * (1.0 / GLA_GATE_NORMALIZER)

    def qkv_of(rows):
        return _f32(q_ref[rows, :]) * (GLA_DK ** -0.5), _f32(k_ref[rows, :]), _f32(v_ref[rows, :])

    def finish(rows, outs):
        g = _f32(g_ref[rows, :])
        for h, o in enumerate(outs):
            vs = slice(h * GLA_DV, (h + 1) * GLA_DV)
            o_ref[rows, vs] = ((_rms(o) * nw) * _silu(g[:, vs])).astype(BF16)

    _gated_step(lg_of, qkv_of, st_ref, cb_ref, consts, GLA_HEADS, GLA_DK, GLA_DV, GLA_BLOCK, n_tiles, finish)


def _hgrn_kernel(q_ref, f_ref, i_ref, g_ref, lb_ref, nw_ref, o_ref, st_ref, cb_ref, key_ref, *, n_tiles, layer):
    @pl.when(pl.program_id(1) == 0)
    def _():
        st_ref[...] = jnp.zeros_like(st_ref)

    consts = _gate_constants(HG_BLOCK)
    lbv = lb_ref[...]
    e = jnp.exp(lbv - jnp.max(lbv, axis=0, keepdims=True))
    soft = e / jnp.sum(e, axis=0, keepdims=True)
    cum = soft[0:1]
    for i in range(1, layer + 1):
        cum = cum + soft[i:i + 1]
    lb = cum - soft[0:1]
    nw = nw_ref[...]

    def lg_of(rows):
        f = lb + (1.0 - lb) * _sigmoid(_f32(f_ref[rows, :]))
        key_ref[rows, :] = 1.0 - f
        return jnp.log(f)

    def qkv_of(rows):
        q = _silu(_f32(q_ref[rows, :])) * (HG_DK ** -0.5)
        return q, key_ref[rows, :], _f32(i_ref[rows, :])

    def finish(rows, outs):
        g = _f32(g_ref[rows, :])
        for h, o in enumerate(outs):
            hs = slice(h * HG_DK, (h + 1) * HG_DK)
            o_ref[rows, hs] = ((_rms(o) * nw) * _silu(g[:, hs])).astype(BF16)

    _gated_step(lg_of, qkv_of, st_ref, cb_ref, consts, HG_HEADS, HG_DK, HG_DK, HG_BLOCK, n_tiles, finish)


def _col_spec(t, width, name, nt):
    blk = _DST[name] // width
    return pl.BlockSpec((t, width), lambda b, i: (b * nt + i, blk))


def _mixer_out_spec(t, nt):
    return pl.BlockSpec((t, GROUP_WIDTH), lambda b, i: (b * nt + i, 0))


def _mixer_params():
    return pltpu.CompilerParams(dimension_semantics=("arbitrary", "arbitrary"),
                                vmem_limit_bytes=VMEM_LIMIT)


def _gla(proj, w2_pad, b2, nw, batch, seq, t=1024):
    nt = seq // t
    full = lambda shape: pl.BlockSpec(shape, lambda b, i: (0, 0))
    return pl.pallas_call(
        functools.partial(_gla_kernel, n_tiles=t // TILE),
        out_shape=jax.ShapeDtypeStruct((batch * seq, GROUP_WIDTH), BF16),
        grid=(batch, nt),
        in_specs=[_col_spec(t, 256, "a_q", nt), _col_spec(t, 256, "a_k", nt),
                  _col_spec(t, 512, "a_v", nt), _col_spec(t, 512, "a_g", nt),
                  _col_spec(t, LANES, "a_r", nt),
                  full((LANES, GLA_HEADS * GLA_DK)), full((1, GLA_HEADS * GLA_DK)), full((1, GLA_DV))],
        out_specs=_mixer_out_spec(t, nt),
        scratch_shapes=[pltpu.VMEM((GLA_HEADS * GLA_DK // LANES, GLA_DV, LANES), F32),
                        pltpu.VMEM((t, GLA_HEADS * GLA_DK), F32)],
        compiler_params=_mixer_params(),
        name="gla_mixer",
    )(proj, proj, proj, proj, proj, w2_pad, b2.reshape(1, -1), nw.reshape(1, -1))


def _hgrn(proj, hgrn_lb, nw, layer, batch, seq, t=1024):
    nt = seq // t
    full = lambda shape: pl.BlockSpec(shape, lambda b, i: (0, 0))
    return pl.pallas_call(
        functools.partial(_hgrn_kernel, n_tiles=t // TILE, layer=layer),
        out_shape=jax.ShapeDtypeStruct((batch * seq, GROUP_WIDTH), BF16),
        grid=(batch, nt),
        in_specs=[_col_spec(t, 512, "d_q", nt), _col_spec(t, 512, "d_f", nt),
                  _col_spec(t, 512, "d_i", nt), _col_spec(t, 512, "d_g", nt),
                  full(hgrn_lb.shape), full((1, HG_DK))],
        out_specs=_mixer_out_spec(t, nt),
        scratch_shapes=[pltpu.VMEM((HG_HEADS * HG_DK // LANES, HG_DK, LANES), F32),
                        pltpu.VMEM((t, HG_HEADS * HG_DK), F32), pltpu.VMEM((t, HG_HEADS * HG_DK), F32)],
        compiler_params=_mixer_params(),
        name="hgrn_mixer",
    )(proj, proj, proj, proj, hgrn_lb, nw.reshape(1, -1))


def _ret_kernel(q_ref, k_ref, v_ref, g_ref, cos_ref, sin_ref, o_ref, st_ref, *, t):
    @pl.when(pl.program_id(1) == 0)
    def _():
        st_ref[...] = jnp.zeros_like(st_ref)

    cos = cos_ref[0]
    sin = sin_ref[0]
    q = _f32(q_ref[...])
    k = _f32(k_ref[...])
    v = v_ref[...]
    g = _f32(g_ref[...])
    rowm = lax.broadcasted_iota(jnp.int32, (t, t), 0)
    colm = lax.broadcasted_iota(jnp.int32, (t, t), 1)
    rel = (rowm - colm).astype(F32)
    pos = lax.broadcasted_iota(jnp.int32, (t, RET_DK), 0).astype(F32)
    half = RET_DK // 2
    for h in range(RET_HEADS):
        lg = math.log1p(-(2.0 ** (-5.0 - h)))
        hs = slice(h * RET_DK, (h + 1) * RET_DK)
        qh = q[:, hs]
        kh = k[:, hs]
        qr = qh * cos + pltpu.roll(qh, half, axis=1) * sin
        kr = (kh * cos + pltpu.roll(kh, half, axis=1) * sin) * (RET_DK ** -0.5)
        vb = v[:, hs].astype(BF16)
        dmask = jnp.where(rel >= 0, jnp.exp(jnp.minimum(lg * rel, 0.0)), 0.0)
        scores = _dot_nt(qr.astype(BF16), kr.astype(BF16)) * dmask
        st = st_ref[h]
        xi = jnp.exp(lg * (pos + 1.0))
        o = _dot(scores.astype(BF16), vb) + _dot((qr * xi).astype(BF16), st.astype(BF16))
        zeta = jnp.exp(lg * (t - 1.0 - pos))
        st_ref[h] = st * math.exp(lg * t) + _dot_tn((kr * zeta).astype(BF16), vb)
        o_ref[:, hs] = (_rms(o) * _silu(g[:, hs])).astype(BF16)


def _ret(proj, cos, sin, batch, seq, t=256):
    nt = seq // t
    tab = pl.BlockSpec((1, t, RET_DK), lambda b, i: (b, i, 0))
    return pl.pallas_call(
        functools.partial(_ret_kernel, t=t),
        out_shape=jax.ShapeDtypeStruct((batch * seq, GROUP_WIDTH), BF16),
        grid=(batch, nt),
        in_specs=[_col_spec(t, 512, "b_q", nt), _col_spec(t, 512, "b_k", nt),
                  _col_spec(t, 512, "b_v", nt), _col_spec(t, 512, "b_g", nt), tab, tab],
        out_specs=_mixer_out_spec(t, nt),
        scratch_shapes=[pltpu.VMEM((RET_HEADS, RET_DK, RET_DK), F32)],
        compiler_params=_mixer_params(),
        name="ret_mixer",
    )(proj, proj, proj, proj, cos, sin)


def _rope_partial(x, c, s_lo, s_hi):
    n = x.shape[-1]
    half = ROPE_DIM // 2
    return x * c + pltpu.roll(x, n - half, axis=1) * s_lo + pltpu.roll(x, half, axis=1) * s_hi


def _swa_window(q_of, kc_raw, vc, tabs, has_prev, sinks, kprev_ref, vprev_ref, store):
    w = WINDOW
    lane = lax.broadcasted_iota(jnp.int32, (1, LANES), 1)
    lo_half = lane < SWA_HD
    c, s_lo, s_hi = tabs
    kc = _rope_partial(kc_raw, c, s_lo, s_hi)
    group = SWA_Q_HEADS // SWA_KV_HEADS
    rowm = lax.broadcasted_iota(jnp.int32, (group * w, w), 0) & (w - 1)
    colm = lax.broadcasted_iota(jnp.int32, (group * w, w), 1)
    cur_ok = colm <= rowm
    prev_ok = (colm > rowm) & has_prev

    def both_halves(a, kv):
        swapped = pltpu.roll(a, SWA_HD, axis=1)
        return jnp.where(lo_half, a, swapped) if kv == 0 else jnp.where(lo_half, swapped, a)

    for kv in range(SWA_KV_HEADS):
        k2c = both_halves(kc, kv).astype(BF16)
        k2p = kprev_ref[kv]
        v2c = both_halves(vc, kv)
        v_half = [(jnp.where(lo_half, v2c, jnp.where(lane == SWA_HD, 1.0, 0.0)).astype(BF16), vprev_ref[kv, 0]),
                  (jnp.where(lo_half, jnp.where(lane == 0, 1.0, 0.0), v2c).astype(BF16), vprev_ref[kv, 1])]
        ones_lane = (SWA_HD, 0)
        kprev_ref[kv] = k2c
        vprev_ref[kv, 0] = v_half[0][0]
        vprev_ref[kv, 1] = v_half[1][0]
        q_rows, sink_rows = [], []
        for t in range(kv * group // 2, (kv + 1) * group // 2):
            qt = _rope_partial(q_of(t), c, s_lo, s_hi) * (SWA_HD ** -0.5)
            q_rows += [jnp.where(lo_half, qt, 0.0), jnp.where(lo_half, 0.0, qt)]
            sink_rows += [jnp.broadcast_to(sinks[:, 2 * t:2 * t + 1], (w, 1)),
                          jnp.broadcast_to(sinks[:, 2 * t + 1:2 * t + 2], (w, 1))]
        q4 = jnp.concatenate(q_rows, axis=0).astype(BF16)
        sink = jnp.concatenate(sink_rows, axis=0)
        s_c = jnp.where(cur_ok, _dot_nt(q4, k2c), -jnp.inf)
        s_p = jnp.where(prev_ok, _dot_nt(q4, k2p), -jnp.inf)
        m = jnp.maximum(jnp.max(jnp.maximum(s_c, s_p), axis=-1, keepdims=True), sink)
        e_c = jnp.exp(s_c - m).astype(BF16)
        e_p = jnp.exp(s_p - m).astype(BF16)
        e_sink = jnp.exp(sink - m)
        for n, t in enumerate(range(kv * group // 2, (kv + 1) * group // 2)):
            halves = []
            for half in range(2):
                rs = slice((2 * n + half) * w, (2 * n + half + 1) * w)
                vh_c, vh_p = v_half[half]
                o = _dot(e_c[rs], vh_c) + _dot(e_p[rs], vh_p)
                denom = o[:, ones_lane[half]:ones_lane[half] + 1] + e_sink[rs]
                halves.append(o * (1.0 / denom))
            store(t, jnp.where(lo_half, halves[0], halves[1]).astype(BF16))


def _swa_kernel(q_ref, kc_ref, vc_ref, c_ref, slo_ref, shi_ref, sink_ref, o_ref, kprev_ref, vprev_ref, *, t):
    first = pl.program_id(1) == 0

    @pl.when(first)
    def _():
        kprev_ref[...] = jnp.zeros_like(kprev_ref)
        vprev_ref[...] = jnp.zeros_like(vprev_ref)

    sinks = sink_ref[...]
    for i in range(t // WINDOW):
        rows = slice(i * WINDOW, (i + 1) * WINDOW)
        has_prev = jnp.logical_not(first) if i == 0 else True

        def store(tile, value, rows=rows):
            o_ref[rows, tile * LANES:(tile + 1) * LANES] = value

        _swa_window(lambda tile, rows=rows: _f32(q_ref[rows, tile * LANES:(tile + 1) * LANES]),
                    _f32(kc_ref[rows, :]), _f32(vc_ref[rows, :]),
                    (c_ref[0, rows], slo_ref[0, rows], shi_ref[0, rows]), has_prev, sinks,
                    kprev_ref, vprev_ref, store)


def _swa(proj, tabs, sinks, batch, seq, t=512):
    nt = seq // t
    kv_w = SWA_KV_HEADS * SWA_HD
    cur = lambda name: pl.BlockSpec((t, kv_w), lambda b, i: (b * nt + i, _DST[name] // kv_w))
    tab = pl.BlockSpec((1, t, LANES), lambda b, i: (b, i, 0))
    return pl.pallas_call(
        functools.partial(_swa_kernel, t=t),
        out_shape=jax.ShapeDtypeStruct((batch * seq, GROUP_WIDTH), BF16),
        grid=(batch, nt),
        in_specs=[_col_spec(t, 512, "c_q", nt), cur("c_k"), cur("c_v"), tab, tab, tab,
                  pl.BlockSpec((1, SWA_Q_HEADS), lambda b, i: (0, 0))],
        out_specs=_mixer_out_spec(t, nt),
        scratch_shapes=[pltpu.VMEM((SWA_KV_HEADS, WINDOW, LANES), BF16),
                        pltpu.VMEM((SWA_KV_HEADS, 2, WINDOW, LANES), BF16)],
        compiler_params=_mixer_params(),
        name="swa_mixer",
    )(proj, proj, proj, *tabs, sinks.reshape(1, -1))


def _out_proj_kernel(x_ref, mod_ref, a_ref, b_ref, c_ref, d_ref, w_ref, o_ref):
    gw = GROUP_WIDTH
    acc = _dot(a_ref[...], w_ref[0:gw, :])
    acc += _dot(b_ref[...], w_ref[gw:2 * gw, :])
    acc += _dot(c_ref[...], w_ref[2 * gw:3 * gw, :])
    acc += _dot(d_ref[...], w_ref[3 * gw:4 * gw, :])
    gate = mod_ref[0][2:3]
    o_ref[...] = x_ref[...] + gate * acc


def _out_proj(x2, mod_l, mixed, w_out, layer, seq, tm=512):
    m_rows, d = x2.shape
    per_b = seq // tm
    grp = pl.BlockSpec((tm, GROUP_WIDTH), lambda i: (i, 0))
    return pl.pallas_call(
        _out_proj_kernel,
        out_shape=jax.ShapeDtypeStruct((m_rows, d), F32),
        grid=(m_rows // tm,),
        in_specs=[pl.BlockSpec((tm, d), lambda i: (i, 0)),
                  pl.BlockSpec((1, 6, d), lambda i: (i // per_b, 0, 0)),
                  grp, grp, grp, grp,
                  pl.BlockSpec((None, d, d), lambda i: (layer, 0, 0))],
        out_specs=pl.BlockSpec((tm, d), lambda i: (i, 0)),
        compiler_params=pltpu.CompilerParams(
            dimension_semantics=("arbitrary",), vmem_limit_bytes=VMEM_LIMIT),
        name="out_proj",
    )(x2, mod_l, *mixed, w_out)


def _ffn_kernel(x_ref, mod_ref, nw_ref, wg_ref, wu_ref, wd_ref, fw_ref, o_ref, h_ref, *, final):
    f = pl.program_id(1)
    half = h_ref.shape[0] // 2

    @pl.when(f == 0)
    def _():
        m = mod_ref[0]
        h = _modulated_norm(x_ref[...], nw_ref[...], m[3:4], m[4:5])
        h_ref[...] = h.astype(BF16)
        o_ref[...] = jnp.zeros_like(o_ref)

    for r0 in (0, half):
        rows = slice(r0, r0 + half)
        h = h_ref[rows, :]
        g = _dot(h, wg_ref[...])
        u = _dot(h, wu_ref[...])
        act = (_silu(g) * u).astype(BF16)
        o_ref[rows, :] += _dot(act, wd_ref[...])

    @pl.when(f == pl.num_programs(1) - 1)
    def _():
        y = x_ref[...] + mod_ref[0][5:6] * o_ref[...]
        o_ref[...] = _rms(y) * fw_ref[...] if final else y


VMEM_LIMIT_FFN = 56 * 1024 * 1024


def _ffn(x2, mod_l, nw, w_in, w_down, layer, final_w, final, seq, tm=1024, tf=256):
    m_rows, d = x2.shape
    nf = D_FF // tf
    per_b = seq // tm
    return pl.pallas_call(
        functools.partial(_ffn_kernel, final=final),
        out_shape=jax.ShapeDtypeStruct((m_rows, d), F32),
        grid=(m_rows // tm, nf),
        in_specs=[pl.BlockSpec((tm, d), lambda i, f: (i, 0)),
                  pl.BlockSpec((1, 6, d), lambda i, f: (i // per_b, 0, 0)),
                  pl.BlockSpec((1, d), lambda i, f: (0, 0)),
                  pl.BlockSpec((None, d, tf), lambda i, f: (layer, 0, f)),
                  pl.BlockSpec((None, d, tf), lambda i, f: (layer, 0, nf + f)),
                  pl.BlockSpec((None, tf, d), lambda i, f: (layer, f, 0)),
                  pl.BlockSpec((1, d), lambda i, f: (0, 0))],
        out_specs=pl.BlockSpec((tm, d), lambda i, f: (i, 0)),
        scratch_shapes=[pltpu.VMEM((tm, d), BF16)],
        compiler_params=pltpu.CompilerParams(
            dimension_semantics=("arbitrary", "arbitrary"), vmem_limit_bytes=VMEM_LIMIT_FFN),
        name="ffn",
    )(x2, mod_l, nw.reshape(1, d), w_in, w_in, w_down, final_w.reshape(1, d))


CAST_ROWS = 256
CAST_COLS = 2816


def _cast_kernel(x_ref, o_ref):
    o_ref[...] = x_ref[...].astype(BF16)


def _to_bf16(w):
    depth, k, n = w.shape
    tn = n if n <= CAST_COLS else CAST_COLS
    blk = pl.BlockSpec((1, CAST_ROWS, tn), lambda l, i, j: (l, i, j))
    return pl.pallas_call(
        _cast_kernel,
        out_shape=jax.ShapeDtypeStruct(w.shape, BF16),
        grid=(depth, k // CAST_ROWS, n // tn),
        in_specs=[blk], out_specs=blk,
        compiler_params=pltpu.CompilerParams(
            dimension_semantics=("arbitrary",) * 3, vmem_limit_bytes=VMEM_LIMIT),
        name="cast_weights",
    )(w)


def _permute_kernel(x_ref, o_ref):
    off = 0
    for name, width in _DST_ORDER:
        o_ref[0, off:off + width, :] = x_ref[0, _SRC[name]:_SRC[name] + width, :].astype(BF16)
        off += width
    o_ref[0, off:, :] = jnp.zeros((PROJ_WIDTH - off, o_ref.shape[2]), BF16)


def _permute_w_in(w):
    w_t = jnp.swapaxes(w, 1, 2)
    depth, n, k = w_t.shape
    return pl.pallas_call(
        _permute_kernel,
        out_shape=jax.ShapeDtypeStruct((depth, PROJ_WIDTH, k), BF16),
        grid=(depth, k // CAST_ROWS),
        in_specs=[pl.BlockSpec((1, n, CAST_ROWS), lambda l, i: (l, 0, i))],
        out_specs=pl.BlockSpec((1, PROJ_WIDTH, CAST_ROWS), lambda l, i: (l, 0, i)),
        compiler_params=pltpu.CompilerParams(
            dimension_semantics=("arbitrary", "arbitrary"), vmem_limit_bytes=VMEM_LIMIT),
        name="permute_w_in",
    )(w_t)


def _ret_inv_freq():
    inv = 1.0 / jnp.power(RET_ROT_BASE, jnp.linspace(0.0, 1.0, RET_DK // 2, dtype=F32))
    return jnp.concatenate([inv, inv])[None, :]


def _swa_inv_freq():
    half = ROPE_DIM // 2
    inv = 1.0 / jnp.power(ROPE_THETA, jnp.arange(half, dtype=F32) / half)
    per_head = jnp.concatenate([inv, inv, jnp.zeros((SWA_HD - ROPE_DIM,), F32)])
    return jnp.concatenate([per_head] * (LANES // SWA_HD))[None, :]


def _rotary_tables_kernel(pos_ref, rinv_ref, sinv_ref, rcos_ref, rsin_ref, c_ref, slo_ref, shi_ref):
    lane = lax.broadcasted_iota(jnp.int32, (1, LANES), 1)
    pos = pos_ref[0]
    ang = pos * rinv_ref[...]
    sin = jnp.sin(ang)
    rcos_ref[0] = jnp.cos(ang)
    rsin_ref[0] = jnp.where(lane < RET_DK // 2, -sin, sin)
    ang = pos * sinv_ref[...]
    sin = jnp.sin(ang)
    in_head = lane & (SWA_HD - 1)
    c_ref[0] = jnp.cos(ang)
    slo_ref[0] = jnp.where(in_head < ROPE_DIM // 2, -sin, 0.0)
    shi_ref[0] = jnp.where((in_head >= ROPE_DIM // 2) & (in_head < ROPE_DIM), sin, 0.0)


def _rotary_tables(pos, ret_inv, swa_inv, t=512):
    batch, seq, _ = pos.shape
    tab = pl.BlockSpec((1, t, LANES), lambda b, i: (b, i, 0))
    inv = pl.BlockSpec((1, LANES), lambda b, i: (0, 0))
    return pl.pallas_call(
        _rotary_tables_kernel,
        out_shape=[jax.ShapeDtypeStruct((batch, seq, LANES), F32)] * 5,
        grid=(batch, seq // t),
        in_specs=[pl.BlockSpec((1, t, 1), lambda b, i: (b, i, 0)), inv, inv],
        out_specs=[tab] * 5,
        compiler_params=_mixer_params(),
        name="rotary_tables",
    )(pos, ret_inv, swa_inv)


def kernel(x, c, positions, w_ada, b_ada, norm1_w, w_in, gla_gate_w2, gla_gate_b2, gla_norm_w, swa_sinks, hgrn_lb, hgrn_norm_w, w_out, norm2_w, w_ffn_in, w_ffn_down, final_norm_w):
    batch, seq, d = x.shape
    depth = w_ada.shape[0]
    mod = _ada_modulation(c, w_ada, b_ada).reshape(depth, batch, 6, d)
    ret_cos, ret_sin, *swa_tabs = _rotary_tables(positions.astype(F32)[:, :, None], _ret_inv_freq(), _swa_inv_freq())
    x2 = x.reshape(batch * seq, d)
    w_in_b, w_out_b = _permute_w_in(w_in), _to_bf16(w_out)
    w_ffn_in_b, w_ffn_down_b = _to_bf16(w_ffn_in), _to_bf16(w_ffn_down)
    for l in range(depth):
        w2_pad = jnp.zeros((LANES, GLA_HEADS * GLA_DK), F32).at[:GLA_GATE_RANK].set(gla_gate_w2[l])
        proj = _in_proj(x2, mod[l], norm1_w[l], w_in_b, l, seq)
        mixed = (
            _gla(proj, w2_pad, gla_gate_b2[l], gla_norm_w[l], batch, seq),
            _ret(proj, ret_cos, ret_sin, batch, seq),
            _swa(proj, swa_tabs, swa_sinks[l], batch, seq),
            _hgrn(proj, hgrn_lb, hgrn_norm_w[l], l, batch, seq),
        )
        x2 = _out_proj(x2, mod[l], mixed, w_out_b, l, seq)
        x2 = _ffn(x2, mod[l], norm2_w[l], w_ffn_in_b, w_ffn_down_b, l, final_norm_w, l == depth - 1, seq)
    return x2.reshape(batch, seq, d)
```
